```python
import jax, jax.numpy as jnp
from jax import lax
import numpy as np

D_MODEL = 1024
BATCH = 8
SEQ = 2048
DEPTH = 2

CONV_CH = 256
CONV_WIDTH = 31
NSA_HEADS = 4
NSA_HEAD_DIM = 64
CMP_BLOCK = 32
CMP_STRIDE = 16
SEL_BLOCK = 64
SEL_TOPN = 16
WINDOW = 512
MLA_HEADS = 4
MLA_Q_RANK = 256
MLA_KV_RANK = 128
MLA_NOPE = 64
MLA_ROPE = 32
MLA_V = 64
SB_HEADS = 4
SB_HEAD_DIM = 64
N_BRANCH = 4
BRANCH_W = 256
ROPE_THETA = 10000.0
Q_BLOCK = 128
LN_EPS = 1e-5
RMS_EPS = 1e-6
D_FF = 2816
N_EXPERTS = 8
TOP_K = 2
D_FF_EXPERT = 3584
MOE_BLOCK = 512
P_DIM = 256
N_DENSE = (DEPTH + 1) // 2
N_MOE = DEPTH // 2
DEEPNORM_ALPHA = (2 * DEPTH) ** 0.25
DEEPNORM_BETA = (8 * DEPTH) ** -0.25
IN_SIZES = (2 * CONV_CH, NSA_HEADS * NSA_HEAD_DIM, 6 * NSA_HEAD_DIM, 3 * NSA_HEADS,
            MLA_Q_RANK, MLA_KV_RANK, MLA_ROPE, 3 * SB_HEADS * SB_HEAD_DIM, N_BRANCH * D_MODEL)
D_IN = sum(IN_SIZES)

kernel_name = 'hybrid_conv_nsa_mla_stickbreak_moe_deepnorm'

F32 = jnp.float32


def _split_points():
    pts, acc = [], 0
    for s in IN_SIZES[:-1]:
        acc += s
        pts.append(acc)
    return pts


def layer_norm(x, g, b):
    xf = x.astype(F32)
    mu = jnp.mean(xf, axis=-1, keepdims=True)
    var = jnp.mean(jnp.square(xf - mu), axis=-1, keepdims=True)
    return ((xf - mu) * lax.rsqrt(var + LN_EPS) * g.astype(F32) + b.astype(F32)).astype(x.dtype)


def rms_norm(x, g):
    xf = x.astype(F32)
    return (xf * lax.rsqrt(jnp.mean(xf * xf, axis=-1, keepdims=True) + RMS_EPS) * g.astype(F32)).astype(x.dtype)


def rope(x, pos):
    half = x.shape[-1] // 2
    inv = ROPE_THETA ** (-jnp.arange(half, dtype=F32) / half)
    ang = pos.astype(F32)[..., None] * inv
    cos = jnp.cos(ang)[:, :, None, :]
    sin = jnp.sin(ang)[:, :, None, :]
    x1 = x[..., :half].astype(F32)
    x2 = x[..., half:].astype(F32)
    return jnp.concatenate([x1 * cos - x2 * sin, x2 * cos + x1 * sin], axis=-1).astype(x.dtype)


def _to_blocks(a, nb):
    return jnp.moveaxis(a.reshape((a.shape[0], nb, Q_BLOCK) + a.shape[2:]), 1, 0)


def _from_blocks(a):
    a = jnp.moveaxis(a, 0, 1)
    return a.reshape((a.shape[0], -1) + a.shape[3:])


def conformer_conv(u, w_dw, b_dw, ln_g, ln_b):
    a, g = jnp.split(u, 2, axis=-1)
    h = a * jax.nn.sigmoid(g)
    h = lax.conv_general_dilated(h, w_dw[:, None, :], (1,), [(CONV_WIDTH - 1, 0)],
                                 dimension_numbers=('NWC', 'WIO', 'NWC'),
                                 feature_group_count=CONV_CH) + b_dw
    h = layer_norm(h, ln_g, ln_b)
    return jax.nn.silu(h)


def nsa_attention(q, kv, gate_logits, pos, cmp_pe, cmp_w1, cmp_w2):
    B, S, H, Dh = q.shape
    scale = Dh ** -0.5
    t = jnp.arange(S)
    n_cmp = (S - CMP_BLOCK) // CMP_STRIDE + 1
    blk_idx = jnp.arange(n_cmp)[:, None] * CMP_STRIDE + jnp.arange(CMP_BLOCK)[None, :]
    blocks = kv[:, :, 0:2][:, blk_idx] + cmp_pe
    flat = jnp.transpose(blocks, (0, 1, 3, 2, 4)).reshape(B, n_cmp, 2, CMP_BLOCK * Dh)
    hid = jax.nn.gelu(jnp.einsum('bjcf,cfe->bjce', flat, cmp_w1))
    comp = jnp.einsum('bjce,ceo->bjco', hid, cmp_w2)
    k_c, v_c = comp[:, :, 0], comp[:, :, 1]
    cmp_end = jnp.arange(n_cmp) * CMP_STRIDE + CMP_BLOCK - 1
    valid = cmp_end[None, :] <= t[:, None]
    s = jnp.einsum('bthd,bjd->bhtj', q, k_c).astype(F32) * scale
    s = jnp.where(valid, s, -jnp.inf)
    m = jnp.max(s, axis=-1, keepdims=True)
    m = jnp.where(jnp.isfinite(m), m, 0.0)
    e = jnp.exp(s - m)
    den = jnp.sum(e, axis=-1, keepdims=True)
    p_cmp = e / jnp.where(den > 0, den, 1.0)
    o_cmp = jnp.einsum('bhtj,bjd->bthd', p_cmp.astype(v_c.dtype), v_c)
    n_sel = S // SEL_BLOCK
    topn = min(SEL_TOPN, n_sel)
    sel_start = jnp.arange(n_sel) * SEL_BLOCK
    cmp_start = jnp.arange(n_cmp) * CMP_STRIDE
    overlap = ((cmp_start[:, None] < sel_start[None, :] + SEL_BLOCK) &
               (cmp_start[:, None] + CMP_BLOCK > sel_start[None, :])).astype(F32)
    imp = jnp.einsum('bhtj,jn->btn', p_cmp, overlap)
    cur = t // SEL_BLOCK
    n_ids = jnp.arange(n_sel)
    forced = (n_ids[None, :] == 0) | (n_ids[None, :] == cur[:, None]) | (n_ids[None, :] == cur[:, None] - 1)
    imp = jnp.where(forced, jnp.inf, imp)
    imp = jnp.where(n_ids[None, :] > cur[:, None], -jnp.inf, imp)
    top_val, sel_idx = lax.top_k(imp, topn)
    sel_ok = top_val > -jnp.inf
    q_r = rope(q, pos)
    k_s = rope(kv[:, :, 2:3], pos)[:, :, 0]
    k_w = rope(kv[:, :, 4:5], pos)[:, :, 0]
    k_sb = k_s.reshape(B, n_sel, SEL_BLOCK, Dh)
    v_sb = kv[:, :, 3].reshape(B, n_sel, SEL_BLOCK, Dh)
    pad = jnp.zeros((B, WINDOW, Dh), k_w.dtype)
    k_wp = jnp.concatenate([pad, k_w], axis=1)
    v_wp = jnp.concatenate([pad, kv[:, :, 5]], axis=1)
    b_ids = jnp.arange(B)[:, None, None]
    nb = S // Q_BLOCK

    def block(args):
        qb, idx, ok, i = args
        qpos = i * Q_BLOCK + jnp.arange(Q_BLOCK)
        kg = k_sb[b_ids, idx]
        vg = v_sb[b_ids, idx]
        kpos = idx[..., None] * SEL_BLOCK + jnp.arange(SEL_BLOCK)
        msk = ok[..., None] & (kpos <= qpos[None, :, None, None])
        ss = jnp.einsum('bqhd,bqnld->bhqnl', qb, kg).astype(F32) * scale
        ss = jnp.where(msk[:, None], ss, -jnp.inf).reshape(B, H, Q_BLOCK, topn * SEL_BLOCK)
        ps = jax.nn.softmax(ss, axis=-1).astype(vg.dtype).reshape(B, H, Q_BLOCK, topn, SEL_BLOCK)
        o_s = jnp.einsum('bhqnl,bqnld->bqhd', ps, vg)
        start = i * Q_BLOCK
        kw = lax.dynamic_slice_in_dim(k_wp, start, WINDOW + Q_BLOCK, axis=1)
        vw = lax.dynamic_slice_in_dim(v_wp, start, WINDOW + Q_BLOCK, axis=1)
        wpos = start - WINDOW + jnp.arange(WINDOW + Q_BLOCK)
        mw = ((wpos[None, :] <= qpos[:, None]) & (wpos[None, :] > qpos[:, None] - WINDOW) &
              (wpos[None, :] >= 0))
        sw = jnp.einsum('bqhd,bkd->bhqk', qb, kw).astype(F32) * scale
        pw = jax.nn.softmax(jnp.where(mw, sw, -jnp.inf), axis=-1).astype(vw.dtype)
        o_w = jnp.einsum('bhqk,bkd->bqhd', pw, vw)
        return o_s, o_w

    o_s, o_w = lax.map(block, (_to_blocks(q_r, nb), _to_blocks(sel_idx, nb),
                               _to_blocks(sel_ok, nb), jnp.arange(nb)))
    o_s, o_w = _from_blocks(o_s), _from_blocks(o_w)
    g = jax.nn.sigmoid(gate_logits.reshape(B, S, H, 3))
    o = g[..., 0:1] * o_cmp + g[..., 1:2] * o_s + g[..., 2:3] * o_w
    return o.reshape(B, S, H * Dh)


def causal_softmax_attention(q, k, v, scale):
    B, S, H, _ = q.shape
    nb = S // Q_BLOCK
    kpos = jnp.arange(S)

    def block(args):
        qb, i = args
        qpos = i * Q_BLOCK + jnp.arange(Q_BLOCK)
        s = jnp.einsum('bqhd,bkhd->bhqk', qb, k).astype(F32) * scale
        s = jnp.where(kpos[None, :] <= qpos[:, None], s, -jnp.inf)
        p = jax.nn.softmax(s, axis=-1).astype(v.dtype)
        return jnp.einsum('bhqk,bkhd->bqhd', p, v)

    return _from_blocks(lax.map(block, (_to_blocks(q, nb), jnp.arange(nb))))


def mla_attention(q_lat, kv_lat, k_rope, pos, q_norm, kv_norm, w_uq, w_ukv):
    B, S, _ = q_lat.shape
    q = (rms_norm(q_lat, q_norm) @ w_uq).reshape(B, S, MLA_HEADS, MLA_NOPE + MLA_ROPE)
    q = jnp.concatenate([q[..., :MLA_NOPE], rope(q[..., MLA_NOPE:], pos)], axis=-1)
    kv = (rms_norm(kv_lat, kv_norm) @ w_ukv).reshape(B, S, MLA_HEADS, MLA_NOPE + MLA_V)
    k_r = jnp.broadcast_to(rope(k_rope[:, :, None, :], pos), (B, S, MLA_HEADS, MLA_ROPE))
    k = jnp.concatenate([kv[..., :MLA_NOPE], k_r], axis=-1)
    v = kv[..., MLA_NOPE:]
    o = causal_softmax_attention(q, k, v, (MLA_NOPE + MLA_ROPE) ** -0.5)
    return o.reshape(B, S, MLA_HEADS * MLA_V)


def stick_breaking_attention(qkv):
    B, S, _ = qkv.shape
    q, k, v = jnp.split(qkv.reshape(B, S, 3, SB_HEADS, SB_HEAD_DIM), 3, axis=2)
    q, k, v = q[:, :, 0], k[:, :, 0], v[:, :, 0]
    scale = SB_HEAD_DIM ** -0.5
    nb = S // Q_BLOCK
    kpos = jnp.arange(S)

    def block(args):
        qb, i = args
        qpos = i * Q_BLOCK + jnp.arange(Q_BLOCK)
        mask = kpos[None, :] < qpos[:, None]
        z = jnp.einsum('bqhd,bkhd->bhqk', qb, k).astype(F32) * scale
        log_beta = jax.nn.log_sigmoid(z)
        log_keep = jnp.where(mask, jax.nn.log_sigmoid(-z), 0.0)
        later = lax.cumsum(log_keep, axis=3, reverse=True) - log_keep
        a = jnp.where(mask, jnp.exp(log_beta + later), 0.0)
        return jnp.einsum('bhqk,bkhd->bqhd', a.astype(v.dtype), v)

    o = _from_blocks(lax.map(block, (_to_blocks(q, nb), jnp.arange(nb))))
    return o.reshape(B, S, SB_HEADS * SB_HEAD_DIM)


def swiglu(x, w_in, w_out):
    a, u = jnp.split(x @ w_in, 2, axis=-1)
    return (jax.nn.silu(a) * u) @ w_out


def moe_swiglu(x, w_router, w_in, w_out):
    B, S, D = x.shape
    xt = x.reshape(-1, D)
    N = xt.shape[0]
    logits = (xt @ w_router).astype(F32)
    top_val, top_idx = lax.top_k(logits, TOP_K)
    gate = jax.nn.softmax(top_val, axis=-1)
    flat_e = top_idx.reshape(-1)
    flat_tok = jnp.repeat(jnp.arange(N, dtype=jnp.int32), TOP_K)
    flat_w = gate.reshape(-1)
    order = jnp.argsort(flat_e)
    e_sorted = flat_e[order]
    counts = jnp.bincount(flat_e, length=N_EXPERTS)
    padded = ((counts + MOE_BLOCK - 1) // MOE_BLOCK) * MOE_BLOCK
    start = jnp.cumsum(counts) - counts
    start_pad = jnp.cumsum(padded) - padded
    dest = start_pad[e_sorted] + (jnp.arange(N * TOP_K) - start[e_sorted])
    n_rows = ((N * TOP_K + MOE_BLOCK - 1) // MOE_BLOCK) * MOE_BLOCK + N_EXPERTS * MOE_BLOCK
    row_tok = jnp.full((n_rows,), N, jnp.int32).at[dest].set(flat_tok[order])
    row_w = jnp.zeros((n_rows,), F32).at[dest].set(flat_w[order])
    n_blk = n_rows // MOE_BLOCK
    blk_expert = jnp.searchsorted(jnp.cumsum(padded), jnp.arange(n_blk) * MOE_BLOCK, side='right')
    blk_expert = jnp.minimum(blk_expert, N_EXPERTS - 1)
    x_pad = jnp.concatenate([xt, jnp.zeros((1, D), xt.dtype)], axis=0)
    xs = x_pad[row_tok].reshape(n_blk, MOE_BLOCK, D)

    def expert_block(args):
        xb, e = args
        a, u = jnp.split(xb @ w_in[e], 2, axis=-1)
        return (jax.nn.silu(a) * u) @ w_out[e]

    ys = lax.map(expert_block, (xs, blk_expert)).reshape(n_rows, D)
    ys = ys * row_w[:, None].astype(ys.dtype)
    out = jax.ops.segment_sum(ys, row_tok, num_segments=N + 1)[:N]
    return out.reshape(B, S, D)


def setup_inputs(seed: int = 0) -> dict:
    key = jax.random.key(seed)
    ks = iter(jax.random.split(key, 40))
    nrm = lambda shape, s: jax.random.normal(next(ks), shape, F32) * s
    gain = lambda shape: 1.0 + 0.01 * jax.random.normal(next(ks), shape, F32)
    x = jax.random.normal(next(ks), (BATCH, SEQ, D_MODEL), F32)
    p = jax.random.normal(next(ks), (DEPTH, BATCH, SEQ, P_DIM), F32)
    offset = jax.random.randint(next(ks), (BATCH, 1), 0, 4096, dtype=jnp.int32)
    positions = offset + jnp.arange(SEQ, dtype=jnp.int32)[None, :]
    return {
        'x': x, 'p': p, 'positions': positions,
        'w_in': nrm((DEPTH, D_MODEL, D_IN), D_MODEL ** -0.5),
        'conv_w': nrm((DEPTH, CONV_WIDTH, CONV_CH), CONV_WIDTH ** -0.5),
        'conv_b': nrm((DEPTH, CONV_CH), 0.01),
        'conv_ln_g': gain((DEPTH, CONV_CH)),
        'conv_ln_b': nrm((DEPTH, CONV_CH), 0.01),
        'nsa_cmp_pe': nrm((DEPTH, CMP_BLOCK, 2, NSA_HEAD_DIM), 0.02),
        'nsa_cmp_w1': nrm((DEPTH, 2, CMP_BLOCK * NSA_HEAD_DIM, NSA_HEAD_DIM), (CMP_BLOCK * NSA_HEAD_DIM) ** -0.5),
        'nsa_cmp_w2': nrm((DEPTH, 2, NSA_HEAD_DIM, NSA_HEAD_DIM), NSA_HEAD_DIM ** -0.5),
        'mla_q_norm': gain((DEPTH, MLA_Q_RANK)),
        'mla_kv_norm': gain((DEPTH, MLA_KV_RANK)),
        'mla_w_uq': nrm((DEPTH, MLA_Q_RANK, MLA_HEADS * (MLA_NOPE + MLA_ROPE)), MLA_Q_RANK ** -0.5),
        'mla_w_ukv': nrm((DEPTH, MLA_KV_RANK, MLA_HEADS * (MLA_NOPE + MLA_V)), MLA_KV_RANK ** -0.5),
        'w_branch': nrm((DEPTH, N_BRANCH, BRANCH_W, D_MODEL), BRANCH_W ** -0.5),
        'w_out': nrm((DEPTH, D_MODEL, D_MODEL), D_MODEL ** -0.5 * DEEPNORM_BETA),
        'ln1_g': gain((DEPTH, D_MODEL)),
        'ln1_b': nrm((DEPTH, D_MODEL), 0.01),
        'ffn_w_in': nrm((N_DENSE, D_MODEL, 2 * D_FF), D_MODEL ** -0.5),
        'ffn_w_out': nrm((N_DENSE, D_FF, D_MODEL), D_FF ** -0.5 * DEEPNORM_BETA),
        'moe_router': nrm((N_MOE, D_MODEL, N_EXPERTS), D_MODEL ** -0.5),
        'moe_w_in': nrm((N_MOE, N_EXPERTS, D_MODEL, 2 * D_FF_EXPERT), D_MODEL ** -0.5),
        'moe_w_out': nrm((N_MOE, N_EXPERTS, D_FF_EXPERT, D_MODEL), D_FF_EXPERT ** -0.5 * DEEPNORM_BETA),
        'ple_w_gate': nrm((DEPTH, D_MODEL, D_MODEL), D_MODEL ** -0.5),
        'ple_w_proj': nrm((DEPTH, P_DIM, D_MODEL), P_DIM ** -0.5 * DEEPNORM_BETA),
        'ln2_g': gain((DEPTH, D_MODEL)),
        'ln2_b': nrm((DEPTH, D_MODEL), 0.01),
    }


def reference(x, p, positions, w_in, conv_w, conv_b, conv_ln_g, conv_ln_b, nsa_cmp_pe, nsa_cmp_w1,
              nsa_cmp_w2, mla_q_norm, mla_kv_norm, mla_w_uq, mla_w_ukv, w_branch, w_out, ln1_g, ln1_b,
              ffn_w_in, ffn_w_out, moe_router, moe_w_in, moe_w_out, ple_w_gate, ple_w_proj, ln2_g, ln2_b):
    B, S, _ = x.shape
    pts = _split_points()
    for i in range(DEPTH):
        u = x @ w_in[i]
        c_glu, nq, nkv, ng, mq, mkv, mkr, sbqkv, bg = jnp.split(u, pts, axis=-1)
        y_a = conformer_conv(c_glu, conv_w[i], conv_b[i], conv_ln_g[i], conv_ln_b[i])
        y_b = nsa_attention(nq.reshape(B, S, NSA_HEADS, NSA_HEAD_DIM),
                            nkv.reshape(B, S, 6, NSA_HEAD_DIM), ng, positions,
                            nsa_cmp_pe[i], nsa_cmp_w1[i], nsa_cmp_w2[i])
        y_c = mla_attention(mq, mkv, mkr, positions, mla_q_norm[i], mla_kv_norm[i],
                            mla_w_uq[i], mla_w_ukv[i])
        y_d = stick_breaking_attention(sbqkv)
        branches = jnp.stack([y_a, y_b, y_c, y_d], axis=2)
        proj = jnp.einsum('bsnc,ncd->bsnd', branches, w_branch[i])
        gates = jax.nn.sigmoid(bg.reshape(B, S, N_BRANCH, D_MODEL))
        mixed = jnp.einsum('bsnd,bsnd->bsd', gates, proj) @ w_out[i]
        x = layer_norm(DEEPNORM_ALPHA * x + mixed, ln1_g[i], ln1_b[i])
        if i % 2 == 0:
            f = swiglu(x, ffn_w_in[i // 2], ffn_w_out[i // 2])
        else:
            f = moe_swiglu(x, moe_router[i // 2], moe_w_in[i // 2], moe_w_out[i // 2])
        ple = jax.nn.sigmoid(x @ ple_w_gate[i]) * (p[i] @ ple_w_proj[i])
        x = layer_norm(DEEPNORM_ALPHA * x + f + ple, ln2_g[i], ln2_b[i])
    return x
```

```python
import functools

import jax
import jax.numpy as jnp
from jax import lax
from jax.experimental import pallas as pl
from jax.experimental.pallas import tpu as pltpu

F32 = jnp.float32
BF16 = jnp.bfloat16

D_MODEL = 1024
DEPTH = 2
CONV_CH = 256
CONV_WIDTH = 31
NSA_HEADS = 4
HEAD_DIM = 64
CMP_BLOCK = 32
CMP_STRIDE = 16
SEL_BLOCK = 64
SEL_TOPN = 16
WINDOW = 512
MLA_HEADS = 4
MLA_Q_RANK = 256
MLA_KV_RANK = 128
MLA_NOPE = 64
MLA_ROPE = 32
MLA_V = 64
N_BRANCH = 4
BRANCH_W = 256
ROPE_THETA = 10000.0
LN_EPS = 1e-5
RMS_EPS = 1e-6
D_FF = 2816
N_EXPERTS = 8
D_FF_EXPERT = 3584
MOE_BLOCK = 512
P_DIM = 256
DEEPNORM_ALPHA = (2 * DEPTH) ** 0.25

LANES = 128
VMEM_LIMIT = 56 * 1024 * 1024

NEG = -1e30

C_CONV = 0
C_NQ = 512
C_NQR = 768
C_KS = 1024
C_KSR = 1152
C_KW = 1280
C_KWR = 1408
C_VS = 1536
C_VW = 1664
C_KVC = 1792
C_NG = 1920
C_MQ = 2048
C_MKV = 2304
C_MKR = 2432
C_MKRR = 2560
C_SB = 2688
C_TOT = 3456

TM = 256
TQ = 256
TK = 128


def _cparams(sem, vmem=VMEM_LIMIT):
    return pltpu.CompilerParams(dimension_semantics=sem, vmem_limit_bytes=vmem)


def _const_spec(shape):
    nd = len(shape)
    return pl.BlockSpec(shape, lambda *_: (0,) * nd, pipeline_mode=pl.Buffered(1))


def _dot(a, b):
    return jnp.dot(a, b, preferred_element_type=F32)


def _dot_nt(a, b):
    return lax.dot_general(a, b, (((1,), (1,)), ((), ())), preferred_element_type=F32)


def _layer_norm(h, g, b):
    mu = jnp.mean(h, axis=-1, keepdims=True)
    d = h - mu
    var = jnp.mean(d * d, axis=-1, keepdims=True)
    return d * lax.rsqrt(var + LN_EPS) * g + b


def _rms_norm(h, g):
    return h * lax.rsqrt(jnp.mean(h * h, axis=-1, keepdims=True) + RMS_EPS) * g


def _sigmoid(x):
    return 1.0 / (1.0 + jnp.exp(-x))


def _silu(x):
    return x * _sigmoid(x)


def _split_bf16(x):
    hi = x.astype(BF16)
    lo = (x - hi.astype(F32)).astype(BF16)
    return hi, lo


def _half_select(sub, x):
    lane = lax.broadcasted_iota(jnp.int32, x.shape, 1)
    keep = (lane < HEAD_DIM) if sub == 0 else (lane >= HEAD_DIM)
    return jnp.where(keep, x, 0.0)


def _mixer_in_kernel(x_ref, pos_ref, w_ref, invn_ref, invm_ref, qn_ref, kvn_ref, wq_ref, wqr_ref,
                     wk_ref, wv_ref,
                     conv_ref, nq_ref, nqr_ref, ks_ref, kw_ref, vs_ref, vw_ref, kvc_ref, ng_ref,
                     mq_ref, mk_ref, mv_ref, sq_ref, sk_ref, sv_ref):
    xb = x_ref[...].astype(BF16)

    def proj(c0, width):
        return _dot(xb, w_ref[:, c0:c0 + width])

    posf = pos_ref[...].astype(F32)
    ang_n = posf * invn_ref[...]
    cos_n, sin_n = jnp.cos(ang_n), jnp.sin(ang_n)
    ang_m = posf * invm_ref[...]
    cos_m, sin_m = jnp.cos(ang_m), jnp.sin(ang_m)

    conv_ref[...] = proj(C_CONV, 512)

    scale = HEAD_DIM ** -0.5
    for half in range(2):
        q = proj(C_NQ + half * LANES, LANES)
        qrot = proj(C_NQR + half * LANES, LANES)
        nq_ref[:, half * LANES:(half + 1) * LANES] = (q * scale).astype(BF16)
        nqr_ref[:, half * LANES:(half + 1) * LANES] = ((q * cos_n + qrot * sin_n) * scale).astype(BF16)
    ks_ref[...] = (proj(C_KS, LANES) * cos_n + proj(C_KSR, LANES) * sin_n).astype(BF16)
    kw_ref[...] = (proj(C_KW, LANES) * cos_n + proj(C_KWR, LANES) * sin_n).astype(BF16)
    vs_ref[...] = proj(C_VS, LANES).astype(BF16)
    vw_ref[...] = proj(C_VW, LANES).astype(BF16)
    kvc_ref[...] = proj(C_KVC, LANES)
    ng_ref[...] = proj(C_NG, LANES)

    qn = _rms_norm(proj(C_MQ, MLA_Q_RANK), qn_ref[...]).astype(BF16)
    kvn = _rms_norm(proj(C_MKV, MLA_KV_RANK), kvn_ref[...]).astype(BF16)
    kr = proj(C_MKR, LANES) * cos_m + proj(C_MKRR, LANES) * sin_m
    for h in range(MLA_HEADS):
        sl = slice(h * LANES, (h + 1) * LANES)
        qa = _dot(qn, wq_ref[:, sl])
        qr = _dot(qn, wqr_ref[:, sl])
        mq_ref[:, sl] = (qa * cos_m + qr * sin_m).astype(BF16)
        mk_ref[:, sl] = (_dot(kvn, wk_ref[:, sl]) + kr).astype(BF16)
    mv_ref[...] = _dot(kvn, wv_ref[...]).astype(BF16)

    sq_ref[...] = (proj(C_SB, 256) * scale).astype(BF16)
    sk_ref[...] = proj(C_SB + 256, 256).astype(BF16)
    sv_ref[...] = proj(C_SB + 512, 256).astype(BF16)


def _mixer_in(x2d, pos2d, wts):
    n = x2d.shape[0]
    row = lambda w: pl.BlockSpec((TM, w), lambda i: (i, 0))
    out_widths = [512, 256, 256, 128, 128, 128, 128, 128, 128, 512, 512, 256, 256, 256, 256]
    out_dtypes = [F32, BF16, BF16, BF16, BF16, BF16, BF16, F32, F32, BF16, BF16, BF16, BF16, BF16, BF16]
    consts = [wts['w1'], wts['inv_nsa'], wts['inv_mla'], wts['mla_qn'], wts['mla_kvn'], wts['wq'],
              wts['wqr'], wts['wk'], wts['wv']]
    return pl.pallas_call(
        _mixer_in_kernel,
        grid=(n // TM,),
        in_specs=[row(D_MODEL), row(1)] + [_const_spec(c.shape) for c in consts],
        out_specs=[row(w) for w in out_widths],
        out_shape=[jax.ShapeDtypeStruct((n, w), d) for w, d in zip(out_widths, out_dtypes)],
        compiler_params=_cparams(("parallel",)),
        name="mixer_in",
    )(x2d, pos2d, *consts)


CONV_PAD = 32
CONV_CHUNK = 128


def _conv_kernel(u_ref, w_ref, b_ref, g_ref, beta_ref, o_ref, hp_ref):
    seq = u_ref.shape[0]
    hp_ref[0:CONV_PAD, :] = jnp.zeros((CONV_PAD, CONV_CH), F32)
    hp_ref[CONV_PAD:CONV_PAD + seq, :] = u_ref[:, 0:CONV_CH] * _sigmoid(u_ref[:, CONV_CH:2 * CONV_CH])
    shift = CONV_PAD - (CONV_WIDTH - 1)
    for c in range(seq // CONV_CHUNK):
        base = c * CONV_CHUNK
        acc = jnp.broadcast_to(b_ref[...], (CONV_CHUNK, CONV_CH))
        for j in range(CONV_WIDTH):
            acc = acc + hp_ref[base + shift + j:base + shift + j + CONV_CHUNK, :] * w_ref[j:j + 1, :]
        y = _layer_norm(acc, g_ref[...], beta_ref[...])
        o_ref[base:base + CONV_CHUNK, :] = _silu(y).astype(BF16)


def _conv(conv_in, w, b, g, beta, batch, seq):
    return pl.pallas_call(
        _conv_kernel,
        grid=(batch,),
        in_specs=[pl.BlockSpec((None, seq, 2 * CONV_CH), lambda i: (i, 0, 0)),
                  _const_spec(w.shape), _const_spec(b.shape), _const_spec(g.shape), _const_spec(beta.shape)],
        out_specs=pl.BlockSpec((None, seq, CONV_CH), lambda i: (i, 0, 0)),
        out_shape=jax.ShapeDtypeStruct((batch, seq, CONV_CH), BF16),
        scratch_shapes=[pltpu.VMEM((CONV_PAD + seq, CONV_CH), F32)],
        compiler_params=_cparams(("parallel",)),
        name="conformer_conv",
    )(conv_in.reshape(batch, seq, 2 * CONV_CH), w, b, g, beta)


def _gelu_tanh(x):
    return 0.5 * x * (1.0 + jnp.tanh(0.7978845608028654 * (x + 0.044715 * x * x * x)))


def _nsa_cmp_kernel(kvc_ref, q_ref, pea_ref, peb_ref, w1a_ref, w1b_ref, w2k_ref, w2v_ref, ov_ref,
                    ocmp_ref, sel_ref, *, seq):
    n_cmp = (seq - CMP_BLOCK) // CMP_STRIDE + 1
    nb = seq // CMP_STRIDE
    x2 = kvc_ref[...]
    xa = (x2 + pea_ref[...]).astype(BF16)
    xb = (x2 + peb_ref[...]).astype(BF16)
    ha = _dot(xa, w1a_ref[...])
    hb = _dot(xb, w1b_ref[...])
    hid = ha + pltpu.roll(hb, nb - 1, 0)
    hid = _gelu_tanh(hid).astype(BF16)
    kk = _dot(hid, w2k_ref[...]).astype(BF16)
    vv = _dot(hid, w2v_ref[...]).astype(BF16)
    ov = ov_ref[...]

    for c in range(seq // TQ):
        r0 = c * TQ
        t = r0 + lax.broadcasted_iota(jnp.int32, (TQ, LANES), 0)
        j = lax.broadcasted_iota(jnp.int32, (TQ, LANES), 1)
        valid = (j * CMP_STRIDE + CMP_BLOCK - 1 <= t) & (j < n_cmp)
        psum = jnp.zeros((TQ, LANES), F32)
        for pair in range(2):
            qp = q_ref[r0:r0 + TQ, pair * LANES:(pair + 1) * LANES].astype(F32)
            outs = []
            for sub in range(2):
                qm = _half_select(sub, qp).astype(BF16)
                s = jnp.where(valid, _dot_nt(qm, kk), NEG)
                m = jnp.max(s, axis=-1, keepdims=True)
                e = jnp.where(valid, jnp.exp(s - m), 0.0)
                den = jnp.sum(e, axis=-1, keepdims=True)
                p = e / jnp.where(den > 0, den, 1.0)
                psum = psum + p
                outs.append(_dot(p.astype(BF16), vv))
            ocmp_ref[r0:r0 + TQ, pair * LANES:(pair + 1) * LANES] = jnp.where(
                lax.broadcasted_iota(jnp.int32, (TQ, LANES), 1) < HEAD_DIM, outs[0], outs[1])
        p_hi, p_lo = _split_bf16(psum)
        imp = _dot(p_hi, ov) + _dot(p_lo, ov)
        cur = t // SEL_BLOCK
        forced = (j == 0) | (j == cur) | (j == cur - 1)
        imp = jnp.where(forced, jnp.inf, imp)
        imp = jnp.where(j > cur, -jnp.inf, imp)
        rank = jnp.zeros((TQ, LANES), F32)
        for n2 in range(seq // SEL_BLOCK):
            col = imp[:, n2:n2 + 1]
            ahead = (col > imp) | ((col == imp) & (n2 < j))
            rank = rank + jnp.where(ahead, 1.0, 0.0)
        sel = (rank < SEL_TOPN) & (imp > -jnp.inf)
        sel_ref[r0:r0 + TQ, :] = jnp.where(sel, 1.0, 0.0).astype(BF16)


def _nsa_cmp(kvc, nq, wts, batch, seq):
    nb = seq // CMP_STRIDE
    consts = [wts['pe_a'], wts['pe_b'], wts['w1a'], wts['w1b'], wts['w2k'], wts['w2v'], wts['overlap']]
    return pl.pallas_call(
        functools.partial(_nsa_cmp_kernel, seq=seq),
        grid=(batch,),
        in_specs=[pl.BlockSpec((None, nb, CMP_STRIDE * LANES), lambda i: (i, 0, 0)),
                  pl.BlockSpec((None, seq, 256), lambda i: (i, 0, 0))] + [_const_spec(c.shape) for c in consts],
        out_specs=[pl.BlockSpec((None, seq, 256), lambda i: (i, 0, 0)),
                   pl.BlockSpec((None, seq, LANES), lambda i: (i, 0, 0))],
        out_shape=[jax.ShapeDtypeStruct((batch, seq, 256), F32),
                   jax.ShapeDtypeStruct((batch, seq, LANES), BF16)],
        compiler_params=_cparams(("parallel",)),
        name="nsa_compress_select",
    )(kvc.reshape(batch, nb, CMP_STRIDE * LANES), nq.reshape(batch, seq, 256), *consts)


def _softmax_step(s, mask, v_blk, carry):
    m, l, acc = carry
    s = jnp.where(mask, s, NEG)
    m_new = jnp.maximum(m, jnp.max(s, axis=-1, keepdims=True))
    alpha = jnp.exp(m - m_new)
    p = jnp.where(mask, jnp.exp(s - m_new), 0.0)
    l = alpha * l + jnp.sum(p, axis=-1, keepdims=True)
    acc = alpha * acc + _dot(p.astype(BF16), v_blk)
    return m_new, l, acc


def _softmax_init():
    return (jnp.full((TQ, 1), NEG, F32), jnp.zeros((TQ, 1), F32), jnp.zeros((TQ, LANES), F32))


def _nsa_attn_kernel(q_ref, ks_ref, vs_ref, kw_ref, vw_ref, sel_ref, exp_ref, ocmp_ref, ng_ref, o_ref,
                     mask_ref):
    i = pl.program_id(1)
    n_kb = (i + 1) * (TQ // TK)
    qpos = i * TQ + lax.broadcasted_iota(jnp.int32, (TQ, TK), 0)
    lane = lax.broadcasted_iota(jnp.int32, (TQ, TK), 1)
    sel = sel_ref[...]

    def build_mask(kb, _):
        hit = _dot(sel, exp_ref[kb])
        kpos = kb * TK + lane
        mask_ref[kb] = jnp.where((hit > 0.5) & (kpos <= qpos), 1.0, 0.0)
        return 0

    lax.fori_loop(0, n_kb, build_mask, 0)

    g = _sigmoid(ng_ref[...])
    kb_lo = jnp.maximum(n_kb - (WINDOW // TK + TQ // TK), 0)
    for pair in range(2):
        qp = q_ref[:, pair * LANES:(pair + 1) * LANES].astype(F32)
        res = []
        for sub in range(2):
            h = 2 * pair + sub
            qm = _half_select(sub, qp).astype(BF16)

            def sel_body(kb, carry):
                k0 = pl.multiple_of(kb * TK, TK)
                s = _dot_nt(qm, ks_ref[pl.ds(k0, TK), :])
                return _softmax_step(s, mask_ref[kb] > 0.5, vs_ref[pl.ds(k0, TK), :], carry)

            _, l_s, acc_s = lax.fori_loop(0, n_kb, sel_body, _softmax_init())

            def win_body(kb, carry):
                k0 = pl.multiple_of(kb * TK, TK)
                kpos = kb * TK + lane
                mask = (kpos <= qpos) & (kpos > qpos - WINDOW)
                s = _dot_nt(qm, kw_ref[pl.ds(k0, TK), :])
                return _softmax_step(s, mask, vw_ref[pl.ds(k0, TK), :], carry)

            _, l_w, acc_w = lax.fori_loop(kb_lo, n_kb, win_body, _softmax_init())
            res.append(g[:, 3 * h + 1:3 * h + 2] * (acc_s / l_s) + g[:, 3 * h + 2:3 * h + 3] * (acc_w / l_w)
                       + g[:, 3 * h:3 * h + 1] * ocmp_ref[:, pair * LANES:(pair + 1) * LANES])
        o_ref[:, pair * LANES:(pair + 1) * LANES] = jnp.where(lane < HEAD_DIM, res[0], res[1]).astype(BF16)


def _nsa_attn(nqr, ks, vs, kw, vw, sel, expand, ocmp, ng, batch, seq):
    qspec = lambda w: pl.BlockSpec((None, TQ, w), lambda b, i: (b, i, 0))
    kspec = pl.BlockSpec((None, seq, LANES), lambda b, i: (b, 0, 0))
    r3 = lambda a: a.reshape(batch, seq, a.shape[-1])
    return pl.pallas_call(
        _nsa_attn_kernel,
        grid=(batch, seq // TQ),
        in_specs=[qspec(256), kspec, kspec, kspec, kspec, qspec(LANES), _const_spec(expand.shape),
                  qspec(256), qspec(LANES)],
        out_specs=qspec(256),
        out_shape=jax.ShapeDtypeStruct((batch, seq, 256), BF16),
        scratch_shapes=[pltpu.VMEM((seq // TK, TQ, TK), F32)],
        compiler_params=_cparams(("parallel", "parallel")),
        name="nsa_select_window",
    )(r3(nqr), r3(ks), r3(vs), r3(kw), r3(vw), sel, expand, ocmp, r3(ng))


def _mla_attn_kernel(q_ref, k_ref, v_ref, o_ref):
    i = pl.program_id(1)
    n_kb = (i + 1) * (TQ // TK)
    scale = (MLA_NOPE + MLA_ROPE) ** -0.5
    qpos = i * TQ + lax.broadcasted_iota(jnp.int32, (TQ, TK), 0)
    lane = lax.broadcasted_iota(jnp.int32, (TQ, TK), 1)
    for pair in range(2):
        res = []
        for sub in range(2):
            h = 2 * pair + sub
            qh = q_ref[:, h * LANES:(h + 1) * LANES]

            def body(kb, carry):
                k0 = pl.multiple_of(kb * TK, TK)
                s = _dot_nt(qh, k_ref[pl.ds(k0, TK), h * LANES:(h + 1) * LANES]) * scale
                mask = kb * TK + lane <= qpos
                return _softmax_step(s, mask, v_ref[pl.ds(k0, TK), pair * LANES:(pair + 1) * LANES], carry)

            _, l, acc = lax.fori_loop(0, n_kb, body, _softmax_init())
            res.append(acc / l)
        o_ref[:, pair * LANES:(pair + 1) * LANES] = jnp.where(lane < HEAD_DIM, res[0], res[1]).astype(BF16)


def _mla_attn(mq, mk, mv, batch, seq):
    r3 = lambda a: a.reshape(batch, seq, a.shape[-1])
    return pl.pallas_call(
        _mla_attn_kernel,
        grid=(batch, seq // TQ),
        in_specs=[pl.BlockSpec((None, TQ, 512), lambda b, i: (b, i, 0)),
                  pl.BlockSpec((None, seq, 512), lambda b, i: (b, 0, 0)),
                  pl.BlockSpec((None, seq, 256), lambda b, i: (b, 0, 0))],
        out_specs=pl.BlockSpec((None, TQ, 256), lambda b, i: (b, i, 0)),
        out_shape=jax.ShapeDtypeStruct((batch, seq, 256), BF16),
        compiler_params=_cparams(("parallel", "parallel")),
        name="mla_attention",
    )(r3(mq), r3(mk), r3(mv))


def _sb_attn_kernel(q_ref, k_ref, v_ref, tri_ref, o_ref):
    i = pl.program_id(1)
    n_kb = (i + 1) * (TQ // TK)
    qpos = i * TQ + lax.broadcasted_iota(jnp.int32, (TQ, TK), 0)
    lane = lax.broadcasted_iota(jnp.int32, (TQ, TK), 1)
    tri = tri_ref[...]
    for pair in range(2):
        psl = slice(pair * LANES, (pair + 1) * LANES)
        qp = q_ref[:, psl].astype(F32)
        res = []
        for sub in range(2):
            qm = _half_select(sub, qp).astype(BF16)

            def body(step, carry):
                tail, acc = carry
                kb = n_kb - 1 - step
                k0 = pl.multiple_of(kb * TK, TK)
                z = _dot_nt(qm, k_ref[pl.ds(k0, TK), psl])
                log_beta = jnp.minimum(z, 0.0) - jnp.log1p(jnp.exp(-jnp.abs(z)))
                mask = kb * TK + lane < qpos
                log_keep = jnp.where(mask, log_beta - z, 0.0)
                lk_hi, lk_lo = _split_bf16(log_keep)
                incl = _dot(lk_hi, tri) + _dot(lk_lo, tri)
                a = jnp.where(mask, jnp.exp(log_beta + (incl - log_keep) + tail), 0.0)
                acc = acc + _dot(a.astype(BF16), v_ref[pl.ds(k0, TK), psl])
                return tail + incl[:, 0:1], acc

            _, acc = lax.fori_loop(0, n_kb, body, (jnp.zeros((TQ, 1), F32), jnp.zeros((TQ, LANES), F32)))
            res.append(acc)
        o_ref[:, psl] = jnp.where(lane < HEAD_DIM, res[0], res[1]).astype(BF16)


def _sb_attn(sq, sk, sv, tri, batch, seq):
    r3 = lambda a: a.reshape(batch, seq, a.shape[-1])
    kspec = pl.BlockSpec((None, seq, 256), lambda b, i: (b, 0, 0))
    qspec = pl.BlockSpec((None, TQ, 256), lambda b, i: (b, i, 0))
    return pl.pallas_call(
        _sb_attn_kernel,
        grid=(batch, seq // TQ),
        in_specs=[qspec, kspec, kspec, _const_spec(tri.shape)],
        out_specs=qspec,
        out_shape=jax.ShapeDtypeStruct((batch, seq, 256), BF16),
        compiler_params=_cparams(("parallel", "parallel")),
        name="stick_breaking_attention",
    )(r3(sq), r3(sk), r3(sv), tri)


def _merge_kernel(x_ref, ya_ref, yb_ref, yc_ref, yd_ref, wg_ref, wb_ref, wo_ref, g_ref, b_ref,
                  o_ref, ob_ref):
    x = x_ref[...]
    xb = x.astype(BF16)
    mixed = jnp.zeros((TM, D_MODEL), F32)
    for n, y_ref in enumerate((ya_ref, yb_ref, yc_ref, yd_ref)):
        gate = _sigmoid(_dot(xb, wg_ref[:, n * D_MODEL:(n + 1) * D_MODEL]))
        mixed = mixed + gate * _dot(y_ref[...], wb_ref[n])
    h = DEEPNORM_ALPHA * x + _dot(mixed.astype(BF16), wo_ref[...])
    out = _layer_norm(h, g_ref[...], b_ref[...])
    o_ref[...] = out
    ob_ref[...] = out.astype(BF16)


def _merge(x2d, ys, wts):
    n = x2d.shape[0]
    row = lambda w: pl.BlockSpec((TM, w), lambda i: (i, 0))
    consts = [wts['w_gate'], wts['w_branch'], wts['w_out'], wts['ln1_g'], wts['ln1_b']]
    return pl.pallas_call(
        _merge_kernel,
        grid=(n // TM,),
        in_specs=[row(D_MODEL)] + [row(BRANCH_W)] * 4 + [_const_spec(c.shape) for c in consts],
        out_specs=[row(D_MODEL), row(D_MODEL)],
        out_shape=[jax.ShapeDtypeStruct((n, D_MODEL), F32), jax.ShapeDtypeStruct((n, D_MODEL), BF16)],
        compiler_params=_cparams(("parallel",)),
        name="branch_merge_ln1",
    )(x2d, *ys, *consts)


def _ple_ln2(x1, x1b, f, p_ref, wpg_ref, wpp_ref, g_ref, b_ref):
    ple = _sigmoid(_dot(x1b, wpg_ref[...])) * _dot(p_ref[...].astype(BF16), wpp_ref[...])
    return _layer_norm(DEEPNORM_ALPHA * x1 + f + ple, g_ref[...], b_ref[...])


FF_CHUNK = 256


def _ffn_dense_kernel(x_ref, xb_ref, p_ref, wi_ref, wo_ref, wpg_ref, wpp_ref, g_ref, b_ref, o_ref, acc_ref):
    xb = xb_ref[...]
    for c in range(D_FF // FF_CHUNK):
        a = _dot(xb, wi_ref[:, c * FF_CHUNK:(c + 1) * FF_CHUNK])
        u = _dot(xb, wi_ref[:, D_FF + c * FF_CHUNK:D_FF + (c + 1) * FF_CHUNK])
        part = _dot((_silu(a) * u).astype(BF16), wo_ref[c * FF_CHUNK:(c + 1) * FF_CHUNK, :])
        if c == 0:
            acc_ref[...] = part
        else:
            acc_ref[...] += part
    o_ref[...] = _ple_ln2(x_ref[...], xb, acc_ref[...], p_ref, wpg_ref, wpp_ref, g_ref, b_ref)


def _ffn_dense(x1, x1b, p2d, wts):
    n = x1.shape[0]
    row = lambda w: pl.BlockSpec((TM, w), lambda i: (i, 0))
    consts = [wts['ffn_w_in'], wts['ffn_w_out'], wts['ple_w_gate'], wts['ple_w_proj'], wts['ln2_g'], wts['ln2_b']]
    return pl.pallas_call(
        _ffn_dense_kernel,
        grid=(n // TM,),
        in_specs=[row(D_MODEL), row(D_MODEL), row(P_DIM)] + [_const_spec(c.shape) for c in consts],
        out_specs=row(D_MODEL),
        out_shape=jax.ShapeDtypeStruct((n, D_MODEL), F32),
        scratch_shapes=[pltpu.VMEM((TM, D_MODEL), F32)],
        compiler_params=_cparams(("parallel",)),
        name="ffn_dense_ple_ln2",
    )(x1, x1b, p2d, *consts)


TR = 512
INFO_LANES = 6


def _moe_route_kernel(x_ref, wr_ref, tri_ref, info_ref, cnt_ref, run_ref):
    @pl.when(pl.program_id(0) == 0)
    def _():
        run_ref[...] = jnp.zeros_like(run_ref)

    logits = jnp.dot(x_ref[...], wr_ref[...], precision=lax.Precision.HIGHEST, preferred_element_type=F32)
    lane = lax.broadcasted_iota(jnp.int32, (TR, LANES), 1)
    lane_f = lane.astype(F32)
    logits = jnp.where(lane < N_EXPERTS, logits, NEG)
    m1 = jnp.max(logits, axis=-1, keepdims=True)
    i1 = jnp.min(jnp.where(logits == m1, lane_f, float(LANES)), axis=-1, keepdims=True)
    rest = jnp.where(lane_f == i1, NEG, logits)
    m2 = jnp.max(rest, axis=-1, keepdims=True)
    i2 = jnp.min(jnp.where(rest == m2, lane_f, float(LANES)), axis=-1, keepdims=True)
    e = jnp.exp(m2 - m1)
    g1 = 1.0 / (1.0 + e)
    g2 = e / (1.0 + e)
    hot1 = lane_f == i1
    hot2 = lane_f == i2
    onehot = jnp.where(hot1 | hot2, 1.0, 0.0)
    before = _dot(tri_ref[...], onehot.astype(BF16)) + run_ref[0:1, :]
    r1 = jnp.sum(jnp.where(hot1, before, 0.0), axis=-1, keepdims=True)
    r2 = jnp.sum(jnp.where(hot2, before, 0.0), axis=-1, keepdims=True)
    run_ref[0:1, :] = run_ref[0:1, :] + jnp.sum(onehot, axis=0, keepdims=True)
    info = jnp.zeros((TR, LANES), F32)
    for k, val in enumerate((i1, i2, r1, r2, g1, g2)):
        info = jnp.where(lane == k, val, info)
    info_ref[...] = info
    cnt_ref[...] = jnp.broadcast_to(run_ref[0:1, :], cnt_ref.shape)


def _moe_route(x1, w_router_pad, tri):
    n = x1.shape[0]
    return pl.pallas_call(
        _moe_route_kernel,
        grid=(n // TR,),
        in_specs=[pl.BlockSpec((TR, D_MODEL), lambda i: (i, 0)), _const_spec(w_router_pad.shape),
                  _const_spec(tri.shape)],
        out_specs=[pl.BlockSpec((TR, LANES), lambda i: (i, 0)), pl.BlockSpec((8, LANES), lambda i: (0, 0))],
        out_shape=[jax.ShapeDtypeStruct((n, LANES), F32), jax.ShapeDtypeStruct((8, LANES), F32)],
        scratch_shapes=[pltpu.VMEM((8, LANES), F32)],
        compiler_params=_cparams(("arbitrary",)),
        name="moe_router_rank",
    )(x1, w_router_pad, tri)


def _moe_rowmap_kernel(d1_ref, d2_ref, rt_ref):
    def clear(r, _):
        rt_ref[r] = 0
        return 0

    lax.fori_loop(0, rt_ref.shape[0], clear, 0)

    def place(t, _):
        rt_ref[d1_ref[t]] = t
        rt_ref[d2_ref[t]] = t
        return 0

    lax.fori_loop(0, d1_ref.shape[0], place, 0)


def _moe_rowmap(dest1, dest2, n_rows):
    smem = pl.BlockSpec(memory_space=pltpu.SMEM)
    return pl.pallas_call(
        _moe_rowmap_kernel,
        in_specs=[smem, smem],
        out_specs=smem,
        out_shape=jax.ShapeDtypeStruct((n_rows,), jnp.int32),
        name="moe_row_map",
    )(dest1, dest2)


EF_CHUNK = 512


def _moe_ffn_kernel(be_ref, na_ref, rt_ref, x_hbm, wa_ref, wu_ref, wo_ref, ys_ref, xs_ref, acc_ref, sem):
    del be_ref
    blk, c = pl.program_id(0), pl.program_id(1)
    n_live = na_ref[0]
    slot = blk % 2

    def row_copy(block, r, s):
        return pltpu.make_async_copy(x_hbm.at[pl.ds(rt_ref[block * MOE_BLOCK + r], 1), :],
                                     xs_ref.at[s, pl.ds(r, 1), :], sem.at[s])

    def start_rows(block, s):
        def body(r, _):
            row_copy(block, r, s).start()
            return 0
        lax.fori_loop(0, MOE_BLOCK, body, 0)

    def wait_rows(block, s):
        def body(r, _):
            row_copy(block, r, s).wait()
            return 0
        lax.fori_loop(0, MOE_BLOCK, body, 0)

    @pl.when(blk < n_live)
    def _():
        @pl.when((blk == 0) & (c == 0))
        def _():
            start_rows(0, 0)

        @pl.when(c == 0)
        def _():
            wait_rows(blk, slot)

        @pl.when((c == 1) & (blk + 1 < n_live))
        def _():
            start_rows(blk + 1, 1 - slot)

        xb = xs_ref[slot].astype(BF16)
        h = (_silu(_dot(xb, wa_ref[...])) * _dot(xb, wu_ref[...])).astype(BF16)
        part = _dot(h, wo_ref[...])

        @pl.when(c == 0)
        def _():
            acc_ref[...] = part

        @pl.when(c > 0)
        def _():
            acc_ref[...] += part

        @pl.when(c == pl.num_programs(1) - 1)
        def _():
            ys_ref[...] = acc_ref[...]

    @pl.when((blk >= n_live) & (c == pl.num_programs(1) - 1))
    def _():
        ys_ref[...] = jnp.zeros_like(ys_ref)


def _moe_ffn(x1, row_tok, blk_expert, n_active, w_in, w_out):
    n_rows = row_tok.shape[0]
    n_blk = n_rows // MOE_BLOCK
    n_ch = D_FF_EXPERT // EF_CHUNK
    live = lambda b, na: jnp.minimum(b, na[0] - 1)
    chunk = lambda b, c, na: jnp.where(b < na[0], c, n_ch - 1)
    return pl.pallas_call(
        _moe_ffn_kernel,
        grid_spec=pltpu.PrefetchScalarGridSpec(
            num_scalar_prefetch=3,
            grid=(n_blk, n_ch),
            in_specs=[
                pl.BlockSpec(memory_space=pl.ANY),
                pl.BlockSpec((None, D_MODEL, EF_CHUNK),
                             lambda b, c, be, na, rt: (be[live(b, na)], 0, chunk(b, c, na))),
                pl.BlockSpec((None, D_MODEL, EF_CHUNK),
                             lambda b, c, be, na, rt: (be[live(b, na)], 0, n_ch + chunk(b, c, na))),
                pl.BlockSpec((None, EF_CHUNK, D_MODEL),
                             lambda b, c, be, na, rt: (be[live(b, na)], chunk(b, c, na), 0)),
            ],
            out_specs=pl.BlockSpec((MOE_BLOCK, D_MODEL), lambda b, c, be, na, rt: (b, 0)),
            scratch_shapes=[pltpu.VMEM((2, MOE_BLOCK, D_MODEL), F32), pltpu.VMEM((MOE_BLOCK, D_MODEL), F32),
                            pltpu.SemaphoreType.DMA((2,))],
        ),
        out_shape=jax.ShapeDtypeStruct((n_rows, D_MODEL), F32),
        compiler_params=_cparams(("arbitrary", "arbitrary")),
        name="moe_expert_swiglu",
    )(blk_expert, n_active, row_tok, x1, w_in, w_in, w_out)


TC = 256


def _moe_combine_kernel(d1_ref, d2_ref, ys_hbm, x_ref, xb_ref, p_ref, info_ref, wpg_ref, wpp_ref, g_ref,
                        b_ref, o_ref, ya_ref, yb_ref, sem):
    t0 = pl.program_id(0) * TC

    def copies(r):
        return (pltpu.make_async_copy(ys_hbm.at[pl.ds(d1_ref[t0 + r], 1), :], ya_ref.at[pl.ds(r, 1), :], sem),
                pltpu.make_async_copy(ys_hbm.at[pl.ds(d2_ref[t0 + r], 1), :], yb_ref.at[pl.ds(r, 1), :], sem))

    def start(r, _):
        for c in copies(r):
            c.start()
        return 0

    def wait(r, _):
        for c in copies(r):
            c.wait()
        return 0

    lax.fori_loop(0, TC, start, 0)
    lax.fori_loop(0, TC, wait, 0)
    info = info_ref[...]
    f = info[:, 4:5] * ya_ref[...] + info[:, 5:6] * yb_ref[...]
    o_ref[...] = _ple_ln2(x_ref[...], xb_ref[...], f, p_ref, wpg_ref, wpp_ref, g_ref, b_ref)


def _moe_combine(ys, dest1, dest2, x1, x1b, p2d, info, wts):
    n = x1.shape[0]
    row = lambda w: pl.BlockSpec((TC, w), lambda i, d1, d2: (i, 0))
    consts = [wts['ple_w_gate'], wts['ple_w_proj'], wts['ln2_g'], wts['ln2_b']]
    cspec = lambda c: pl.BlockSpec(c.shape, lambda i, d1, d2: (0,) * c.ndim, pipeline_mode=pl.Buffered(1))
    return pl.pallas_call(
        _moe_combine_kernel,
        grid_spec=pltpu.PrefetchScalarGridSpec(
            num_scalar_prefetch=2,
            grid=(n // TC,),
            in_specs=[pl.BlockSpec(memory_space=pl.ANY), row(D_MODEL), row(D_MODEL), row(P_DIM), row(LANES)]
            + [cspec(c) for c in consts],
            out_specs=row(D_MODEL),
            scratch_shapes=[pltpu.VMEM((TC, D_MODEL), F32), pltpu.VMEM((TC, D_MODEL), F32),
                            pltpu.SemaphoreType.DMA(())],
        ),
        out_shape=jax.ShapeDtypeStruct((n, D_MODEL), F32),
        compiler_params=_cparams(("arbitrary",)),
        name="moe_combine_ple_ln2",
    )(dest1, dest2, ys, x1, x1b, p2d, info, *consts)


def _moe(x1, x1b, p2d, wts):
    n = x1.shape[0]
    n_rows = ((n * 2 + MOE_BLOCK - 1) // MOE_BLOCK) * MOE_BLOCK + N_EXPERTS * MOE_BLOCK
    info, cnt = _moe_route(x1, wts['w_router'], wts['tri_tokens'])
    counts = cnt[0, :N_EXPERTS].astype(jnp.int32)
    padded = ((counts + MOE_BLOCK - 1) // MOE_BLOCK) * MOE_BLOCK
    ends = jnp.cumsum(padded)
    start_pad = ends - padded
    e1, e2 = info[:, 0].astype(jnp.int32), info[:, 1].astype(jnp.int32)
    dest1 = start_pad[e1] + info[:, 2].astype(jnp.int32)
    dest2 = start_pad[e2] + info[:, 3].astype(jnp.int32)
    n_blk = n_rows // MOE_BLOCK
    blk_expert = jnp.minimum(jnp.searchsorted(ends, jnp.arange(n_blk, dtype=jnp.int32) * MOE_BLOCK, side='right'),
                             N_EXPERTS - 1).astype(jnp.int32)
    n_active = (ends[-1:] // MOE_BLOCK).astype(jnp.int32)
    row_tok = _moe_rowmap(dest1, dest2, n_rows)
    ys = _moe_ffn(x1, row_tok, blk_expert, n_active, wts['moe_w_in'], wts['moe_w_out'])
    return _moe_combine(ys, dest1, dest2, x1, x1b, p2d, info, wts)


def _rot_half_cols(w, heads, dim):
    w3 = w.reshape(w.shape[0], heads, dim)
    half = dim // 2
    return jnp.concatenate([-w3[..., half:], w3[..., :half]], axis=-1).reshape(w.shape[0], heads * dim)


def _prep_layer(i, w_in, conv_w, conv_b, conv_ln_g, conv_ln_b, nsa_cmp_pe, nsa_cmp_w1, nsa_cmp_w2,
                mla_q_norm, mla_kv_norm, mla_w_uq, mla_w_ukv, w_branch, w_out, ln1_g, ln1_b,
                ple_w_gate, ple_w_proj, ln2_g, ln2_b):
    w = w_in[i]
    d = w.shape[0]
    z = lambda n: jnp.zeros((d, n), F32)
    dup = lambda a: jnp.concatenate([a, a], axis=1)
    c_glu, nq = w[:, 0:512], w[:, 512:768]
    nkv = w[:, 768:1152]
    k_cmp, v_cmp, k_slc, v_slc, k_win, v_win = [nkv[:, j * 64:(j + 1) * 64] for j in range(6)]
    ng = w[:, 1152:1164]
    mq, mkv, mkr = w[:, 1164:1420], w[:, 1420:1548], w[:, 1548:1580]
    sb = w[:, 1580:2348]
    bg = w[:, 2348:6444]
    cols = [c_glu, nq, _rot_half_cols(nq, NSA_HEADS, HEAD_DIM),
            dup(k_slc), dup(_rot_half_cols(k_slc, 1, HEAD_DIM)),
            dup(k_win), dup(_rot_half_cols(k_win, 1, HEAD_DIM)),
            dup(v_slc), dup(v_win), k_cmp, v_cmp, ng, z(LANES - 12), mq, mkv,
            z(64), mkr, z(32), z(64), _rot_half_cols(mkr, 1, MLA_ROPE), z(32), sb]
    w1 = jnp.concatenate(cols, axis=1).astype(BF16)
    assert w1.shape[1] == C_TOT

    inv32 = ROPE_THETA ** (-jnp.arange(HEAD_DIM // 2, dtype=F32) / (HEAD_DIM // 2))
    inv16 = ROPE_THETA ** (-jnp.arange(MLA_ROPE // 2, dtype=F32) / (MLA_ROPE // 2))
    inv_nsa = jnp.tile(inv32, 4)[None, :]
    inv_mla = jnp.concatenate([jnp.zeros((64,), F32), inv16, inv16, jnp.zeros((32,), F32)])[None, :]

    wuq = mla_w_uq[i].reshape(MLA_Q_RANK, MLA_HEADS, MLA_NOPE + MLA_ROPE)
    zq = jnp.zeros((MLA_Q_RANK, MLA_HEADS, 32), F32)
    wq = jnp.concatenate([wuq, zq], axis=-1).reshape(MLA_Q_RANK, MLA_HEADS * LANES)
    rope_rot = jnp.concatenate([-wuq[..., MLA_NOPE + 16:], wuq[..., MLA_NOPE:MLA_NOPE + 16]], axis=-1)
    wqr = jnp.concatenate([jnp.zeros((MLA_Q_RANK, MLA_HEADS, MLA_NOPE), F32), rope_rot, zq],
                          axis=-1).reshape(MLA_Q_RANK, MLA_HEADS * LANES)
    wukv = mla_w_ukv[i].reshape(MLA_KV_RANK, MLA_HEADS, MLA_NOPE + MLA_V)
    wk = jnp.concatenate([wukv[..., :MLA_NOPE], jnp.zeros((MLA_KV_RANK, MLA_HEADS, 64), F32)],
                         axis=-1).reshape(MLA_KV_RANK, MLA_HEADS * LANES)
    wv = wukv[..., MLA_NOPE:].reshape(MLA_KV_RANK, MLA_HEADS * MLA_V)

    pe = nsa_cmp_pe[i]
    pe_rows = pe.reshape(CMP_BLOCK, 2 * HEAD_DIM)
    pe_a = pe_rows[:CMP_STRIDE].reshape(1, CMP_STRIDE * LANES)
    pe_b = pe_rows[CMP_STRIDE:].reshape(1, CMP_STRIDE * LANES)
    w1c = nsa_cmp_w1[i].reshape(2, CMP_BLOCK, HEAD_DIM, HEAD_DIM)
    zblk = jnp.zeros((CMP_BLOCK, HEAD_DIM, HEAD_DIM), F32)
    w1full = jnp.concatenate([jnp.concatenate([w1c[0], zblk], axis=2),
                              jnp.concatenate([zblk, w1c[1]], axis=2)], axis=1)
    w1a = w1full[:CMP_STRIDE].reshape(CMP_STRIDE * LANES, LANES).astype(BF16)
    w1b = w1full[CMP_STRIDE:].reshape(CMP_STRIDE * LANES, LANES).astype(BF16)
    w2 = nsa_cmp_w2[i]
    z64 = jnp.zeros((HEAD_DIM, LANES), F32)
    w2k = jnp.concatenate([dup(w2[0]), z64], axis=0).astype(BF16)
    w2v = jnp.concatenate([z64, dup(w2[1])], axis=0).astype(BF16)

    return dict(
        w1=w1, inv_nsa=inv_nsa, inv_mla=inv_mla,
        mla_qn=mla_q_norm[i][None, :], mla_kvn=mla_kv_norm[i][None, :],
        wq=wq.astype(BF16), wqr=wqr.astype(BF16), wk=wk.astype(BF16), wv=wv.astype(BF16),
        conv_w=conv_w[i], conv_b=conv_b[i][None, :], conv_g=conv_ln_g[i][None, :], conv_beta=conv_ln_b[i][None, :],
        pe_a=pe_a, pe_b=pe_b, w1a=w1a, w1b=w1b, w2k=w2k, w2v=w2v,
        w_gate=bg.astype(BF16), w_branch=w_branch[i].astype(BF16), w_out=w_out[i].astype(BF16),
        ln1_g=ln1_g[i][None, :], ln1_b=ln1_b[i][None, :],
        ple_w_gate=ple_w_gate[i].astype(BF16), ple_w_proj=ple_w_proj[i].astype(BF16),
        ln2_g=ln2_g[i][None, :], ln2_b=ln2_b[i][None, :],
    )


def _tables(seq):
    n_cmp_rows = seq // CMP_STRIDE
    n_sel = seq // SEL_BLOCK
    cmp_start = jnp.arange(n_cmp_rows) * CMP_STRIDE
    sel_start = jnp.arange(LANES) * SEL_BLOCK
    n_cmp = (seq - CMP_BLOCK) // CMP_STRIDE + 1
    overlap = ((cmp_start[:, None] < sel_start[None, :] + SEL_BLOCK)
               & (cmp_start[:, None] + CMP_BLOCK > sel_start[None, :])
               & (jnp.arange(n_cmp_rows)[:, None] < n_cmp) & (jnp.arange(LANES)[None, :] < n_sel))
    kb = jnp.arange(seq // TK)[:, None, None]
    nn = jnp.arange(LANES)[None, :, None]
    ll = jnp.arange(TK)[None, None, :]
    expand = (kb * TK + ll) // SEL_BLOCK == nn
    jj = jnp.arange(TK)
    tri_keys = jj[:, None] >= jj[None, :]
    tt = jnp.arange(TR)
    tri_tokens = tt[None, :] < tt[:, None]
    return dict(overlap=overlap.astype(BF16), expand=expand.astype(BF16), tri_keys=tri_keys.astype(BF16),
                tri_tokens=tri_tokens.astype(BF16))


def kernel(x, p, positions, w_in, conv_w, conv_b, conv_ln_g, conv_ln_b, nsa_cmp_pe, nsa_cmp_w1, nsa_cmp_w2,
           mla_q_norm, mla_kv_norm, mla_w_uq, mla_w_ukv, w_branch, w_out, ln1_g, ln1_b, ffn_w_in, ffn_w_out,
           moe_router, moe_w_in, moe_w_out, ple_w_gate, ple_w_proj, ln2_g, ln2_b):
    batch, seq, _ = x.shape
    n = batch * seq
    tabs = _tables(seq)
    x2d = x.reshape(n, D_MODEL)
    pos2d = positions.reshape(n, 1)
    for i in range(DEPTH):
        wts = _prep_layer(i, w_in, conv_w, conv_b, conv_ln_g, conv_ln_b, nsa_cmp_pe, nsa_cmp_w1, nsa_cmp_w2,
                          mla_q_norm, mla_kv_norm, mla_w_uq, mla_w_ukv, w_branch, w_out, ln1_g, ln1_b,
                          ple_w_gate, ple_w_proj, ln2_g, ln2_b)
        wts['overlap'] = tabs['overlap']
        (conv_in, nq, nqr, ks, kw, vs, vw, kvc, ng, mq, mk, mv, sq, sk, sv) = _mixer_in(x2d, pos2d, wts)
        y_a = _conv(conv_in, wts['conv_w'], wts['conv_b'], wts['conv_g'], wts['conv_beta'], batch, seq)
        ocmp, sel = _nsa_cmp(kvc, nq, wts, batch, seq)
        y_b = _nsa_attn(nqr, ks, vs, kw, vw, sel, tabs['expand'], ocmp, ng, batch, seq)
        y_c = _mla_attn(mq, mk, mv, batch, seq)
        y_d = _sb_attn(sq, sk, sv, tabs['tri_keys'], batch, seq)
        ys = [y.reshape(n, BRANCH_W) for y in (y_a, y_b, y_c, y_d)]
        x1, x1b = _merge(x2d, ys, wts)
        p2d = p[i].reshape(n, P_DIM)
        if i % 2 == 0:
            wts['ffn_w_in'] = ffn_w_in[i // 2].astype(BF16)
            wts['ffn_w_out'] = ffn_w_out[i // 2].astype(BF16)
            x2d = _ffn_dense(x1, x1b, p2d, wts)
        else:
            wts['w_router'] = jnp.concatenate(
                [moe_router[i // 2], jnp.zeros((D_MODEL, LANES - N_EXPERTS), F32)], axis=1)
            wts['tri_tokens'] = tabs['tri_tokens']
            wts['moe_w_in'] = moe_w_in[i // 2].astype(BF16)
            wts['moe_w_out'] = moe_w_out[i // 2].astype(BF16)
            x2d = _moe(x1, x1b, p2d, wts)
    return x2d.reshape(batch, seq, D_MODEL)
```

```python
import functools

import jax
import jax.numpy as jnp
from jax import lax
from jax.experimental import pallas as pl
from jax.experimental.pallas import tpu as pltpu

F32 = jnp.float32
BF16 = jnp.bfloat16

D_MODEL = 1024
DEPTH = 2
CONV_CH = 256
CONV_WIDTH = 31
NSA_HEADS = 4
HEAD_DIM = 64
CMP_BLOCK = 32
CMP_STRIDE = 16
SEL_BLOCK = 64
SEL_TOPN = 16
WINDOW = 512
MLA_HEADS = 4
MLA_Q_RANK = 256
MLA_KV_RANK = 128
MLA_NOPE = 64
MLA_ROPE = 32
MLA_V = 64
N_BRANCH = 4
BRANCH_W = 256
ROPE_THETA = 10000.0
LN_EPS = 1e-5
RMS_EPS = 1e-6
D_FF = 2816
N_EXPERTS = 8
D_FF_EXPERT = 3584
MOE_BLOCK = 512
P_DIM = 256
DEEPNORM_ALPHA = (2 * DEPTH) ** 0.25

LANES = 128
VMEM_LIMIT = 56 * 1024 * 1024

NEG = -1e30

C_CONV = 0
C_NQ = 512
C_NQR = 768
C_KS = 1024
C_KSR = 1152
C_KW = 1280
C_KWR = 1408
C_VS = 1536
C_VW = 1664
C_KVC = 1792
C_NG = 1920
C_MQ = 2048
C_MKV = 2304
C_MKR = 2432
C_MKRR = 2560
C_SB = 2688
C_TOT = 3456

TM = 256
TQ = 256
TK = 256


def _cparams(sem, vmem=VMEM_LIMIT):
    return pltpu.CompilerParams(dimension_semantics=sem, vmem_limit_bytes=vmem)


def _const_spec(shape):
    nd = len(shape)
    return pl.BlockSpec(shape, lambda *_: (0,) * nd, pipeline_mode=pl.Buffered(1))


def _dot(a, b):
    return jnp.dot(a, b, preferred_element_type=F32)


def _dot_nt(a, b):
    return lax.dot_general(a, b, (((1,), (1,)), ((), ())), preferred_element_type=F32)


def _layer_norm(h, g, b):
    mu = jnp.mean(h, axis=-1, keepdims=True)
    d = h - mu
    var = jnp.mean(d * d, axis=-1, keepdims=True)
    return d * lax.rsqrt(var + LN_EPS) * g + b


def _rms_norm(h, g):
    return h * lax.rsqrt(jnp.mean(h * h, axis=-1, keepdims=True) + RMS_EPS) * g


def _sigmoid(x):
    return 1.0 / (1.0 + jnp.exp(-x))


def _silu(x):
    return x * _sigmoid(x)


def _split_bf16(x):
    hi = x.astype(BF16)
    lo = (x - hi.astype(F32)).astype(BF16)
    return hi, lo


def _half_select(sub, x):
    lane = lax.broadcasted_iota(jnp.int32, x.shape, 1)
    keep = (lane < HEAD_DIM) if sub == 0 else (lane >= HEAD_DIM)
    return jnp.where(keep, x, 0.0)


def _mixer_in_kernel(x_ref, pos_ref, w_ref, invn_ref, invm_ref, qn_ref, kvn_ref, wq_ref, wqr_ref,
                     wk_ref, wv_ref,
                     conv_ref, nq_ref, nqr_ref, ks_ref, kw_ref, vs_ref, vw_ref, kvc_ref, ng_ref,
                     mq_ref, mk_ref, mv_ref, sq_ref, sk_ref, sv_ref):
    xb = x_ref[...].astype(BF16)

    def proj(c0, width):
        return _dot(xb, w_ref[:, c0:c0 + width])

    posf = pos_ref[...].astype(F32)
    ang_n = posf * invn_ref[...]
    cos_n, sin_n = jnp.cos(ang_n), jnp.sin(ang_n)
    ang_m = posf * invm_ref[...]
    cos_m, sin_m = jnp.cos(ang_m), jnp.sin(ang_m)

    conv_ref[...] = proj(C_CONV, 512)

    scale = HEAD_DIM ** -0.5
    for half in range(2):
        q = proj(C_NQ + half * LANES, LANES)
        qrot = proj(C_NQR + half * LANES, LANES)
        nq_ref[:, half * LANES:(half + 1) * LANES] = (q * scale).astype(BF16)
        nqr_ref[:, half * LANES:(half + 1) * LANES] = ((q * cos_n + qrot * sin_n) * scale).astype(BF16)
    ks_ref[...] = (proj(C_KS, LANES) * cos_n + proj(C_KSR, LANES) * sin_n).astype(BF16)
    kw_ref[...] = (proj(C_KW, LANES) * cos_n + proj(C_KWR, LANES) * sin_n).astype(BF16)
    vs_ref[...] = proj(C_VS, LANES).astype(BF16)
    vw_ref[...] = proj(C_VW, LANES).astype(BF16)
    kvc_ref[...] = proj(C_KVC, LANES)
    ng_ref[...] = proj(C_NG, LANES)

    qn = _rms_norm(proj(C_MQ, MLA_Q_RANK), qn_ref[...]).astype(BF16)
    kvn = _rms_norm(proj(C_MKV, MLA_KV_RANK), kvn_ref[...]).astype(BF16)
    kr = proj(C_MKR, LANES) * cos_m + proj(C_MKRR, LANES) * sin_m
    for h in range(MLA_HEADS):
        sl = slice(h * LANES, (h + 1) * LANES)
        qa = _dot(qn, wq_ref[:, sl])
        qr = _dot(qn, wqr_ref[:, sl])
        mq_ref[:, sl] = (qa * cos_m + qr * sin_m).astype(BF16)
        mk_ref[:, sl] = (_dot(kvn, wk_ref[:, sl]) + kr).astype(BF16)
    mv_ref[...] = _dot(kvn, wv_ref[...]).astype(BF16)

    sq_ref[...] = (proj(C_SB, 256) * scale).astype(BF16)
    sk_ref[...] = proj(C_SB + 256, 256).astype(BF16)
    sv_ref[...] = proj(C_SB + 512, 256).astype(BF16)


def _mixer_in(x2d, pos2d, wts):
    n = x2d.shape[0]
    row = lambda w: pl.BlockSpec((TM, w), lambda i: (i, 0))
    out_widths = [512, 256, 256, 128, 128, 128, 128, 128, 128, 512, 512, 256, 256, 256, 256]
    out_dtypes = [F32, BF16, BF16, BF16, BF16, BF16, BF16, F32, F32, BF16, BF16, BF16, BF16, BF16, BF16]
    consts = [wts['w1'], wts['inv_nsa'], wts['inv_mla'], wts['mla_qn'], wts['mla_kvn'], wts['wq'],
              wts['wqr'], wts['wk'], wts['wv']]
    return pl.pallas_call(
        _mixer_in_kernel,
        grid=(n // TM,),
        in_specs=[row(D_MODEL), row(1)] + [_const_spec(c.shape) for c in consts],
        out_specs=[row(w) for w in out_widths],
        out_shape=[jax.ShapeDtypeStruct((n, w), d) for w, d in zip(out_widths, out_dtypes)],
        compiler_params=_cparams(("parallel",)),
        name="mixer_in",
    )(x2d, pos2d, *consts)


CONV_PAD = 32
CONV_CHUNK = 128


def _conv_kernel(u_ref, w_ref, b_ref, g_ref, beta_ref, o_ref, hp_ref):
    seq = u_ref.shape[0]
    hp_ref[0:CONV_PAD, :] = jnp.zeros((CONV_PAD, CONV_CH), F32)
    hp_ref[CONV_PAD:CONV_PAD + seq, :] = u_ref[:, 0:CONV_CH] * _sigmoid(u_ref[:, CONV_CH:2 * CONV_CH])
    shift = CONV_PAD - (CONV_WIDTH - 1)
    for c in range(seq // CONV_CHUNK):
        base = c * CONV_CHUNK
        acc = jnp.broadcast_to(b_ref[...], (CONV_CHUNK, CONV_CH))
        for j in range(CONV_WIDTH):
            acc = acc + hp_ref[base + shift + j:base + shift + j + CONV_CHUNK, :] * w_ref[j:j + 1, :]
        y = _layer_norm(acc, g_ref[...], beta_ref[...])
        o_ref[base:base + CONV_CHUNK, :] = _silu(y).astype(BF16)


def _conv(conv_in, w, b, g, beta, batch, seq):
    return pl.pallas_call(
        _conv_kernel,
        grid=(batch,),
        in_specs=[pl.BlockSpec((None, seq, 2 * CONV_CH), lambda i: (i, 0, 0)),
                  _const_spec(w.shape), _const_spec(b.shape), _const_spec(g.shape), _const_spec(beta.shape)],
        out_specs=pl.BlockSpec((None, seq, CONV_CH), lambda i: (i, 0, 0)),
        out_shape=jax.ShapeDtypeStruct((batch, seq, CONV_CH), BF16),
        scratch_shapes=[pltpu.VMEM((CONV_PAD + seq, CONV_CH), F32)],
        compiler_params=_cparams(("parallel",)),
        name="conformer_conv",
    )(conv_in.reshape(batch, seq, 2 * CONV_CH), w, b, g, beta)


def _gelu_tanh(x):
    return 0.5 * x * (1.0 + jnp.tanh(0.7978845608028654 * (x + 0.044715 * x * x * x)))


def _nsa_cmp_kernel(kvc_ref, q_ref, pea_ref, peb_ref, w1a_ref, w1b_ref, w2k_ref, w2v_ref, ov_ref,
                    ocmp_ref, sel_ref, *, seq):
    n_cmp = (seq - CMP_BLOCK) // CMP_STRIDE + 1
    nb = seq // CMP_STRIDE
    x2 = kvc_ref[...]
    xa = (x2 + pea_ref[...]).astype(BF16)
    xb = (x2 + peb_ref[...]).astype(BF16)
    ha = _dot(xa, w1a_ref[...])
    hb = _dot(xb, w1b_ref[...])
    hid = ha + pltpu.roll(hb, nb - 1, 0)
    hid = _gelu_tanh(hid).astype(BF16)
    kk = _dot(hid, w2k_ref[...]).astype(BF16)
    vv = _dot(hid, w2v_ref[...]).astype(BF16)
    ov = ov_ref[...]

    for c in range(seq // TQ):
        r0 = c * TQ
        t = r0 + lax.broadcasted_iota(jnp.int32, (TQ, LANES), 0)
        j = lax.broadcasted_iota(jnp.int32, (TQ, LANES), 1)
        valid = (j * CMP_STRIDE + CMP_BLOCK - 1 <= t) & (j < n_cmp)
        psum = jnp.zeros((TQ, LANES), F32)
        for pair in range(2):
            qp = q_ref[r0:r0 + TQ, pair * LANES:(pair + 1) * LANES].astype(F32)
            outs = []
            for sub in range(2):
                qm = _half_select(sub, qp).astype(BF16)
                s = jnp.where(valid, _dot_nt(qm, kk), NEG)
                m = jnp.max(s, axis=-1, keepdims=True)
                e = jnp.where(valid, jnp.exp(s - m), 0.0)
                den = jnp.sum(e, axis=-1, keepdims=True)
                p = e / jnp.where(den > 0, den, 1.0)
                psum = psum + p
                outs.append(_dot(p.astype(BF16), vv))
            ocmp_ref[r0:r0 + TQ, pair * LANES:(pair + 1) * LANES] = jnp.where(
                lax.broadcasted_iota(jnp.int32, (TQ, LANES), 1) < HEAD_DIM, outs[0], outs[1])
        p_hi, p_lo = _split_bf16(psum)
        imp = _dot(p_hi, ov) + _dot(p_lo, ov)
        cur = t // SEL_BLOCK
        forced = (j == 0) | (j == cur) | (j == cur - 1)
        imp = jnp.where(forced, jnp.inf, imp)
        imp = jnp.where(j > cur, -jnp.inf, imp)
        rank = jnp.zeros((TQ, LANES), F32)
        for n2 in range(seq // SEL_BLOCK):
            col = imp[:, n2:n2 + 1]
            ahead = (col > imp) | ((col == imp) & (n2 < j))
            rank = rank + jnp.where(ahead, 1.0, 0.0)
        sel = (rank < SEL_TOPN) & (imp > -jnp.inf)
        sel_ref[r0:r0 + TQ, :] = jnp.where(sel, 1.0, 0.0).astype(BF16)


def _nsa_cmp(kvc, nq, wts, batch, seq):
    nb = seq // CMP_STRIDE
    consts = [wts['pe_a'], wts['pe_b'], wts['w1a'], wts['w1b'], wts['w2k'], wts['w2v'], wts['overlap']]
    return pl.pallas_call(
        functools.partial(_nsa_cmp_kernel, seq=seq),
        grid=(batch,),
        in_specs=[pl.BlockSpec((None, nb, CMP_STRIDE * LANES), lambda i: (i, 0, 0)),
                  pl.BlockSpec((None, seq, 256), lambda i: (i, 0, 0))] + [_const_spec(c.shape) for c in consts],
        out_specs=[pl.BlockSpec((None, seq, 256), lambda i: (i, 0, 0)),
                   pl.BlockSpec((None, seq, LANES), lambda i: (i, 0, 0))],
        out_shape=[jax.ShapeDtypeStruct((batch, seq, 256), F32),
                   jax.ShapeDtypeStruct((batch, seq, LANES), BF16)],
        compiler_params=_cparams(("parallel",)),
        name="nsa_compress_select",
    )(kvc.reshape(batch, nb, CMP_STRIDE * LANES), nq.reshape(batch, seq, 256), *consts)


LOG2E = 1.4426950408889634
N_HEADS = 4


def _softmax_scratch(n_tiles):
    slab = pltpu.VMEM((N_HEADS, TQ, LANES), F32)
    return [pltpu.VMEM((N_HEADS, n_tiles, TQ, TK), F32), slab, slab, slab, slab]


def _scores_put(h, t, s, s_ref, mx_ref):
    s_ref[h, t] = s
    mx_ref[h] = jnp.maximum(mx_ref[h], jnp.maximum(s[:, :LANES], s[:, LANES:]))


def _row_max(mx_ref, mb_ref):
    for h in range(N_HEADS):
        mb_ref[h] = jnp.broadcast_to(jnp.max(mx_ref[h], axis=-1, keepdims=True), (TQ, LANES))


def _probs_accumulate(h, t, c, v_blk, s_ref, mb_ref, ls_ref, acc_ref):
    s, mb = s_ref[h, t], mb_ref[h]
    pa = jnp.exp2((s[:, :LANES] - mb) * c)
    pb = jnp.exp2((s[:, LANES:] - mb) * c)
    ls_ref[h] += pa + pb
    acc_ref[h] += _dot(jnp.concatenate([pa, pb], axis=1).astype(BF16), v_blk)


def _softmax_out(h, ls_ref, acc_ref):
    return acc_ref[h] / jnp.sum(ls_ref[h], axis=-1, keepdims=True)


def _tile_iotas():
    return (lax.broadcasted_iota(jnp.int32, (TQ, TK), 0), lax.broadcasted_iota(jnp.int32, (TQ, TK), 1))


def _nsa_attn_kernel(q_ref, ks_ref, vs_ref, kw_ref, vw_ref, sel_ref, exp_ref, ocmp_ref, ng_ref, o_ref,
                     qm_ref, ss_ref, mxs_ref, mbs_ref, lss_ref, accs_ref,
                     sw_ref, mxw_ref, mbw_ref, lsw_ref, accw_ref):
    i = pl.program_id(1)
    row, col = _tile_iotas()
    sel = sel_ref[...]
    for ref in (mxs_ref, mxw_ref):
        ref[...] = jnp.full(ref.shape, NEG, F32)
    for ref in (lss_ref, accs_ref, lsw_ref, accw_ref):
        ref[...] = jnp.zeros(ref.shape, F32)
    for pair in range(2):
        qp = q_ref[:, pair * LANES:(pair + 1) * LANES].astype(F32)
        for sub in range(2):
            qm_ref[2 * pair + sub] = _half_select(sub, qp).astype(BF16)

    def selected_scores(kb, causal):
        k0 = pl.multiple_of(kb * TK, TK)
        hit = _dot(sel, exp_ref[kb]) > 0.5
        if causal:
            hit = hit & (col <= row)
        bias = jnp.where(hit, 0.0, NEG)
        k_blk = ks_ref[pl.ds(k0, TK), :]
        for h in range(N_HEADS):
            _scores_put(h, kb, _dot_nt(qm_ref[h], k_blk) + bias, ss_ref, mxs_ref)

    def window_scores(slot, mask):
        k0 = pl.multiple_of((i - 2 + slot) * TK, TK)
        k_blk = kw_ref[pl.ds(k0, TK), :]
        for h in range(N_HEADS):
            s = _dot_nt(qm_ref[h], k_blk)
            if mask is not None:
                s = jnp.where(mask, s, NEG)
            _scores_put(h, slot, s, sw_ref, mxw_ref)

    def off_diagonal(kb, _):
        selected_scores(kb, False)
        return 0

    lax.fori_loop(0, i, off_diagonal, 0)
    selected_scores(i, True)
    pl.when(i >= 2)(lambda: window_scores(0, col > row))
    pl.when(i >= 1)(lambda: window_scores(1, None))
    window_scores(2, col <= row)
    _row_max(mxs_ref, mbs_ref)
    _row_max(mxw_ref, mbw_ref)

    def selected_probs(kb, _):
        v_blk = vs_ref[pl.ds(pl.multiple_of(kb * TK, TK), TK), :]
        for h in range(N_HEADS):
            _probs_accumulate(h, kb, LOG2E, v_blk, ss_ref, mbs_ref, lss_ref, accs_ref)
        return 0

    def window_probs(slot):
        v_blk = vw_ref[pl.ds(pl.multiple_of((i - 2 + slot) * TK, TK), TK), :]
        for h in range(N_HEADS):
            _probs_accumulate(h, slot, LOG2E, v_blk, sw_ref, mbw_ref, lsw_ref, accw_ref)

    lax.fori_loop(0, i + 1, selected_probs, 0)
    pl.when(i >= 2)(lambda: window_probs(0))
    pl.when(i >= 1)(lambda: window_probs(1))
    window_probs(2)

    g = _sigmoid(ng_ref[...])
    lane = lax.broadcasted_iota(jnp.int32, (TQ, LANES), 1)
    for pair in range(2):
        res = []
        for sub in range(2):
            h = 2 * pair + sub
            res.append(g[:, 3 * h + 1:3 * h + 2] * _softmax_out(h, lss_ref, accs_ref)
                       + g[:, 3 * h + 2:3 * h + 3] * _softmax_out(h, lsw_ref, accw_ref)
                       + g[:, 3 * h:3 * h + 1] * ocmp_ref[:, pair * LANES:(pair + 1) * LANES])
        o_ref[:, pair * LANES:(pair + 1) * LANES] = jnp.where(lane < HEAD_DIM, res[0], res[1]).astype(BF16)


def _nsa_attn(nqr, ks, vs, kw, vw, sel, expand, ocmp, ng, batch, seq):
    qspec = lambda w: pl.BlockSpec((None, TQ, w), lambda b, i: (b, i, 0))
    kspec = pl.BlockSpec((None, seq, LANES), lambda b, i: (b, 0, 0))
    r3 = lambda a: a.reshape(batch, seq, a.shape[-1])
    return pl.pallas_call(
        _nsa_attn_kernel,
        grid=(batch, seq // TQ),
        in_specs=[qspec(256), kspec, kspec, kspec, kspec, qspec(LANES), _const_spec(expand.shape),
                  qspec(256), qspec(LANES)],
        out_specs=qspec(256),
        out_shape=jax.ShapeDtypeStruct((batch, seq, 256), BF16),
        scratch_shapes=[pltpu.VMEM((N_HEADS, TQ, LANES), BF16)] + _softmax_scratch(seq // TK)
        + _softmax_scratch(WINDOW // TK + 1),
        compiler_params=_cparams(("parallel", "parallel")),
        name="nsa_select_window",
    )(r3(nqr), r3(ks), r3(vs), r3(kw), r3(vw), sel, expand, ocmp, r3(ng))


def _mla_attn_kernel(q_ref, k_ref, v_ref, o_ref, s_ref, mx_ref, mb_ref, ls_ref, acc_ref):
    i = pl.program_id(1)
    c = (MLA_NOPE + MLA_ROPE) ** -0.5 * LOG2E
    row, col = _tile_iotas()
    mx_ref[...] = jnp.full(mx_ref.shape, NEG, F32)
    ls_ref[...] = jnp.zeros(ls_ref.shape, F32)
    acc_ref[...] = jnp.zeros(acc_ref.shape, F32)

    def scores(kb, causal):
        k0 = pl.multiple_of(kb * TK, TK)
        for h in range(N_HEADS):
            hs = slice(h * LANES, (h + 1) * LANES)
            s = _dot_nt(q_ref[:, hs], k_ref[pl.ds(k0, TK), hs])
            if causal:
                s = jnp.where(col <= row, s, NEG)
            _scores_put(h, kb, s, s_ref, mx_ref)

    def off_diagonal(kb, _):
        scores(kb, False)
        return 0

    lax.fori_loop(0, i, off_diagonal, 0)
    scores(i, True)
    _row_max(mx_ref, mb_ref)

    def probs(kb, _):
        k0 = pl.multiple_of(kb * TK, TK)
        for h in range(N_HEADS):
            v_blk = v_ref[pl.ds(k0, TK), (h // 2) * LANES:(h // 2 + 1) * LANES]
            _probs_accumulate(h, kb, c, v_blk, s_ref, mb_ref, ls_ref, acc_ref)
        return 0

    lax.fori_loop(0, i + 1, probs, 0)
    lane = lax.broadcasted_iota(jnp.int32, (TQ, LANES), 1)
    for pair in range(2):
        o_ref[:, pair * LANES:(pair + 1) * LANES] = jnp.where(
            lane < HEAD_DIM, _softmax_out(2 * pair, ls_ref, acc_ref),
            _softmax_out(2 * pair + 1, ls_ref, acc_ref)).astype(BF16)


def _mla_attn(mq, mk, mv, batch, seq):
    r3 = lambda a: a.reshape(batch, seq, a.shape[-1])
    return pl.pallas_call(
        _mla_attn_kernel,
        grid=(batch, seq // TQ),
        in_specs=[pl.BlockSpec((None, TQ, 512), lambda b, i: (b, i, 0)),
                  pl.BlockSpec((None, seq, 512), lambda b, i: (b, 0, 0)),
                  pl.BlockSpec((None, seq, 256), lambda b, i: (b, 0, 0))],
        out_specs=pl.BlockSpec((None, TQ, 256), lambda b, i: (b, i, 0)),
        out_shape=jax.ShapeDtypeStruct((batch, seq, 256), BF16),
        scratch_shapes=_softmax_scratch(seq // TK),
        compiler_params=_cparams(("parallel", "parallel")),
        name="mla_attention",
    )(r3(mq), r3(mk), r3(mv))


def _sb_attn_kernel(q_ref, k_ref, v_ref, tri_ref, o_ref, qm_ref, e_ref, tail_ref, acc_ref):
    i = pl.program_id(1)
    row, col = _tile_iotas()
    tail_ref[...] = jnp.zeros(tail_ref.shape, F32)
    acc_ref[...] = jnp.zeros(acc_ref.shape, F32)
    for pair in range(2):
        qp = q_ref[:, pair * LANES:(pair + 1) * LANES].astype(F32)
        for sub in range(2):
            qm_ref[2 * pair + sub] = _half_select(sub, qp).astype(BF16)

    def log_weights(kb, diagonal):
        k0 = pl.multiple_of(kb * TK, TK)
        tri = tri_ref[...]
        for h in range(N_HEADS):
            ps = slice((h // 2) * LANES, (h // 2 + 1) * LANES)
            z = _dot_nt(qm_ref[h], k_ref[pl.ds(k0, TK), ps])
            log_beta = jnp.minimum(z, 0.0) - jnp.log1p(jnp.exp(-jnp.abs(z)))
            log_keep = log_beta - z
            if diagonal:
                log_keep = jnp.where(col < row, log_keep, 0.0)
            lk_hi, lk_lo = _split_bf16(log_keep)
            incl = _dot(lk_hi, tri) + _dot(lk_lo, tri)
            tail = tail_ref[h]
            e = log_beta + (incl - log_keep) + jnp.concatenate([tail, tail], axis=1)
            if diagonal:
                e = jnp.where(col < row, e, NEG)
            e_ref[h, kb] = e
            tail_ref[h] = tail + jnp.broadcast_to(incl[:, 0:1], (TQ, LANES))

    log_weights(i, True)

    def off_diagonal(step, _):
        log_weights(i - 1 - step, False)
        return 0

    lax.fori_loop(0, i, off_diagonal, 0)

    def weighted_values(kb, _):
        k0 = pl.multiple_of(kb * TK, TK)
        for h in range(N_HEADS):
            ps = slice((h // 2) * LANES, (h // 2 + 1) * LANES)
            acc_ref[h] += _dot(jnp.exp(e_ref[h, kb]).astype(BF16), v_ref[pl.ds(k0, TK), ps])
        return 0

    lax.fori_loop(0, i + 1, weighted_values, 0)
    lane = lax.broadcasted_iota(jnp.int32, (TQ, LANES), 1)
    for pair in range(2):
        o_ref[:, pair * LANES:(pair + 1) * LANES] = jnp.where(
            lane < HEAD_DIM, acc_ref[2 * pair], acc_ref[2 * pair + 1]).astype(BF16)


def _sb_attn(sq, sk, sv, tri, batch, seq):
    r3 = lambda a: a.reshape(batch, seq, a.shape[-1])
    kspec = pl.BlockSpec((None, seq, 256), lambda b, i: (b, 0, 0))
    qspec = pl.BlockSpec((None, TQ, 256), lambda b, i: (b, i, 0))
    return pl.pallas_call(
        _sb_attn_kernel,
        grid=(batch, seq // TQ),
        in_specs=[qspec, kspec, kspec, _const_spec(tri.shape)],
        out_specs=qspec,
        out_shape=jax.ShapeDtypeStruct((batch, seq, 256), BF16),
        scratch_shapes=[pltpu.VMEM((N_HEADS, TQ, LANES), BF16), pltpu.VMEM((N_HEADS, seq // TK, TQ, TK), F32),
                        pltpu.VMEM((N_HEADS, TQ, LANES), F32), pltpu.VMEM((N_HEADS, TQ, LANES), F32)],
        compiler_params=_cparams(("parallel", "parallel")),
        name="stick_breaking_attention",
    )(r3(sq), r3(sk), r3(sv), tri)


def _merge_kernel(x_ref, ya_ref, yb_ref, yc_ref, yd_ref, wg_ref, wb_ref, wo_ref, g_ref, b_ref,
                  o_ref, ob_ref):
    x = x_ref[...]
    xb = x.astype(BF16)
    mixed = jnp.zeros((TM, D_MODEL), F32)
    for n, y_ref in enumerate((ya_ref, yb_ref, yc_ref, yd_ref)):
        gate = _sigmoid(_dot(xb, wg_ref[:, n * D_MODEL:(n + 1) * D_MODEL]))
        mixed = mixed + gate * _dot(y_ref[...], wb_ref[n])
    h = DEEPNORM_ALPHA * x + _dot(mixed.astype(BF16), wo_ref[...])
    out = _layer_norm(h, g_ref[...], b_ref[...])
    o_ref[...] = out
    ob_ref[...] = out.astype(BF16)


def _merge(x2d, ys, wts):
    n = x2d.shape[0]
    row = lambda w: pl.BlockSpec((TM, w), lambda i: (i, 0))
    consts = [wts['w_gate'], wts['w_branch'], wts['w_out'], wts['ln1_g'], wts['ln1_b']]
    return pl.pallas_call(
        _merge_kernel,
        grid=(n // TM,),
        in_specs=[row(D_MODEL)] + [row(BRANCH_W)] * 4 + [_const_spec(c.shape) for c in consts],
        out_specs=[row(D_MODEL), row(D_MODEL)],
        out_shape=[jax.ShapeDtypeStruct((n, D_MODEL), F32), jax.ShapeDtypeStruct((n, D_MODEL), BF16)],
        compiler_params=_cparams(("parallel",)),
        name="branch_merge_ln1",
    )(x2d, *ys, *consts)


def _ple_ln2(x1, x1b, f, p_ref, wpg_ref, wpp_ref, g_ref, b_ref):
    ple = _sigmoid(_dot(x1b, wpg_ref[...])) * _dot(p_ref[...].astype(BF16), wpp_ref[...])
    return _layer_norm(DEEPNORM_ALPHA * x1 + f + ple, g_ref[...], b_ref[...])


FF_CHUNK = 256


def _ffn_dense_kernel(x_ref, xb_ref, p_ref, wi_ref, wo_ref, wpg_ref, wpp_ref, g_ref, b_ref, o_ref, acc_ref):
    xb = xb_ref[...]
    for c in range(D_FF // FF_CHUNK):
        a = _dot(xb, wi_ref[:, c * FF_CHUNK:(c + 1) * FF_CHUNK])
        u = _dot(xb, wi_ref[:, D_FF + c * FF_CHUNK:D_FF + (c + 1) * FF_CHUNK])
        part = _dot((_silu(a) * u).astype(BF16), wo_ref[c * FF_CHUNK:(c + 1) * FF_CHUNK, :])
        if c == 0:
            acc_ref[...] = part
        else:
            acc_ref[...] += part
    o_ref[...] = _ple_ln2(x_ref[...], xb, acc_ref[...], p_ref, wpg_ref, wpp_ref, g_ref, b_ref)


def _ffn_dense(x1, x1b, p2d, wts):
    n = x1.shape[0]
    row = lambda w: pl.BlockSpec((TM, w), lambda i: (i, 0))
    consts = [wts['ffn_w_in'], wts['ffn_w_out'], wts['ple_w_gate'], wts['ple_w_proj'], wts['ln2_g'], wts['ln2_b']]
    return pl.pallas_call(
        _ffn_dense_kernel,
        grid=(n // TM,),
        in_specs=[row(D_MODEL), row(D_MODEL), row(P_DIM)] + [_const_spec(c.shape) for c in consts],
        out_specs=row(D_MODEL),
        out_shape=jax.ShapeDtypeStruct((n, D_MODEL), F32),
        scratch_shapes=[pltpu.VMEM((TM, D_MODEL), F32)],
        compiler_params=_cparams(("parallel",)),
        name="ffn_dense_ple_ln2",
    )(x1, x1b, p2d, *consts)


TR = 512
INFO_LANES = 6


def _moe_route_kernel(x_ref, wr_ref, tri_ref, info_ref, cnt_ref, run_ref):
    @pl.when(pl.program_id(0) == 0)
    def _():
        run_ref[...] = jnp.zeros_like(run_ref)

    logits = jnp.dot(x_ref[...], wr_ref[...], precision=lax.Precision.HIGHEST, preferred_element_type=F32)
    lane = lax.broadcasted_iota(jnp.int32, (TR, LANES), 1)
    lane_f = lane.astype(F32)
    logits = jnp.where(lane < N_EXPERTS, logits, NEG)
    m1 = jnp.max(logits, axis=-1, keepdims=True)
    i1 = jnp.min(jnp.where(logits == m1, lane_f, float(LANES)), axis=-1, keepdims=True)
    rest = jnp.where(lane_f == i1, NEG, logits)
    m2 = jnp.max(rest, axis=-1, keepdims=True)
    i2 = jnp.min(jnp.where(rest == m2, lane_f, float(LANES)), axis=-1, keepdims=True)
    e = jnp.exp(m2 - m1)
    g1 = 1.0 / (1.0 + e)
    g2 = e / (1.0 + e)
    hot1 = lane_f == i1
    hot2 = lane_f == i2
    onehot = jnp.where(hot1 | hot2, 1.0, 0.0)
    before = _dot(tri_ref[...], onehot.astype(BF16)) + run_ref[0:1, :]
    r1 = jnp.sum(jnp.where(hot1, before, 0.0), axis=-1, keepdims=True)
    r2 = jnp.sum(jnp.where(hot2, before, 0.0), axis=-1, keepdims=True)
    run_ref[0:1, :] = run_ref[0:1, :] + jnp.sum(onehot, axis=0, keepdims=True)
    info = jnp.zeros((TR, LANES), F32)
    for k, val in enumerate((i1, i2, r1, r2, g1, g2)):
        info = jnp.where(lane == k, val, info)
    info_ref[...] = info
    cnt_ref[...] = jnp.broadcast_to(run_ref[0:1, :], cnt_ref.shape)


def _moe_route(x1, w_router_pad, tri):
    n = x1.shape[0]
    return pl.pallas_call(
        _moe_route_kernel,
        grid=(n // TR,),
        in_specs=[pl.BlockSpec((TR, D_MODEL), lambda i: (i, 0)), _const_spec(w_router_pad.shape),
                  _const_spec(tri.shape)],
        out_specs=[pl.BlockSpec((TR, LANES), lambda i: (i, 0)), pl.BlockSpec((8, LANES), lambda i: (0, 0))],
        out_shape=[jax.ShapeDtypeStruct((n, LANES), F32), jax.ShapeDtypeStruct((8, LANES), F32)],
        scratch_shapes=[pltpu.VMEM((8, LANES), F32)],
        compiler_params=_cparams(("arbitrary",)),
        name="moe_router_rank",
    )(x1, w_router_pad, tri)


def _moe_rowmap_kernel(d1_ref, d2_ref, rt_ref):
    def clear(r, _):
        rt_ref[r] = 0
        return 0

    lax.fori_loop(0, rt_ref.shape[0], clear, 0, unroll=16)

    def place(t, _):
        rt_ref[d1_ref[t]] = t
        rt_ref[d2_ref[t]] = t
        return 0

    lax.fori_loop(0, d1_ref.shape[0], place, 0, unroll=8)


def _moe_rowmap(dest1, dest2, n_rows):
    smem = pl.BlockSpec(memory_space=pltpu.SMEM)
    return pl.pallas_call(
        _moe_rowmap_kernel,
        in_specs=[smem, smem],
        out_specs=smem,
        out_shape=jax.ShapeDtypeStruct((n_rows,), jnp.int32),
        name="moe_row_map",
    )(dest1, dest2)


EF_CHUNK = 512


def _moe_ffn_kernel(be_ref, na_ref, rt_ref, x_hbm, wa_ref, wu_ref, wo_ref, ys_ref, xs_ref, acc_ref, sem):
    del be_ref
    blk, c = pl.program_id(0), pl.program_id(1)
    n_live = na_ref[0]
    slot = blk % 2

    def row_copy(block, r, s):
        return pltpu.make_async_copy(x_hbm.at[pl.ds(rt_ref[block * MOE_BLOCK + r], 1), :],
                                     xs_ref.at[s, pl.ds(r, 1), :], sem.at[s])

    def start_rows(block, s):
        def body(r, _):
            row_copy(block, r, s).start()
            return 0
        lax.fori_loop(0, MOE_BLOCK, body, 0)

    def wait_rows(block, s):
        def body(r, _):
            row_copy(block, r, s).wait()
            return 0
        lax.fori_loop(0, MOE_BLOCK, body, 0)

    @pl.when(blk < n_live)
    def _():
        @pl.when((blk == 0) & (c == 0))
        def _():
            start_rows(0, 0)

        @pl.when(c == 0)
        def _():
            wait_rows(blk, slot)

        @pl.when((c == 1) & (blk + 1 < n_live))
        def _():
            start_rows(blk + 1, 1 - slot)

        xb = xs_ref[slot].astype(BF16)
        h = (_silu(_dot(xb, wa_ref[...])) * _dot(xb, wu_ref[...])).astype(BF16)
        part = _dot(h, wo_ref[...])

        @pl.when(c == 0)
        def _():
            acc_ref[...] = part

        @pl.when(c > 0)
        def _():
            acc_ref[...] += part

        @pl.when(c == pl.num_programs(1) - 1)
        def _():
            ys_ref[...] = acc_ref[...]

    @pl.when((blk >= n_live) & (c == pl.num_programs(1) - 1))
    def _():
        ys_ref[...] = jnp.zeros_like(ys_ref)


def _moe_ffn(x1, row_tok, blk_expert, n_active, w_in, w_out):
    n_rows = row_tok.shape[0]
    n_blk = n_rows // MOE_BLOCK
    n_ch = D_FF_EXPERT // EF_CHUNK
    live = lambda b, na: jnp.minimum(b, na[0] - 1)
    chunk = lambda b, c, na: jnp.where(b < na[0], c, n_ch - 1)
    return pl.pallas_call(
        _moe_ffn_kernel,
        grid_spec=pltpu.PrefetchScalarGridSpec(
            num_scalar_prefetch=3,
            grid=(n_blk, n_ch),
            in_specs=[
                pl.BlockSpec(memory_space=pl.ANY),
                pl.BlockSpec((None, D_MODEL, EF_CHUNK),
                             lambda b, c, be, na, rt: (be[live(b, na)], 0, chunk(b, c, na))),
                pl.BlockSpec((None, D_MODEL, EF_CHUNK),
                             lambda b, c, be, na, rt: (be[live(b, na)], 0, n_ch + chunk(b, c, na))),
                pl.BlockSpec((None, EF_CHUNK, D_MODEL),
                             lambda b, c, be, na, rt: (be[live(b, na)], chunk(b, c, na), 0)),
            ],
            out_specs=pl.BlockSpec((MOE_BLOCK, D_MODEL), lambda b, c, be, na, rt: (b, 0)),
            scratch_shapes=[pltpu.VMEM((2, MOE_BLOCK, D_MODEL), F32), pltpu.VMEM((MOE_BLOCK, D_MODEL), F32),
                            pltpu.SemaphoreType.DMA((2,))],
        ),
        out_shape=jax.ShapeDtypeStruct((n_rows, D_MODEL), F32),
        compiler_params=_cparams(("arbitrary", "arbitrary")),
        name="moe_expert_swiglu",
    )(blk_expert, n_active, row_tok, x1, w_in, w_in, w_out)


TC = 256


def _moe_combine_kernel(d1_ref, d2_ref, ys_hbm, x_ref, xb_ref, p_ref, info_ref, wpg_ref, wpp_ref, g_ref,
                        b_ref, o_ref, ya_ref, yb_ref, sem):
    t0 = pl.program_id(0) * TC

    def copies(r):
        return (pltpu.make_async_copy(ys_hbm.at[pl.ds(d1_ref[t0 + r], 1), :], ya_ref.at[pl.ds(r, 1), :], sem),
                pltpu.make_async_copy(ys_hbm.at[pl.ds(d2_ref[t0 + r], 1), :], yb_ref.at[pl.ds(r, 1), :], sem))

    def start(r, _):
        for c in copies(r):
            c.start()
        return 0

    def wait(r, _):
        for c in copies(r):
            c.wait()
        return 0

    lax.fori_loop(0, TC, start, 0)
    lax.fori_loop(0, TC, wait, 0)
    info = info_ref[...]
    f = info[:, 4:5] * ya_ref[...] + info[:, 5:6] * yb_ref[...]
    o_ref[...] = _ple_ln2(x_ref[...], xb_ref[...], f, p_ref, wpg_ref, wpp_ref, g_ref, b_ref)


def _moe_combine(ys, dest1, dest2, x1, x1b, p2d, info, wts):
    n = x1.shape[0]
    row = lambda w: pl.BlockSpec((TC, w), lambda i, d1, d2: (i, 0))
    consts = [wts['ple_w_gate'], wts['ple_w_proj'], wts['ln2_g'], wts['ln2_b']]
    cspec = lambda c: pl.BlockSpec(c.shape, lambda i, d1, d2: (0,) * c.ndim, pipeline_mode=pl.Buffered(1))
    return pl.pallas_call(
        _moe_combine_kernel,
        grid_spec=pltpu.PrefetchScalarGridSpec(
            num_scalar_prefetch=2,
            grid=(n // TC,),
            in_specs=[pl.BlockSpec(memory_space=pl.ANY), row(D_MODEL), row(D_MODEL), row(P_DIM), row(LANES)]
            + [cspec(c) for c in consts],
            out_specs=row(D_MODEL),
            scratch_shapes=[pltpu.VMEM((TC, D_MODEL), F32), pltpu.VMEM((TC, D_MODEL), F32),
                            pltpu.SemaphoreType.DMA(())],
        ),
        out_shape=jax.ShapeDtypeStruct((n, D_MODEL), F32),
        compiler_params=_cparams(("arbitrary",)),
        name="moe_combine_ple_ln2",
    )(dest1, dest2, ys, x1, x1b, p2d, info, *consts)


def _moe(x1, x1b, p2d, wts):
    n = x1.shape[0]
    n_rows = ((n * 2 + MOE_BLOCK - 1) // MOE_BLOCK) * MOE_BLOCK + N_EXPERTS * MOE_BLOCK
    info, cnt = _moe_route(x1, wts['w_router'], wts['tri_tokens'])
    counts = cnt[0, :N_EXPERTS].astype(jnp.int32)
    padded = ((counts + MOE_BLOCK - 1) // MOE_BLOCK) * MOE_BLOCK
    ends = jnp.cumsum(padded)
    start_pad = ends - padded
    e1, e2 = info[:, 0].astype(jnp.int32), info[:, 1].astype(jnp.int32)
    dest1 = start_pad[e1] + info[:, 2].astype(jnp.int32)
    dest2 = start_pad[e2] + info[:, 3].astype(jnp.int32)
    n_blk = n_rows // MOE_BLOCK
    blk_row0 = jnp.arange(n_blk, dtype=jnp.int32) * MOE_BLOCK
    blk_expert = jnp.minimum(jnp.sum((ends[None, :] <= blk_row0[:, None]).astype(jnp.int32), axis=1),
                             N_EXPERTS - 1)
    n_active = (ends[-1:] // MOE_BLOCK).astype(jnp.int32)
    row_tok = _moe_rowmap(dest1, dest2, n_rows)
    ys = _moe_ffn(x1, row_tok, blk_expert, n_active, wts['moe_w_in'], wts['moe_w_out'])
    return _moe_combine(ys, dest1, dest2, x1, x1b, p2d, info, wts)


def _rot_half_cols(w, heads, dim):
    w3 = w.reshape(w.shape[0], heads, dim)
    half = dim // 2
    return jnp.concatenate([-w3[..., half:], w3[..., :half]], axis=-1).reshape(w.shape[0], heads * dim)


def _prep_layer(i, w_in, conv_w, conv_b, conv_ln_g, conv_ln_b, nsa_cmp_pe, nsa_cmp_w1, nsa_cmp_w2,
                mla_q_norm, mla_kv_norm, mla_w_uq, mla_w_ukv, w_branch, w_out, ln1_g, ln1_b,
                ple_w_gate, ple_w_proj, ln2_g, ln2_b):
    w = w_in[i]
    d = w.shape[0]
    z = lambda n: jnp.zeros((d, n), F32)
    dup = lambda a: jnp.concatenate([a, a], axis=1)
    c_glu, nq = w[:, 0:512], w[:, 512:768]
    nkv = w[:, 768:1152]
    k_cmp, v_cmp, k_slc, v_slc, k_win, v_win = [nkv[:, j * 64:(j + 1) * 64] for j in range(6)]
    ng = w[:, 1152:1164]
    mq, mkv, mkr = w[:, 1164:1420], w[:, 1420:1548], w[:, 1548:1580]
    sb = w[:, 1580:2348]
    bg = w[:, 2348:6444]
    cols = [c_glu, nq, _rot_half_cols(nq, NSA_HEADS, HEAD_DIM),
            dup(k_slc), dup(_rot_half_cols(k_slc, 1, HEAD_DIM)),
            dup(k_win), dup(_rot_half_cols(k_win, 1, HEAD_DIM)),
            dup(v_slc), dup(v_win), k_cmp, v_cmp, ng, z(LANES - 12), mq, mkv,
            z(64), mkr, z(32), z(64), _rot_half_cols(mkr, 1, MLA_ROPE), z(32), sb]
    w1 = jnp.concatenate(cols, axis=1).astype(BF16)
    assert w1.shape[1] == C_TOT

    inv32 = ROPE_THETA ** (-jnp.arange(HEAD_DIM // 2, dtype=F32) / (HEAD_DIM // 2))
    inv16 = ROPE_THETA ** (-jnp.arange(MLA_ROPE // 2, dtype=F32) / (MLA_ROPE // 2))
    inv_nsa = jnp.tile(inv32, 4)[None, :]
    inv_mla = jnp.concatenate([jnp.zeros((64,), F32), inv16, inv16, jnp.zeros((32,), F32)])[None, :]

    wuq = mla_w_uq[i].reshape(MLA_Q_RANK, MLA_HEADS, MLA_NOPE + MLA_ROPE)
    zq = jnp.zeros((MLA_Q_RANK, MLA_HEADS, 32), F32)
    wq = jnp.concatenate([wuq, zq], axis=-1).reshape(MLA_Q_RANK, MLA_HEADS * LANES)
    rope_rot = jnp.concatenate([-wuq[..., MLA_NOPE + 16:], wuq[..., MLA_NOPE:MLA_NOPE + 16]], axis=-1)
    wqr = jnp.concatenate([jnp.zeros((MLA_Q_RANK, MLA_HEADS, MLA_NOPE), F32), rope_rot, zq],
                          axis=-1).reshape(MLA_Q_RANK, MLA_HEADS * LANES)
    wukv = mla_w_ukv[i].reshape(MLA_KV_RANK, MLA_HEADS, MLA_NOPE + MLA_V)
    wk = jnp.concatenate([wukv[..., :MLA_NOPE], jnp.zeros((MLA_KV_RANK, MLA_HEADS, 64), F32)],
                         axis=-1).reshape(MLA_KV_RANK, MLA_HEADS * LANES)
    wv = wukv[..., MLA_NOPE:].reshape(MLA_KV_RANK, MLA_HEADS * MLA_V)

    pe = nsa_cmp_pe[i]
    pe_rows = pe.reshape(CMP_BLOCK, 2 * HEAD_DIM)
    pe_a = pe_rows[:CMP_STRIDE].reshape(1, CMP_STRIDE * LANES)
    pe_b = pe_rows[CMP_STRIDE:].reshape(1, CMP_STRIDE * LANES)
    w1c = nsa_cmp_w1[i].reshape(2, CMP_BLOCK, HEAD_DIM, HEAD_DIM)
    zblk = jnp.zeros((CMP_BLOCK, HEAD_DIM, HEAD_DIM), F32)
    w1full = jnp.concatenate([jnp.concatenate([w1c[0], zblk], axis=2),
                              jnp.concatenate([zblk, w1c[1]], axis=2)], axis=1)
    w1a = w1full[:CMP_STRIDE].reshape(CMP_STRIDE * LANES, LANES).astype(BF16)
    w1b = w1full[CMP_STRIDE:].reshape(CMP_STRIDE * LANES, LANES).astype(BF16)
    w2 = nsa_cmp_w2[i]
    z64 = jnp.zeros((HEAD_DIM, LANES), F32)
    w2k = jnp.concatenate([dup(w2[0]), z64], axis=0).astype(BF16)
    w2v = jnp.concatenate([z64, dup(w2[1])], axis=0).astype(BF16)

    return dict(
        w1=w1, inv_nsa=inv_nsa, inv_mla=inv_mla,
        mla_qn=mla_q_norm[i][None, :], mla_kvn=mla_kv_norm[i][None, :],
        wq=wq.astype(BF16), wqr=wqr.astype(BF16), wk=wk.astype(BF16), wv=wv.astype(BF16),
        conv_w=conv_w[i], conv_b=conv_b[i][None, :], conv_g=conv_ln_g[i][None, :], conv_beta=conv_ln_b[i][None, :],
        pe_a=pe_a, pe_b=pe_b, w1a=w1a, w1b=w1b, w2k=w2k, w2v=w2v,
        w_gate=bg.astype(BF16), w_branch=w_branch[i].astype(BF16), w_out=w_out[i].astype(BF16),
        ln1_g=ln1_g[i][None, :], ln1_b=ln1_b[i][None, :],
        ple_w_gate=ple_w_gate[i].astype(BF16), ple_w_proj=ple_w_proj[i].astype(BF16),
        ln2_g=ln2_g[i][None, :], ln2_b=ln2_b[i][None, :],
    )


def _tables(seq):
    n_cmp_rows = seq // CMP_STRIDE
    n_sel = seq // SEL_BLOCK
    cmp_start = jnp.arange(n_cmp_rows) * CMP_STRIDE
    sel_start = jnp.arange(LANES) * SEL_BLOCK
    n_cmp = (seq - CMP_BLOCK) // CMP_STRIDE + 1
    overlap = ((cmp_start[:, None] < sel_start[None, :] + SEL_BLOCK)
               & (cmp_start[:, None] + CMP_BLOCK > sel_start[None, :])
               & (jnp.arange(n_cmp_rows)[:, None] < n_cmp) & (jnp.arange(LANES)[None, :] < n_sel))
    kb = jnp.arange(seq // TK)[:, None, None]
    nn = jnp.arange(LANES)[None, :, None]
    ll = jnp.arange(TK)[None, None, :]
    expand = (kb * TK + ll) // SEL_BLOCK == nn
    jj = jnp.arange(TK)
    tri_keys = jj[:, None] >= jj[None, :]
    tt = jnp.arange(TR)
    tri_tokens = tt[None, :] < tt[:, None]
    return dict(overlap=overlap.astype(BF16), expand=expand.astype(BF16), tri_keys=tri_keys.astype(BF16),
                tri_tokens=tri_tokens.astype(BF16))


def kernel(x, p, positions, w_in, conv_w, conv_b, conv_ln_g, conv_ln_b, nsa_cmp_pe, nsa_cmp_w1, nsa_cmp_w2,
           mla_q_norm, mla_kv_norm, mla_w_uq, mla_w_ukv, w_branch, w_out, ln1_g, ln1_b, ffn_w_in, ffn_w_out,
           moe_router, moe_w_in, moe_w_out, ple_w_gate, ple_w_proj, ln2_g, ln2_b):
    batch, seq, _ = x.shape
    n = batch * seq
    tabs = _tables(seq)
    x2d = x.reshape(n, D_MODEL)
    pos2d = positions.reshape(n, 1)
    for i in range(DEPTH):
        wts = _prep_layer(i, w_in, conv_w, conv_b, conv_ln_g, conv_ln_b, nsa_cmp_pe, nsa_cmp_w1, nsa_cmp_w2,
                          mla_q_norm, mla_kv_norm, mla_w_uq, mla_w_ukv, w_branch, w_out, ln1_g, ln1_b,
                          ple_w_gate, ple_w_proj, ln2_g, ln2_b)
        wts['overlap'] = tabs['overlap']
        (conv_in, nq, nqr, ks, kw, vs, vw, kvc, ng, mq, mk, mv, sq, sk, sv) = _mixer_in(x2d, pos2d, wts)
        y_a = _conv(conv_in, wts['conv_w'], wts['conv_b'], wts['conv_g'], wts['conv_beta'], batch, seq)
        ocmp, sel = _nsa_cmp(kvc, nq, wts, batch, seq)
        y_b = _nsa_attn(nqr, ks, vs, kw, vw, sel, tabs['expand'], ocmp, ng, batch, seq)
        y_c = _mla_attn(mq, mk, mv, batch, seq)
        y_d = _sb_attn(sq, sk, sv, tabs['tri_keys'], batch, seq)
        ys = [y.reshape(n, BRANCH_W) for y in (y_a, y_b, y_c, y_d)]
        x1, x1b = _merge(x2d, ys, wts)
        p2d = p[i].reshape(n, P_DIM)
        if i % 2 == 0:
            wts['ffn_w_in'] = ffn_w_in[i // 2].astype(BF16)
            wts['ffn_w_out'] = ffn_w_out[i // 2].astype(BF16)
            x2d = _ffn_dense(x1, x1b, p2d, wts)
        else:
            wts['w_router'] = jnp.concatenate(
                [moe_router[i // 2], jnp.zeros((D_MODEL, LANES - N_EXPERTS), F32)], axis=1)
            wts['tri_tokens'] = tabs['tri_tokens']
            wts['moe_w_in'] = moe_w_in[i // 2].astype(BF16)
            wts['moe_w_out'] = moe_w_out[i // 2].astype(BF16)
            x2d = _moe(x1, x1b, p2d, wts)
    return x2d.reshape(batch, seq, D_MODEL)
```

```python
import functools

import jax
import jax.numpy as jnp
from jax import lax
from jax.experimental import pallas as pl
from jax.experimental.pallas import tpu as pltpu

F32 = jnp.float32
BF16 = jnp.bfloat16

D_MODEL = 1024
DEPTH = 2
CONV_CH = 256
CONV_WIDTH = 31
NSA_HEADS = 4
HEAD_DIM = 64
CMP_BLOCK = 32
CMP_STRIDE = 16
SEL_BLOCK = 64
SEL_TOPN = 16
WINDOW = 512
MLA_HEADS = 4
MLA_Q_RANK = 256
MLA_KV_RANK = 128
MLA_NOPE = 64
MLA_ROPE = 32
MLA_V = 64
N_BRANCH = 4
BRANCH_W = 256
ROPE_THETA = 10000.0
LN_EPS = 1e-5
RMS_EPS = 1e-6
D_FF = 2816
N_EXPERTS = 8
D_FF_EXPERT = 3584
MOE_BLOCK = 512
P_DIM = 256
DEEPNORM_ALPHA = (2 * DEPTH) ** 0.25

LANES = 128
VMEM_LIMIT = 56 * 1024 * 1024

NEG = -1e30

C_CONV = 0
C_NQ = 512
C_NQR = 768
C_KS = 1024
C_KW = 1280
C_VS = 1536
C_KVC = 1792
C_MQ = 2048
C_MKV = 2304
C_MKRR = 2560
C_SB = 2816
C_TOT = 3584

TM = 256
TQ = 256
TK = 256


def _cparams(sem, vmem=VMEM_LIMIT):
    return pltpu.CompilerParams(dimension_semantics=sem, vmem_limit_bytes=vmem)


def _const_spec(shape):
    nd = len(shape)
    return pl.BlockSpec(shape, lambda *_: (0,) * nd, pipeline_mode=pl.Buffered(1))


def _dot(a, b):
    return jnp.dot(a, b, preferred_element_type=F32)


def _dot_nt(a, b):
    return lax.dot_general(a, b, (((1,), (1,)), ((), ())), preferred_element_type=F32)


def _layer_norm(h, g, b):
    mu = jnp.mean(h, axis=-1, keepdims=True)
    d = h - mu
    var = jnp.mean(d * d, axis=-1, keepdims=True)
    return d * lax.rsqrt(var + LN_EPS) * g + b


def _rms_norm(h, g):
    return h * lax.rsqrt(jnp.mean(h * h, axis=-1, keepdims=True) + RMS_EPS) * g


def _sigmoid(x):
    return 1.0 / (1.0 + jnp.exp(-x))


def _silu(x):
    return x * _sigmoid(x)


def _split_bf16(x):
    hi = x.astype(BF16)
    lo = (x - hi.astype(F32)).astype(BF16)
    return hi, lo


def _half_select(sub, x):
    lane = lax.broadcasted_iota(jnp.int32, x.shape, 1)
    keep = (lane < HEAD_DIM) if sub == 0 else (lane >= HEAD_DIM)
    return jnp.where(keep, x, 0.0)


def _mixer_in_kernel(x_ref, pos_ref, w_ref, invn_ref, invm_ref, qn_ref, kvn_ref, wq_ref, wqr_ref,
                     wk_ref, wv_ref,
                     conv_ref, nq_ref, nqr_ref, ks_ref, kw_ref, vs_ref, vw_ref, kvc_ref, ng_ref,
                     mq_ref, mk_ref, mv_ref, sq_ref, sk_ref, sv_ref):
    xb = x_ref[...].astype(BF16)

    def proj(c0, width):
        return _dot(xb, w_ref[:, c0:c0 + width])

    posf = pos_ref[...].astype(F32)
    ang_n = posf * invn_ref[...]
    cos_n, sin_n = jnp.cos(ang_n), jnp.sin(ang_n)
    ang_m = posf * invm_ref[...]
    cos_m, sin_m = jnp.cos(ang_m), jnp.sin(ang_m)

    conv_ref[...] = proj(C_CONV, 512)

    scale = HEAD_DIM ** -0.5
    lo, hi = slice(0, LANES), slice(LANES, 2 * LANES)
    cos_n2 = jnp.concatenate([cos_n, cos_n], axis=1)
    sin_n2 = jnp.concatenate([sin_n, sin_n], axis=1)
    q, qrot = proj(C_NQ, 256), proj(C_NQR, 256)
    nq_ref[...] = (q * scale).astype(BF16)
    nqr_ref[...] = ((q * cos_n2 + qrot * sin_n2) * scale).astype(BF16)
    k = proj(C_KS, 256)
    ks_ref[...] = (k[:, lo] * cos_n + k[:, hi] * sin_n).astype(BF16)
    k = proj(C_KW, 256)
    kw_ref[...] = (k[:, lo] * cos_n + k[:, hi] * sin_n).astype(BF16)
    v = proj(C_VS, 256)
    vs_ref[...] = v[:, lo].astype(BF16)
    vw_ref[...] = v[:, hi].astype(BF16)
    u = proj(C_KVC, 256)
    kvc_ref[...] = u[:, lo]
    ng_ref[...] = u[:, hi]

    qn = _rms_norm(proj(C_MQ, MLA_Q_RANK), qn_ref[...]).astype(BF16)
    u = proj(C_MKV, 256)
    kvn = _rms_norm(u[:, lo], kvn_ref[...]).astype(BF16)
    kr = u[:, hi] * cos_m + proj(C_MKRR, 256)[:, lo] * sin_m
    qa, qr, kk = _dot(qn, wq_ref[...]), _dot(qn, wqr_ref[...]), _dot(kvn, wk_ref[...])
    for h in range(MLA_HEADS):
        sl = slice(h * LANES, (h + 1) * LANES)
        mq_ref[:, sl] = (qa[:, sl] * cos_m + qr[:, sl] * sin_m).astype(BF16)
        mk_ref[:, sl] = (kk[:, sl] + kr).astype(BF16)
    mv_ref[...] = _dot(kvn, wv_ref[...]).astype(BF16)

    sq_ref[...] = (proj(C_SB, 256) * scale).astype(BF16)
    sk_ref[...] = proj(C_SB + 256, 256).astype(BF16)
    sv_ref[...] = proj(C_SB + 512, 256).astype(BF16)


def _mixer_in(x2d, pos2d, wts):
    n = x2d.shape[0]
    row = lambda w: pl.BlockSpec((TM, w), lambda i: (i, 0))
    out_widths = [512, 256, 256, 128, 128, 128, 128, 128, 128, 512, 512, 256, 256, 256, 256]
    out_dtypes = [F32, BF16, BF16, BF16, BF16, BF16, BF16, F32, F32, BF16, BF16, BF16, BF16, BF16, BF16]
    consts = [wts['w1'], wts['inv_nsa'], wts['inv_mla'], wts['mla_qn'], wts['mla_kvn'], wts['wq'],
              wts['wqr'], wts['wk'], wts['wv']]
    return pl.pallas_call(
        _mixer_in_kernel,
        grid=(n // TM,),
        in_specs=[row(D_MODEL), row(1)] + [_const_spec(c.shape) for c in consts],
        out_specs=[row(w) for w in out_widths],
        out_shape=[jax.ShapeDtypeStruct((n, w), d) for w, d in zip(out_widths, out_dtypes)],
        compiler_params=_cparams(("parallel",)),
        name="mixer_in",
    )(x2d, pos2d, *consts)


CONV_PAD = 32
CONV_CHUNK = 128


def _conv_kernel(u_ref, w_ref, b_ref, g_ref, beta_ref, o_ref, hp_ref):
    seq = u_ref.shape[0]
    hp_ref[0:CONV_PAD, :] = jnp.zeros((CONV_PAD, CONV_CH), F32)
    hp_ref[CONV_PAD:CONV_PAD + seq, :] = u_ref[:, 0:CONV_CH] * _sigmoid(u_ref[:, CONV_CH:2 * CONV_CH])
    shift = CONV_PAD - (CONV_WIDTH - 1)
    for c in range(seq // CONV_CHUNK):
        base = c * CONV_CHUNK
        acc = jnp.broadcast_to(b_ref[...], (CONV_CHUNK, CONV_CH))
        for j in range(CONV_WIDTH):
            acc = acc + hp_ref[base + shift + j:base + shift + j + CONV_CHUNK, :] * w_ref[j:j + 1, :]
        y = _layer_norm(acc, g_ref[...], beta_ref[...])
        o_ref[base:base + CONV_CHUNK, :] = _silu(y).astype(BF16)


def _conv(conv_in, w, b, g, beta, batch, seq):
    return pl.pallas_call(
        _conv_kernel,
        grid=(batch,),
        in_specs=[pl.BlockSpec((None, seq, 2 * CONV_CH), lambda i: (i, 0, 0)),
                  _const_spec(w.shape), _const_spec(b.shape), _const_spec(g.shape), _const_spec(beta.shape)],
        out_specs=pl.BlockSpec((None, seq, CONV_CH), lambda i: (i, 0, 0)),
        out_shape=jax.ShapeDtypeStruct((batch, seq, CONV_CH), BF16),
        scratch_shapes=[pltpu.VMEM((CONV_PAD + seq, CONV_CH), F32)],
        compiler_params=_cparams(("parallel",)),
        name="conformer_conv",
    )(conv_in.reshape(batch, seq, 2 * CONV_CH), w, b, g, beta)


def _gelu_tanh(x):
    return 0.5 * x * (1.0 + jnp.tanh(0.7978845608028654 * (x + 0.044715 * x * x * x)))


SEL_SHIFT = 6


def _nsa_cmp_kernel(kvc_ref, q_ref, pea_ref, peb_ref, w1a_ref, w1b_ref, w2k_ref, w2v_ref, ovt_ref, eye_ref,
                    ocmp_ref, sel_ref, *, seq):
    n_cmp = (seq - CMP_BLOCK) // CMP_STRIDE + 1
    n_sel = seq // SEL_BLOCK
    nb = seq // CMP_STRIDE
    x2 = kvc_ref[...]
    xa = (x2 + pea_ref[...]).astype(BF16)
    xb = (x2 + peb_ref[...]).astype(BF16)
    ha = _dot(xa, w1a_ref[...])
    hb = _dot(xb, w1b_ref[...])
    hid = ha + pltpu.roll(hb, nb - 1, 0)
    hid = _gelu_tanh(hid).astype(BF16)
    kk = _dot(hid, w2k_ref[...]).astype(BF16)
    vv = _dot(hid, w2v_ref[...]).astype(BF16)
    ovt = ovt_ref[...]

    for c in range(seq // TQ):
        r0 = c * TQ
        t = r0 + lax.broadcasted_iota(jnp.int32, (TQ, LANES), 0)
        j = lax.broadcasted_iota(jnp.int32, (TQ, LANES), 1)
        valid = (j * CMP_STRIDE + CMP_BLOCK - 1 <= t) & (j < n_cmp)
        psum = jnp.zeros((TQ, LANES), F32)
        for pair in range(2):
            qp = q_ref[r0:r0 + TQ, pair * LANES:(pair + 1) * LANES].astype(F32)
            outs = []
            for sub in range(2):
                qm = _half_select(sub, qp).astype(BF16)
                s = jnp.where(valid, _dot_nt(qm, kk), NEG)
                m = jnp.max(s, axis=-1, keepdims=True)
                e = jnp.where(valid, jnp.exp(s - m), 0.0)
                den = jnp.sum(e, axis=-1, keepdims=True)
                p = e / jnp.where(den > 0, den, 1.0)
                psum = psum + p
                outs.append(_dot(p.astype(BF16), vv))
            ocmp_ref[r0:r0 + TQ, pair * LANES:(pair + 1) * LANES] = jnp.where(
                lax.broadcasted_iota(jnp.int32, (TQ, LANES), 1) < HEAD_DIM, outs[0], outs[1])
        p_hi, p_lo = _split_bf16(psum)
        imp = (_dot_nt(ovt, p_hi) + _dot_nt(ovt, p_lo))[0:n_sel, :]
        n = lax.broadcasted_iota(jnp.int32, (n_sel, TQ), 0)
        cur = jnp.right_shift(r0 + lax.broadcasted_iota(jnp.int32, (n_sel, TQ), 1), SEL_SHIFT)
        forced = (n == 0) | (n == cur) | (n == cur - 1)
        imp = jnp.where(forced, jnp.inf, imp)
        imp = jnp.where(n > cur, -jnp.inf, imp)
        rank = jnp.zeros((n_sel, TQ), F32)
        for n2 in range(n_sel):
            other = imp[n2:n2 + 1, :]
            ahead = (other > imp) | ((other == imp) & (n2 < n))
            rank = rank + jnp.where(ahead, 1.0, 0.0)
        sel_t = jnp.where((rank < SEL_TOPN) & (imp > -jnp.inf), 1.0, 0.0)
        sel_t = jnp.concatenate([sel_t, jnp.zeros((LANES - n_sel, TQ), F32)], axis=0).astype(BF16)
        sel_ref[r0:r0 + TQ, :] = _dot_nt(eye_ref[...], sel_t).astype(BF16)


def _nsa_cmp(kvc, nq, wts, batch, seq):
    nb = seq // CMP_STRIDE
    consts = [wts['pe_a'], wts['pe_b'], wts['w1a'], wts['w1b'], wts['w2k'], wts['w2v'], wts['overlap_t'],
              wts['eye_q']]
    return pl.pallas_call(
        functools.partial(_nsa_cmp_kernel, seq=seq),
        grid=(batch,),
        in_specs=[pl.BlockSpec((None, nb, CMP_STRIDE * LANES), lambda i: (i, 0, 0)),
                  pl.BlockSpec((None, seq, 256), lambda i: (i, 0, 0))] + [_const_spec(c.shape) for c in consts],
        out_specs=[pl.BlockSpec((None, seq, 256), lambda i: (i, 0, 0)),
                   pl.BlockSpec((None, seq, LANES), lambda i: (i, 0, 0))],
        out_shape=[jax.ShapeDtypeStruct((batch, seq, 256), F32),
                   jax.ShapeDtypeStruct((batch, seq, LANES), BF16)],
        compiler_params=_cparams(("parallel",)),
        name="nsa_compress_select",
    )(kvc.reshape(batch, nb, CMP_STRIDE * LANES), nq.reshape(batch, seq, 256), *consts)


LOG2E = 1.4426950408889634
N_HEADS = 4


def _softmax_scratch(n_tiles):
    slab = pltpu.VMEM((N_HEADS, TQ, LANES), F32)
    return [pltpu.VMEM((N_HEADS, n_tiles, TQ, TK), F32), slab, slab, slab, slab]


def _scores_put(h, t, s, s_ref, mx_ref):
    s_ref[h, t] = s
    mx_ref[h] = jnp.maximum(mx_ref[h], jnp.maximum(s[:, :LANES], s[:, LANES:]))


def _row_max(mx_ref, mb_ref):
    for h in range(N_HEADS):
        mb_ref[h] = jnp.broadcast_to(jnp.max(mx_ref[h], axis=-1, keepdims=True), (TQ, LANES))


def _probs_accumulate(h, t, c, v_blk, s_ref, mb_ref, ls_ref, acc_ref):
    s, mb = s_ref[h, t], mb_ref[h]
    pa = jnp.exp2((s[:, :LANES] - mb) * c)
    pb = jnp.exp2((s[:, LANES:] - mb) * c)
    ls_ref[h] += pa + pb
    acc_ref[h] += _dot(jnp.concatenate([pa, pb], axis=1).astype(BF16), v_blk)


def _softmax_out(h, ls_ref, acc_ref):
    return acc_ref[h] / jnp.sum(ls_ref[h], axis=-1, keepdims=True)


def _tile_iotas():
    return (lax.broadcasted_iota(jnp.int32, (TQ, TK), 0), lax.broadcasted_iota(jnp.int32, (TQ, TK), 1))


def _nsa_attn_kernel(q_ref, ks_ref, vs_ref, kw_ref, vw_ref, sel_ref, exp_ref, ocmp_ref, ng_ref, o_ref,
                     qm_ref, ss_ref, mxs_ref, mbs_ref, lss_ref, accs_ref,
                     sw_ref, mxw_ref, mbw_ref, lsw_ref, accw_ref):
    i = pl.program_id(1)
    row, col = _tile_iotas()
    sel = sel_ref[...]
    for ref in (mxs_ref, mxw_ref):
        ref[...] = jnp.full(ref.shape, NEG, F32)
    for ref in (lss_ref, accs_ref, lsw_ref, accw_ref):
        ref[...] = jnp.zeros(ref.shape, F32)
    for pair in range(2):
        qp = q_ref[:, pair * LANES:(pair + 1) * LANES].astype(F32)
        for sub in range(2):
            qm_ref[2 * pair + sub] = _half_select(sub, qp).astype(BF16)

    def selected_scores(kb, causal):
        k0 = pl.multiple_of(kb * TK, TK)
        hit = _dot(sel, exp_ref[kb]) > 0.5
        if causal:
            hit = hit & (col <= row)
        bias = jnp.where(hit, 0.0, NEG)
        k_blk = ks_ref[pl.ds(k0, TK), :]
        for h in range(N_HEADS):
            _scores_put(h, kb, _dot_nt(qm_ref[h], k_blk) + bias, ss_ref, mxs_ref)

    def window_scores(slot, mask):
        k0 = pl.multiple_of((i - 2 + slot) * TK, TK)
        k_blk = kw_ref[pl.ds(k0, TK), :]
        for h in range(N_HEADS):
            s = _dot_nt(qm_ref[h], k_blk)
            if mask is not None:
                s = jnp.where(mask, s, NEG)
            _scores_put(h, slot, s, sw_ref, mxw_ref)

    def off_diagonal(kb, _):
        selected_scores(kb, False)
        return 0

    lax.fori_loop(0, i, off_diagonal, 0)
    selected_scores(i, True)
    pl.when(i >= 2)(lambda: window_scores(0, col > row))
    pl.when(i >= 1)(lambda: window_scores(1, None))
    window_scores(2, col <= row)
    _row_max(mxs_ref, mbs_ref)
    _row_max(mxw_ref, mbw_ref)

    def selected_probs(kb, _):
        v_blk = vs_ref[pl.ds(pl.multiple_of(kb * TK, TK), TK), :]
        for h in range(N_HEADS):
            _probs_accumulate(h, kb, LOG2E, v_blk, ss_ref, mbs_ref, lss_ref, accs_ref)
        return 0

    def window_probs(slot):
        v_blk = vw_ref[pl.ds(pl.multiple_of((i - 2 + slot) * TK, TK), TK), :]
        for h in range(N_HEADS):
            _probs_accumulate(h, slot, LOG2E, v_blk, sw_ref, mbw_ref, lsw_ref, accw_ref)

    lax.fori_loop(0, i + 1, selected_probs, 0)
    pl.when(i >= 2)(lambda: window_probs(0))
    pl.when(i >= 1)(lambda: window_probs(1))
    window_probs(2)

    g = _sigmoid(ng_ref[...])
    lane = lax.broadcasted_iota(jnp.int32, (TQ, LANES), 1)
    for pair in range(2):
        res = []
        for sub in range(2):
            h = 2 * pair + sub
            res.append(g[:, 3 * h + 1:3 * h + 2] * _softmax_out(h, lss_ref, accs_ref)
                       + g[:, 3 * h + 2:3 * h + 3] * _softmax_out(h, lsw_ref, accw_ref)
                       + g[:, 3 * h:3 * h + 1] * ocmp_ref[:, pair * LANES:(pair + 1) * LANES])
        o_ref[:, pair * LANES:(pair + 1) * LANES] = jnp.where(lane < HEAD_DIM, res[0], res[1]).astype(BF16)


def _nsa_attn(nqr, ks, vs, kw, vw, sel, expand, ocmp, ng, batch, seq):
    qspec = lambda w: pl.BlockSpec((None, TQ, w), lambda b, i: (b, i, 0))
    kspec = pl.BlockSpec((None, seq, LANES), lambda b, i: (b, 0, 0))
    r3 = lambda a: a.reshape(batch, seq, a.shape[-1])
    return pl.pallas_call(
        _nsa_attn_kernel,
        grid=(batch, seq // TQ),
        in_specs=[qspec(256), kspec, kspec, kspec, kspec, qspec(LANES), _const_spec(expand.shape),
                  qspec(256), qspec(LANES)],
        out_specs=qspec(256),
        out_shape=jax.ShapeDtypeStruct((batch, seq, 256), BF16),
        scratch_shapes=[pltpu.VMEM((N_HEADS, TQ, LANES), BF16)] + _softmax_scratch(seq // TK)
        + _softmax_scratch(WINDOW // TK + 1),
        compiler_params=_cparams(("parallel", "parallel")),
        name="nsa_select_window",
    )(r3(nqr), r3(ks), r3(vs), r3(kw), r3(vw), sel, expand, ocmp, r3(ng))


def _mla_attn_kernel(q_ref, k_ref, v_ref, o_ref, s_ref, mx_ref, mb_ref, ls_ref, acc_ref):
    i = pl.program_id(1)
    c = (MLA_NOPE + MLA_ROPE) ** -0.5 * LOG2E
    row, col = _tile_iotas()
    mx_ref[...] = jnp.full(mx_ref.shape, NEG, F32)
    ls_ref[...] = jnp.zeros(ls_ref.shape, F32)
    acc_ref[...] = jnp.zeros(acc_ref.shape, F32)

    def scores(kb, causal):
        k0 = pl.multiple_of(kb * TK, TK)
        for h in range(N_HEADS):
            hs = slice(h * LANES, (h + 1) * LANES)
            s = _dot_nt(q_ref[:, hs], k_ref[pl.ds(k0, TK), hs])
            if causal:
                s = jnp.where(col <= row, s, NEG)
            _scores_put(h, kb, s, s_ref, mx_ref)

    def off_diagonal(kb, _):
        scores(kb, False)
        return 0

    lax.fori_loop(0, i, off_diagonal, 0)
    scores(i, True)
    _row_max(mx_ref, mb_ref)

    def probs(kb, _):
        k0 = pl.multiple_of(kb * TK, TK)
        for h in range(N_HEADS):
            v_blk = v_ref[pl.ds(k0, TK), (h // 2) * LANES:(h // 2 + 1) * LANES]
            _probs_accumulate(h, kb, c, v_blk, s_ref, mb_ref, ls_ref, acc_ref)
        return 0

    lax.fori_loop(0, i + 1, probs, 0)
    lane = lax.broadcasted_iota(jnp.int32, (TQ, LANES), 1)
    for pair in range(2):
        o_ref[:, pair * LANES:(pair + 1) * LANES] = jnp.where(
            lane < HEAD_DIM, _softmax_out(2 * pair, ls_ref, acc_ref),
            _softmax_out(2 * pair + 1, ls_ref, acc_ref)).astype(BF16)


def _mla_attn(mq, mk, mv, batch, seq):
    r3 = lambda a: a.reshape(batch, seq, a.shape[-1])
    return pl.pallas_call(
        _mla_attn_kernel,
        grid=(batch, seq // TQ),
        in_specs=[pl.BlockSpec((None, TQ, 512), lambda b, i: (b, i, 0)),
                  pl.BlockSpec((None, seq, 512), lambda b, i: (b, 0, 0)),
                  pl.BlockSpec((None, seq, 256), lambda b, i: (b, 0, 0))],
        out_specs=pl.BlockSpec((None, TQ, 256), lambda b, i: (b, i, 0)),
        out_shape=jax.ShapeDtypeStruct((batch, seq, 256), BF16),
        scratch_shapes=_softmax_scratch(seq // TK),
        compiler_params=_cparams(("parallel", "parallel")),
        name="mla_attention",
    )(r3(mq), r3(mk), r3(mv))


def _sb_attn_kernel(q_ref, k_ref, v_ref, tri_ref, o_ref, qm_ref, e_ref, tail_ref, acc_ref):
    i = pl.program_id(1)
    row, col = _tile_iotas()
    tail_ref[...] = jnp.zeros(tail_ref.shape, F32)
    acc_ref[...] = jnp.zeros(acc_ref.shape, F32)
    for pair in range(2):
        qp = q_ref[:, pair * LANES:(pair + 1) * LANES].astype(F32)
        for sub in range(2):
            qm_ref[2 * pair + sub] = _half_select(sub, qp).astype(BF16)

    def log_weights(kb, diagonal):
        k0 = pl.multiple_of(kb * TK, TK)
        tri2 = tri_ref[...]
        heads = range(N_HEADS)
        zs = [_dot_nt(qm_ref[h], k_ref[pl.ds(k0, TK), (h // 2) * LANES:(h // 2 + 1) * LANES]) for h in heads]
        lks = [jnp.minimum(z, 0.0) - jnp.log(1.0 + jnp.exp(-jnp.abs(z))) - z for z in zs]
        if diagonal:
            lks = [jnp.where(col < row, lk, 0.0) for lk in lks]
        incls = [_dot(jnp.concatenate(_split_bf16(lk), axis=1), tri2) for lk in lks]
        for h in heads:
            tail = tail_ref[h]
            e = zs[h] + incls[h] + jnp.concatenate([tail, tail], axis=1)
            if diagonal:
                e = jnp.where(col < row, e, NEG)
            e_ref[h, kb] = e
            tail_ref[h] = tail + jnp.broadcast_to(incls[h][:, 0:1], (TQ, LANES))

    log_weights(i, True)

    def off_diagonal(step, _):
        log_weights(i - 1 - step, False)
        return 0

    lax.fori_loop(0, i, off_diagonal, 0)

    def weighted_values(kb, _):
        k0 = pl.multiple_of(kb * TK, TK)
        for h in range(N_HEADS):
            ps = slice((h // 2) * LANES, (h // 2 + 1) * LANES)
            acc_ref[h] += _dot(jnp.exp(e_ref[h, kb]).astype(BF16), v_ref[pl.ds(k0, TK), ps])
        return 0

    lax.fori_loop(0, i + 1, weighted_values, 0)
    lane = lax.broadcasted_iota(jnp.int32, (TQ, LANES), 1)
    for pair in range(2):
        o_ref[:, pair * LANES:(pair + 1) * LANES] = jnp.where(
            lane < HEAD_DIM, acc_ref[2 * pair], acc_ref[2 * pair + 1]).astype(BF16)


def _sb_attn(sq, sk, sv, tri, batch, seq):
    r3 = lambda a: a.reshape(batch, seq, a.shape[-1])
    kspec = pl.BlockSpec((None, seq, 256), lambda b, i: (b, 0, 0))
    qspec = pl.BlockSpec((None, TQ, 256), lambda b, i: (b, i, 0))
    return pl.pallas_call(
        _sb_attn_kernel,
        grid=(batch, seq // TQ),
        in_specs=[qspec, kspec, kspec, _const_spec(tri.shape)],
        out_specs=qspec,
        out_shape=jax.ShapeDtypeStruct((batch, seq, 256), BF16),
        scratch_shapes=[pltpu.VMEM((N_HEADS, TQ, LANES), BF16), pltpu.VMEM((N_HEADS, seq // TK, TQ, TK), F32),
                        pltpu.VMEM((N_HEADS, TQ, LANES), F32), pltpu.VMEM((N_HEADS, TQ, LANES), F32)],
        compiler_params=_cparams(("parallel", "parallel")),
        name="stick_breaking_attention",
    )(r3(sq), r3(sk), r3(sv), tri)


def _merge_kernel(x_ref, ya_ref, yb_ref, yc_ref, yd_ref, wg_ref, wb_ref, wo_ref, g_ref, b_ref,
                  o_ref, ob_ref):
    x = x_ref[...]
    xb = x.astype(BF16)
    mixed = jnp.zeros((TM, D_MODEL), F32)
    for n, y_ref in enumerate((ya_ref, yb_ref, yc_ref, yd_ref)):
        gate = _sigmoid(_dot(xb, wg_ref[:, n * D_MODEL:(n + 1) * D_MODEL]))
        mixed = mixed + gate * _dot(y_ref[...], wb_ref[n])
    h = DEEPNORM_ALPHA * x + _dot(mixed.astype(BF16), wo_ref[...])
    out = _layer_norm(h, g_ref[...], b_ref[...])
    o_ref[...] = out
    ob_ref[...] = out.astype(BF16)


def _merge(x2d, ys, wts):
    n = x2d.shape[0]
    row = lambda w: pl.BlockSpec((TM, w), lambda i: (i, 0))
    consts = [wts['w_gate'], wts['w_branch'], wts['w_out'], wts['ln1_g'], wts['ln1_b']]
    return pl.pallas_call(
        _merge_kernel,
        grid=(n // TM,),
        in_specs=[row(D_MODEL)] + [row(BRANCH_W)] * 4 + [_const_spec(c.shape) for c in consts],
        out_specs=[row(D_MODEL), row(D_MODEL)],
        out_shape=[jax.ShapeDtypeStruct((n, D_MODEL), F32), jax.ShapeDtypeStruct((n, D_MODEL), BF16)],
        compiler_params=_cparams(("parallel",)),
        name="branch_merge_ln1",
    )(x2d, *ys, *consts)


def _ple_ln2(x1, x1b, f, p_ref, wpg_ref, wpp_ref, g_ref, b_ref):
    ple = _sigmoid(_dot(x1b, wpg_ref[...])) * _dot(p_ref[...].astype(BF16), wpp_ref[...])
    return _layer_norm(DEEPNORM_ALPHA * x1 + f + ple, g_ref[...], b_ref[...])


FF_CHUNK = 256


def _ffn_dense_kernel(x_ref, xb_ref, p_ref, wi_ref, wo_ref, wpg_ref, wpp_ref, g_ref, b_ref, o_ref, acc_ref):
    xb = xb_ref[...]
    for c in range(D_FF // FF_CHUNK):
        a = _dot(xb, wi_ref[:, c * FF_CHUNK:(c + 1) * FF_CHUNK])
        u = _dot(xb, wi_ref[:, D_FF + c * FF_CHUNK:D_FF + (c + 1) * FF_CHUNK])
        part = _dot((_silu(a) * u).astype(BF16), wo_ref[c * FF_CHUNK:(c + 1) * FF_CHUNK, :])
        if c == 0:
            acc_ref[...] = part
        else:
            acc_ref[...] += part
    o_ref[...] = _ple_ln2(x_ref[...], xb, acc_ref[...], p_ref, wpg_ref, wpp_ref, g_ref, b_ref)


def _ffn_dense(x1, x1b, p2d, wts):
    n = x1.shape[0]
    row = lambda w: pl.BlockSpec((TM, w), lambda i: (i, 0))
    consts = [wts['ffn_w_in'], wts['ffn_w_out'], wts['ple_w_gate'], wts['ple_w_proj'], wts['ln2_g'], wts['ln2_b']]
    return pl.pallas_call(
        _ffn_dense_kernel,
        grid=(n // TM,),
        in_specs=[row(D_MODEL), row(D_MODEL), row(P_DIM)] + [_const_spec(c.shape) for c in consts],
        out_specs=row(D_MODEL),
        out_shape=jax.ShapeDtypeStruct((n, D_MODEL), F32),
        scratch_shapes=[pltpu.VMEM((TM, D_MODEL), F32)],
        compiler_params=_cparams(("parallel",)),
        name="ffn_dense_ple_ln2",
    )(x1, x1b, p2d, *consts)


TR = 512
INFO_LANES = 6


def _moe_route_kernel(x_ref, wr_ref, tri_ref, info_ref, cnt_ref, run_ref):
    @pl.when(pl.program_id(0) == 0)
    def _():
        run_ref[...] = jnp.zeros_like(run_ref)

    logits = jnp.dot(x_ref[...], wr_ref[...], precision=lax.Precision.HIGHEST, preferred_element_type=F32)
    lane = lax.broadcasted_iota(jnp.int32, (TR, LANES), 1)
    lane_f = lane.astype(F32)
    logits = jnp.where(lane < N_EXPERTS, logits, NEG)
    m1 = jnp.max(logits, axis=-1, keepdims=True)
    i1 = jnp.min(jnp.where(logits == m1, lane_f, float(LANES)), axis=-1, keepdims=True)
    rest = jnp.where(lane_f == i1, NEG, logits)
    m2 = jnp.max(rest, axis=-1, keepdims=True)
    i2 = jnp.min(jnp.where(rest == m2, lane_f, float(LANES)), axis=-1, keepdims=True)
    e = jnp.exp(m2 - m1)
    g1 = 1.0 / (1.0 + e)
    g2 = e / (1.0 + e)
    hot1 = lane_f == i1
    hot2 = lane_f == i2
    onehot = jnp.where(hot1 | hot2, 1.0, 0.0)
    before = _dot(tri_ref[...], onehot.astype(BF16)) + run_ref[0:1, :]
    r1 = jnp.sum(jnp.where(hot1, before, 0.0), axis=-1, keepdims=True)
    r2 = jnp.sum(jnp.where(hot2, before, 0.0), axis=-1, keepdims=True)
    run_ref[0:1, :] = run_ref[0:1, :] + jnp.sum(onehot, axis=0, keepdims=True)
    info = jnp.zeros((TR, LANES), F32)
    for k, val in enumerate((i1, i2, r1, r2, g1, g2)):
        info = jnp.where(lane == k, val, info)
    info_ref[...] = info
    cnt_ref[...] = jnp.broadcast_to(run_ref[0:1, :], cnt_ref.shape)


def _moe_route(x1, w_router_pad, tri):
    n = x1.shape[0]
    return pl.pallas_call(
        _moe_route_kernel,
        grid=(n // TR,),
        in_specs=[pl.BlockSpec((TR, D_MODEL), lambda i: (i, 0)), _const_spec(w_router_pad.shape),
                  _const_spec(tri.shape)],
        out_specs=[pl.BlockSpec((TR, LANES), lambda i: (i, 0)), pl.BlockSpec((8, LANES), lambda i: (0, 0))],
        out_shape=[jax.ShapeDtypeStruct((n, LANES), F32), jax.ShapeDtypeStruct((8, LANES), F32)],
        scratch_shapes=[pltpu.VMEM((8, LANES), F32)],
        compiler_params=_cparams(("arbitrary",)),
        name="moe_router_rank",
    )(x1, w_router_pad, tri)


def _moe_rowmap_kernel(d1_ref, d2_ref, rt_ref):
    def clear(r, _):
        rt_ref[r] = 0
        return 0

    lax.fori_loop(0, rt_ref.shape[0], clear, 0, unroll=16)

    def place(t, _):
        rt_ref[d1_ref[t]] = t
        rt_ref[d2_ref[t]] = t
        return 0

    lax.fori_loop(0, d1_ref.shape[0], place, 0, unroll=8)


def _moe_rowmap(dest1, dest2, n_rows):
    smem = pl.BlockSpec(memory_space=pltpu.SMEM)
    return pl.pallas_call(
        _moe_rowmap_kernel,
        in_specs=[smem, smem],
        out_specs=smem,
        out_shape=jax.ShapeDtypeStruct((n_rows,), jnp.int32),
        name="moe_row_map",
    )(dest1, dest2)


EF_CHUNK = 896
EF_STEPS = D_FF_EXPERT // EF_CHUNK
ROWS_PER_STEP = MOE_BLOCK // EF_STEPS


def _moe_ffn_kernel(be_ref, na_ref, rt_ref, x_hbm, wa_ref, wu_ref, wo_ref, ys_ref, xs_ref, acc_ref, sem):
    del be_ref
    blk, c = pl.program_id(0), pl.program_id(1)
    n_live = na_ref[0]
    last_step = pl.num_programs(1) - 1
    slot = blk % 2

    def row_copy(block, r, s):
        return pltpu.make_async_copy(x_hbm.at[pl.ds(rt_ref[block * MOE_BLOCK + r], 1), :],
                                     xs_ref.at[s, pl.ds(r, 1), :], sem.at[s])

    def wait_block(s):
        pltpu.make_async_copy(x_hbm.at[pl.ds(0, MOE_BLOCK), :], xs_ref.at[s], sem.at[s]).wait()

    @pl.when(blk < n_live)
    def _():
        @pl.when((blk == 0) & (c == 0))
        def _():
            def body(r, _):
                row_copy(0, r, 0).start()
                return 0
            lax.fori_loop(0, MOE_BLOCK, body, 0)

        @pl.when(c == 0)
        def _():
            wait_block(slot)

        nxt = jnp.minimum(blk + 1, pl.num_programs(0) - 1)
        for r in range(ROWS_PER_STEP):
            row_copy(nxt, c * ROWS_PER_STEP + r, 1 - slot).start()

        xb = xs_ref[slot].astype(BF16)
        h = (_silu(_dot(xb, wa_ref[...])) * _dot(xb, wu_ref[...])).astype(BF16)
        part = _dot(h, wo_ref[...])

        @pl.when(c == 0)
        def _():
            acc_ref[...] = part

        @pl.when(c > 0)
        def _():
            acc_ref[...] += part

        @pl.when(c == last_step)
        def _():
            ys_ref[...] = acc_ref[...]

        @pl.when((c == last_step) & (blk == n_live - 1))
        def _():
            wait_block(1 - slot)

    @pl.when((blk >= n_live) & (c == last_step))
    def _():
        ys_ref[...] = jnp.zeros_like(ys_ref)


def _moe_ffn(x1, row_tok, blk_expert, n_active, w_in, w_out):
    n_rows = row_tok.shape[0]
    n_blk = n_rows // MOE_BLOCK
    n_ch = EF_STEPS
    live = lambda b, na: jnp.minimum(b, na[0] - 1)
    chunk = lambda b, c, na: jnp.where(b < na[0], c, n_ch - 1)
    return pl.pallas_call(
        _moe_ffn_kernel,
        grid_spec=pltpu.PrefetchScalarGridSpec(
            num_scalar_prefetch=3,
            grid=(n_blk, n_ch),
            in_specs=[
                pl.BlockSpec(memory_space=pl.ANY),
                pl.BlockSpec((None, D_MODEL, EF_CHUNK),
                             lambda b, c, be, na, rt: (be[live(b, na)], 0, chunk(b, c, na))),
                pl.BlockSpec((None, D_MODEL, EF_CHUNK),
                             lambda b, c, be, na, rt: (be[live(b, na)], 0, n_ch + chunk(b, c, na))),
                pl.BlockSpec((None, EF_CHUNK, D_MODEL),
                             lambda b, c, be, na, rt: (be[live(b, na)], chunk(b, c, na), 0)),
            ],
            out_specs=pl.BlockSpec((MOE_BLOCK, D_MODEL), lambda b, c, be, na, rt: (b, 0)),
            scratch_shapes=[pltpu.VMEM((2, MOE_BLOCK, D_MODEL), F32), pltpu.VMEM((MOE_BLOCK, D_MODEL), F32),
                            pltpu.SemaphoreType.DMA((2,))],
        ),
        out_shape=jax.ShapeDtypeStruct((n_rows, D_MODEL), F32),
        compiler_params=_cparams(("arbitrary", "arbitrary")),
        name="moe_expert_swiglu",
    )(blk_expert, n_active, row_tok, x1, w_in, w_in, w_out)


TC = 256


def _moe_combine_kernel(d1_ref, d2_ref, ys_hbm, x_ref, xb_ref, p_ref, info_ref, wpg_ref, wpp_ref, g_ref,
                        b_ref, o_ref, ya_ref, yb_ref, sem):
    i = pl.program_id(0)
    last = pl.num_programs(0) - 1
    slot = i % 2

    def copies(tile, r, s):
        t = tile * TC + r
        return (pltpu.make_async_copy(ys_hbm.at[pl.ds(d1_ref[t], 1), :], ya_ref.at[s, pl.ds(r, 1), :], sem.at[0, s]),
                pltpu.make_async_copy(ys_hbm.at[pl.ds(d2_ref[t], 1), :], yb_ref.at[s, pl.ds(r, 1), :], sem.at[1, s]))

    def wait_tile(s):
        pltpu.make_async_copy(ys_hbm.at[pl.ds(0, TC), :], ya_ref.at[s], sem.at[0, s]).wait()
        pltpu.make_async_copy(ys_hbm.at[pl.ds(0, TC), :], yb_ref.at[s], sem.at[1, s]).wait()

    @pl.when(i == 0)
    def _():
        def body(r, _):
            for cp in copies(0, r, 0):
                cp.start()
            return 0
        lax.fori_loop(0, TC, body, 0)

    wait_tile(slot)
    nxt = jnp.minimum(i + 1, last)
    for r in range(TC):
        for cp in copies(nxt, r, 1 - slot):
            cp.start()
    info = info_ref[...]
    f = info[:, 4:5] * ya_ref[slot] + info[:, 5:6] * yb_ref[slot]
    o_ref[...] = _ple_ln2(x_ref[...], xb_ref[...], f, p_ref, wpg_ref, wpp_ref, g_ref, b_ref)

    @pl.when(i == last)
    def _():
        wait_tile(1 - slot)


def _moe_combine(ys, dest1, dest2, x1, x1b, p2d, info, wts):
    n = x1.shape[0]
    row = lambda w: pl.BlockSpec((TC, w), lambda i, d1, d2: (i, 0))
    consts = [wts['ple_w_gate'], wts['ple_w_proj'], wts['ln2_g'], wts['ln2_b']]
    cspec = lambda c: pl.BlockSpec(c.shape, lambda i, d1, d2: (0,) * c.ndim, pipeline_mode=pl.Buffered(1))
    return pl.pallas_call(
        _moe_combine_kernel,
        grid_spec=pltpu.PrefetchScalarGridSpec(
            num_scalar_prefetch=2,
            grid=(n // TC,),
            in_specs=[pl.BlockSpec(memory_space=pl.ANY), row(D_MODEL), row(D_MODEL), row(P_DIM), row(LANES)]
            + [cspec(c) for c in consts],
            out_specs=row(D_MODEL),
            scratch_shapes=[pltpu.VMEM((2, TC, D_MODEL), F32), pltpu.VMEM((2, TC, D_MODEL), F32),
                            pltpu.SemaphoreType.DMA((2, 2))],
        ),
        out_shape=jax.ShapeDtypeStruct((n, D_MODEL), F32),
        compiler_params=_cparams(("arbitrary",)),
        name="moe_combine_ple_ln2",
    )(dest1, dest2, ys, x1, x1b, p2d, info, *consts)


def _moe(x1, x1b, p2d, wts):
    n = x1.shape[0]
    n_rows = ((n * 2 + MOE_BLOCK - 1) // MOE_BLOCK) * MOE_BLOCK + N_EXPERTS * MOE_BLOCK
    info, cnt = _moe_route(x1, wts['w_router'], wts['tri_tokens'])
    counts = cnt[0, :N_EXPERTS].astype(jnp.int32)
    padded = ((counts + MOE_BLOCK - 1) // MOE_BLOCK) * MOE_BLOCK
    ends = jnp.cumsum(padded)
    start_pad = ends - padded
    e1, e2 = info[:, 0].astype(jnp.int32), info[:, 1].astype(jnp.int32)
    dest1 = start_pad[e1] + info[:, 2].astype(jnp.int32)
    dest2 = start_pad[e2] + info[:, 3].astype(jnp.int32)
    n_blk = n_rows // MOE_BLOCK
    blk_row0 = jnp.arange(n_blk, dtype=jnp.int32) * MOE_BLOCK
    blk_expert = jnp.minimum(jnp.sum((ends[None, :] <= blk_row0[:, None]).astype(jnp.int32), axis=1),
                             N_EXPERTS - 1)
    n_active = (ends[-1:] // MOE_BLOCK).astype(jnp.int32)
    row_tok = _moe_rowmap(dest1, dest2, n_rows)
    ys = _moe_ffn(x1, row_tok, blk_expert, n_active, wts['moe_w_in'], wts['moe_w_out'])
    return _moe_combine(ys, dest1, dest2, x1, x1b, p2d, info, wts)


def _rot_half_cols(w, heads, dim):
    w3 = w.reshape(w.shape[0], heads, dim)
    half = dim // 2
    return jnp.concatenate([-w3[..., half:], w3[..., :half]], axis=-1).reshape(w.shape[0], heads * dim)


def _prep_layer(i, w_in, conv_w, conv_b, conv_ln_g, conv_ln_b, nsa_cmp_pe, nsa_cmp_w1, nsa_cmp_w2,
                mla_q_norm, mla_kv_norm, mla_w_uq, mla_w_ukv, w_branch, w_out, ln1_g, ln1_b,
                ple_w_gate, ple_w_proj, ln2_g, ln2_b):
    w = w_in[i]
    d = w.shape[0]
    z = lambda n: jnp.zeros((d, n), F32)
    dup = lambda a: jnp.concatenate([a, a], axis=1)
    c_glu, nq = w[:, 0:512], w[:, 512:768]
    nkv = w[:, 768:1152]
    k_cmp, v_cmp, k_slc, v_slc, k_win, v_win = [nkv[:, j * 64:(j + 1) * 64] for j in range(6)]
    ng = w[:, 1152:1164]
    mq, mkv, mkr = w[:, 1164:1420], w[:, 1420:1548], w[:, 1548:1580]
    sb = w[:, 1580:2348]
    bg = w[:, 2348:6444]
    cols = [c_glu, nq, _rot_half_cols(nq, NSA_HEADS, HEAD_DIM),
            dup(k_slc), dup(_rot_half_cols(k_slc, 1, HEAD_DIM)),
            dup(k_win), dup(_rot_half_cols(k_win, 1, HEAD_DIM)),
            dup(v_slc), dup(v_win), k_cmp, v_cmp, ng, z(LANES - 12), mq, mkv,
            z(64), mkr, z(32), z(64), _rot_half_cols(mkr, 1, MLA_ROPE), z(32), z(LANES), sb]
    w1 = jnp.concatenate(cols, axis=1).astype(BF16)
    assert w1.shape[1] == C_TOT

    inv32 = ROPE_THETA ** (-jnp.arange(HEAD_DIM // 2, dtype=F32) / (HEAD_DIM // 2))
    inv16 = ROPE_THETA ** (-jnp.arange(MLA_ROPE // 2, dtype=F32) / (MLA_ROPE // 2))
    inv_nsa = jnp.tile(inv32, 4)[None, :]
    inv_mla = jnp.concatenate([jnp.zeros((64,), F32), inv16, inv16, jnp.zeros((32,), F32)])[None, :]

    wuq = mla_w_uq[i].reshape(MLA_Q_RANK, MLA_HEADS, MLA_NOPE + MLA_ROPE)
    zq = jnp.zeros((MLA_Q_RANK, MLA_HEADS, 32), F32)
    wq = jnp.concatenate([wuq, zq], axis=-1).reshape(MLA_Q_RANK, MLA_HEADS * LANES)
    rope_rot = jnp.concatenate([-wuq[..., MLA_NOPE + 16:], wuq[..., MLA_NOPE:MLA_NOPE + 16]], axis=-1)
    wqr = jnp.concatenate([jnp.zeros((MLA_Q_RANK, MLA_HEADS, MLA_NOPE), F32), rope_rot, zq],
                          axis=-1).reshape(MLA_Q_RANK, MLA_HEADS * LANES)
    wukv = mla_w_ukv[i].reshape(MLA_KV_RANK, MLA_HEADS, MLA_NOPE + MLA_V)
    wk = jnp.concatenate([wukv[..., :MLA_NOPE], jnp.zeros((MLA_KV_RANK, MLA_HEADS, 64), F32)],
                         axis=-1).reshape(MLA_KV_RANK, MLA_HEADS * LANES)
    wv = wukv[..., MLA_NOPE:].reshape(MLA_KV_RANK, MLA_HEADS * MLA_V)

    pe = nsa_cmp_pe[i]
    pe_rows = pe.reshape(CMP_BLOCK, 2 * HEAD_DIM)
    pe_a = pe_rows[:CMP_STRIDE].reshape(1, CMP_STRIDE * LANES)
    pe_b = pe_rows[CMP_STRIDE:].reshape(1, CMP_STRIDE * LANES)
    w1c = nsa_cmp_w1[i].reshape(2, CMP_BLOCK, HEAD_DIM, HEAD_DIM)
    zblk = jnp.zeros((CMP_BLOCK, HEAD_DIM, HEAD_DIM), F32)
    w1full = jnp.concatenate([jnp.concatenate([w1c[0], zblk], axis=2),
                              jnp.concatenate([zblk, w1c[1]], axis=2)], axis=1)
    w1a = w1full[:CMP_STRIDE].reshape(CMP_STRIDE * LANES, LANES).astype(BF16)
    w1b = w1full[CMP_STRIDE:].reshape(CMP_STRIDE * LANES, LANES).astype(BF16)
    w2 = nsa_cmp_w2[i]
    z64 = jnp.zeros((HEAD_DIM, LANES), F32)
    w2k = jnp.concatenate([dup(w2[0]), z64], axis=0).astype(BF16)
    w2v = jnp.concatenate([z64, dup(w2[1])], axis=0).astype(BF16)

    return dict(
        w1=w1, inv_nsa=inv_nsa, inv_mla=inv_mla,
        mla_qn=mla_q_norm[i][None, :], mla_kvn=mla_kv_norm[i][None, :],
        wq=wq.astype(BF16), wqr=wqr.astype(BF16), wk=wk.astype(BF16), wv=wv.astype(BF16),
        conv_w=conv_w[i], conv_b=conv_b[i][None, :], conv_g=conv_ln_g[i][None, :], conv_beta=conv_ln_b[i][None, :],
        pe_a=pe_a, pe_b=pe_b, w1a=w1a, w1b=w1b, w2k=w2k, w2v=w2v,
        w_gate=bg.astype(BF16), w_branch=w_branch[i].astype(BF16), w_out=w_out[i].astype(BF16),
        ln1_g=ln1_g[i][None, :], ln1_b=ln1_b[i][None, :],
        ple_w_gate=ple_w_gate[i].astype(BF16), ple_w_proj=ple_w_proj[i].astype(BF16),
        ln2_g=ln2_g[i][None, :], ln2_b=ln2_b[i][None, :],
    )


def _tables(seq):
    n_cmp_rows = seq // CMP_STRIDE
    n_sel = seq // SEL_BLOCK
    cmp_start = jnp.arange(n_cmp_rows) * CMP_STRIDE
    sel_start = jnp.arange(LANES) * SEL_BLOCK
    n_cmp = (seq - CMP_BLOCK) // CMP_STRIDE + 1
    overlap = ((cmp_start[:, None] < sel_start[None, :] + SEL_BLOCK)
               & (cmp_start[:, None] + CMP_BLOCK > sel_start[None, :])
               & (jnp.arange(n_cmp_rows)[:, None] < n_cmp) & (jnp.arange(LANES)[None, :] < n_sel))
    kb = jnp.arange(seq // TK)[:, None, None]
    nn = jnp.arange(LANES)[None, :, None]
    ll = jnp.arange(TK)[None, None, :]
    expand = (kb * TK + ll) // SEL_BLOCK == nn
    jj = jnp.arange(TK)
    tri_keys = jnp.tile(jj[:, None] >= jj[None, :], (2, 1))
    tt = jnp.arange(TR)
    tri_tokens = tt[None, :] < tt[:, None]
    return dict(overlap_t=overlap.T.astype(BF16), eye_q=jnp.eye(TQ, dtype=BF16),
                expand=expand.astype(BF16), tri_keys=tri_keys.astype(BF16),
                tri_tokens=tri_tokens.astype(BF16))


def kernel(x, p, positions, w_in, conv_w, conv_b, conv_ln_g, conv_ln_b, nsa_cmp_pe, nsa_cmp_w1, nsa_cmp_w2,
           mla_q_norm, mla_kv_norm, mla_w_uq, mla_w_ukv, w_branch, w_out, ln1_g, ln1_b, ffn_w_in, ffn_w_out,
           moe_router, moe_w_in, moe_w_out, ple_w_gate, ple_w_proj, ln2_g, ln2_b):
    batch, seq, _ = x.shape
    n = batch * seq
    tabs = _tables(seq)
    x2d = x.reshape(n, D_MODEL)
    pos2d = positions.reshape(n, 1)
    for i in range(DEPTH):
        wts = _prep_layer(i, w_in, conv_w, conv_b, conv_ln_g, conv_ln_b, nsa_cmp_pe, nsa_cmp_w1, nsa_cmp_w2,
                          mla_q_norm, mla_kv_norm, mla_w_uq, mla_w_ukv, w_branch, w_out, ln1_g, ln1_b,
                          ple_w_gate, ple_w_proj, ln2_g, ln2_b)
        wts['overlap_t'], wts['eye_q'] = tabs['overlap_t'], tabs['eye_q']
        (conv_in, nq, nqr, ks, kw, vs, vw, kvc, ng, mq, mk, mv, sq, sk, sv) = _mixer_in(x2d, pos2d, wts)
        y_a = _conv(conv_in, wts['conv_w'], wts['conv_b'], wts['conv_g'], wts['conv_beta'], batch, seq)
        ocmp, sel = _nsa_cmp(kvc, nq, wts, batch, seq)
        y_b = _nsa_attn(nqr, ks, vs, kw, vw, sel, tabs['expand'], ocmp, ng, batch, seq)
        y_c = _mla_attn(mq, mk, mv, batch, seq)
        y_d = _sb_attn(sq, sk, sv, tabs['tri_keys'], batch, seq)
        ys = [y.reshape(n, BRANCH_W) for y in (y_a, y_b, y_c, y_d)]
        x1, x1b = _merge(x2d, ys, wts)
        p2d = p[i].reshape(n, P_DIM)
        if i % 2 == 0:
            wts['ffn_w_in'] = ffn_w_in[i // 2].astype(BF16)
            wts['ffn_w_out'] = ffn_w_out[i // 2].astype(BF16)
            x2d = _ffn_dense(x1, x1b, p2d, wts)
        else:
            wts['w_router'] = jnp.concatenate(
                [moe_router[i // 2], jnp.zeros((D_MODEL, LANES - N_EXPERTS), F32)], axis=1)
            wts['tri_tokens'] = tabs['tri_tokens']
            wts['moe_w_in'] = moe_w_in[i // 2].astype(BF16)
            wts['moe_w_out'] = moe_w_out[i // 2].astype(BF16)
            x2d = _moe(x1, x1b, p2d, wts)
    return x2d.reshape(batch, seq, D_MODEL)
```

```python
import functools

import jax
import jax.numpy as jnp
from jax import lax
from jax.experimental import pallas as pl
from jax.experimental.pallas import tpu as pltpu

F32 = jnp.float32
BF16 = jnp.bfloat16

D_MODEL = 1024
DEPTH = 2
CONV_CH = 256
CONV_WIDTH = 31
NSA_HEADS = 4
HEAD_DIM = 64
CMP_BLOCK = 32
CMP_STRIDE = 16
SEL_BLOCK = 64
SEL_TOPN = 16
WINDOW = 512
MLA_HEADS = 4
MLA_Q_RANK = 256
MLA_KV_RANK = 128
MLA_NOPE = 64
MLA_ROPE = 32
MLA_V = 64
N_BRANCH = 4
BRANCH_W = 256
ROPE_THETA = 10000.0
LN_EPS = 1e-5
RMS_EPS = 1e-6
D_FF = 2816
N_EXPERTS = 8
D_FF_EXPERT = 3584
MOE_BLOCK = 512
P_DIM = 256
DEEPNORM_ALPHA = (2 * DEPTH) ** 0.25

LANES = 128
VMEM_LIMIT = 56 * 1024 * 1024

NEG = -1e30

C_CONV = 0
C_NQ = 512
C_NQR = 768
C_KS = 1024
C_KW = 1280
C_VS = 1536
C_KVC = 1792
C_MQ = 2048
C_MKV = 2304
C_MKRR = 2560
C_SB = 2816
C_TOT = 3584

TM = 256
TQ = 256
TK = 256


def _cparams(sem, vmem=VMEM_LIMIT):
    return pltpu.CompilerParams(dimension_semantics=sem, vmem_limit_bytes=vmem)


def _const_spec(shape):
    nd = len(shape)
    return pl.BlockSpec(shape, lambda *_: (0,) * nd, pipeline_mode=pl.Buffered(1))


def _dot(a, b):
    return jnp.dot(a, b, preferred_element_type=F32)


def _dot_nt(a, b):
    return lax.dot_general(a, b, (((1,), (1,)), ((), ())), preferred_element_type=F32)


def _layer_norm(h, g, b):
    mu = jnp.mean(h, axis=-1, keepdims=True)
    d = h - mu
    var = jnp.mean(d * d, axis=-1, keepdims=True)
    return d * lax.rsqrt(var + LN_EPS) * g + b


def _rms_norm(h, g):
    return h * lax.rsqrt(jnp.mean(h * h, axis=-1, keepdims=True) + RMS_EPS) * g


def _sigmoid(x):
    return 1.0 / (1.0 + jnp.exp(-x))


def _silu(x):
    return x * _sigmoid(x)


def _split_bf16(x):
    hi = x.astype(BF16)
    lo = (x - hi.astype(F32)).astype(BF16)
    return hi, lo


def _half_select(sub, x):
    lane = lax.broadcasted_iota(jnp.int32, x.shape, 1)
    keep = (lane < HEAD_DIM) if sub == 0 else (lane >= HEAD_DIM)
    return jnp.where(keep, x, 0.0)


def _mixer_in_kernel(x_ref, pos_ref, w_ref, invn_ref, invm_ref, qn_ref, kvn_ref, wq_ref, wqr_ref,
                     wk_ref, wv_ref,
                     conv_ref, nq_ref, nqr_ref, ks_ref, kw_ref, vs_ref, vw_ref, kvc_ref, ng_ref,
                     mq_ref, mk_ref, mv_ref, sq_ref, sk_ref, sv_ref):
    xb = x_ref[...].astype(BF16)

    def proj(c0, width):
        return _dot(xb, w_ref[:, c0:c0 + width])

    posf = pos_ref[...].astype(F32)
    ang_n = posf * invn_ref[...]
    cos_n, sin_n = jnp.cos(ang_n), jnp.sin(ang_n)
    ang_m = posf * invm_ref[...]
    cos_m, sin_m = jnp.cos(ang_m), jnp.sin(ang_m)

    lo, hi = slice(0, LANES), slice(LANES, 2 * LANES)
    qn = _rms_norm(proj(C_MQ, MLA_Q_RANK), qn_ref[...]).astype(BF16)
    mkv = proj(C_MKV, 256)
    kvn = _rms_norm(mkv[:, lo], kvn_ref[...]).astype(BF16)

    conv_ref[...] = proj(C_CONV, 512)

    scale = HEAD_DIM ** -0.5
    cos_n2 = jnp.concatenate([cos_n, cos_n], axis=1)
    sin_n2 = jnp.concatenate([sin_n, sin_n], axis=1)
    q, qrot = proj(C_NQ, 256), proj(C_NQR, 256)
    nq_ref[...] = (q * scale).astype(BF16)
    nqr_ref[...] = ((q * cos_n2 + qrot * sin_n2) * scale).astype(BF16)
    k = proj(C_KS, 256)
    ks_ref[...] = (k[:, lo] * cos_n + k[:, hi] * sin_n).astype(BF16)
    k = proj(C_KW, 256)
    kw_ref[...] = (k[:, lo] * cos_n + k[:, hi] * sin_n).astype(BF16)
    v = proj(C_VS, 256)
    vs_ref[...] = v[:, lo].astype(BF16)
    vw_ref[...] = v[:, hi].astype(BF16)
    u = proj(C_KVC, 256)
    kvc_ref[...] = u[:, lo]
    ng_ref[...] = u[:, hi]

    kr = mkv[:, hi] * cos_m + proj(C_MKRR, 256)[:, lo] * sin_m
    qa, qr, kk = _dot(qn, wq_ref[...]), _dot(qn, wqr_ref[...]), _dot(kvn, wk_ref[...])
    for h in range(MLA_HEADS):
        sl = slice(h * LANES, (h + 1) * LANES)
        mq_ref[:, sl] = (qa[:, sl] * cos_m + qr[:, sl] * sin_m).astype(BF16)
        mk_ref[:, sl] = (kk[:, sl] + kr).astype(BF16)
    mv_ref[...] = _dot(kvn, wv_ref[...]).astype(BF16)

    sq_ref[...] = (proj(C_SB, 256) * scale).astype(BF16)
    sk_ref[...] = proj(C_SB + 256, 256).astype(BF16)
    sv_ref[...] = proj(C_SB + 512, 256).astype(BF16)


def _mixer_in(x2d, pos2d, wts):
    n = x2d.shape[0]
    row = lambda w: pl.BlockSpec((TM, w), lambda i: (i, 0))
    out_widths = [512, 256, 256, 128, 128, 128, 128, 128, 128, 512, 512, 256, 256, 256, 256]
    out_dtypes = [F32, BF16, BF16, BF16, BF16, BF16, BF16, F32, F32, BF16, BF16, BF16, BF16, BF16, BF16]
    consts = [wts['w1'], wts['inv_nsa'], wts['inv_mla'], wts['mla_qn'], wts['mla_kvn'], wts['wq'],
              wts['wqr'], wts['wk'], wts['wv']]
    return pl.pallas_call(
        _mixer_in_kernel,
        grid=(n // TM,),
        in_specs=[row(D_MODEL), row(1)] + [_const_spec(c.shape) for c in consts],
        out_specs=[row(w) for w in out_widths],
        out_shape=[jax.ShapeDtypeStruct((n, w), d) for w, d in zip(out_widths, out_dtypes)],
        compiler_params=_cparams(("parallel",)),
        name="mixer_in",
    )(x2d, pos2d, *consts)


CONV_PAD = 32
CONV_CHUNK = 128


def _conv_kernel(u_ref, w_ref, b_ref, g_ref, beta_ref, o_ref, hp_ref):
    seq = u_ref.shape[0]
    hp_ref[0:CONV_PAD, :] = jnp.zeros((CONV_PAD, CONV_CH), F32)
    hp_ref[CONV_PAD:CONV_PAD + seq, :] = u_ref[:, 0:CONV_CH] * _sigmoid(u_ref[:, CONV_CH:2 * CONV_CH])
    shift = CONV_PAD - (CONV_WIDTH - 1)
    for c in range(seq // CONV_CHUNK):
        base = c * CONV_CHUNK
        acc = jnp.broadcast_to(b_ref[...], (CONV_CHUNK, CONV_CH))
        for j in range(CONV_WIDTH):
            acc = acc + hp_ref[base + shift + j:base + shift + j + CONV_CHUNK, :] * w_ref[j:j + 1, :]
        y = _layer_norm(acc, g_ref[...], beta_ref[...])
        o_ref[base:base + CONV_CHUNK, :] = _silu(y).astype(BF16)


def _conv(conv_in, w, b, g, beta, batch, seq):
    return pl.pallas_call(
        _conv_kernel,
        grid=(batch,),
        in_specs=[pl.BlockSpec((None, seq, 2 * CONV_CH), lambda i: (i, 0, 0)),
                  _const_spec(w.shape), _const_spec(b.shape), _const_spec(g.shape), _const_spec(beta.shape)],
        out_specs=pl.BlockSpec((None, seq, CONV_CH), lambda i: (i, 0, 0)),
        out_shape=jax.ShapeDtypeStruct((batch, seq, CONV_CH), BF16),
        scratch_shapes=[pltpu.VMEM((CONV_PAD + seq, CONV_CH), F32)],
        compiler_params=_cparams(("parallel",)),
        name="conformer_conv",
    )(conv_in.reshape(batch, seq, 2 * CONV_CH), w, b, g, beta)


def _gelu_tanh(x):
    return 0.5 * x * (1.0 + jnp.tanh(0.7978845608028654 * (x + 0.044715 * x * x * x)))


SEL_SHIFT = 6


def _nsa_cmp_kernel(kvc_ref, q_ref, pea_ref, peb_ref, w1a_ref, w1b_ref, w2k_ref, w2v_ref, ovt_ref, eye_ref,
                    ocmp_ref, sel_ref, *, seq):
    n_cmp = (seq - CMP_BLOCK) // CMP_STRIDE + 1
    n_sel = seq // SEL_BLOCK
    nb = seq // CMP_STRIDE
    x2 = kvc_ref[...]
    xa = (x2 + pea_ref[...]).astype(BF16)
    xb = (x2 + peb_ref[...]).astype(BF16)
    ha = _dot(xa, w1a_ref[...])
    hb = _dot(xb, w1b_ref[...])
    hid = ha + pltpu.roll(hb, nb - 1, 0)
    hid = _gelu_tanh(hid).astype(BF16)
    kk = _dot(hid, w2k_ref[...]).astype(BF16)
    vv = _dot(hid, w2v_ref[...]).astype(BF16)
    ovt = ovt_ref[...]

    for c in range(seq // TQ):
        r0 = c * TQ
        t = r0 + lax.broadcasted_iota(jnp.int32, (TQ, LANES), 0)
        j = lax.broadcasted_iota(jnp.int32, (TQ, LANES), 1)
        valid = (j * CMP_STRIDE + CMP_BLOCK - 1 <= t) & (j < n_cmp)
        psum = jnp.zeros((TQ, LANES), F32)
        for pair in range(2):
            qp = q_ref[r0:r0 + TQ, pair * LANES:(pair + 1) * LANES].astype(F32)
            outs = []
            for sub in range(2):
                qm = _half_select(sub, qp).astype(BF16)
                s = jnp.where(valid, _dot_nt(qm, kk), NEG)
                m = jnp.max(s, axis=-1, keepdims=True)
                e = jnp.where(valid, jnp.exp(s - m), 0.0)
                den = jnp.sum(e, axis=-1, keepdims=True)
                p = e / jnp.where(den > 0, den, 1.0)
                psum = psum + p
                outs.append(_dot(p.astype(BF16), vv))
            ocmp_ref[r0:r0 + TQ, pair * LANES:(pair + 1) * LANES] = jnp.where(
                lax.broadcasted_iota(jnp.int32, (TQ, LANES), 1) < HEAD_DIM, outs[0], outs[1])
        p_hi, p_lo = _split_bf16(psum)
        imp = (_dot_nt(ovt, p_hi) + _dot_nt(ovt, p_lo))[0:n_sel, :]
        n = lax.broadcasted_iota(jnp.int32, (n_sel, TQ), 0)
        cur = jnp.right_shift(r0 + lax.broadcasted_iota(jnp.int32, (n_sel, TQ), 1), SEL_SHIFT)
        forced = (n == 0) | (n == cur) | (n == cur - 1)
        imp = jnp.where(forced, jnp.inf, imp)
        imp = jnp.where(n > cur, -jnp.inf, imp)
        rank = jnp.zeros((n_sel, TQ), F32)
        for n2 in range(n_sel):
            other = imp[n2:n2 + 1, :]
            ahead = (other > imp) | ((other == imp) & (n2 < n))
            rank = rank + jnp.where(ahead, 1.0, 0.0)
        sel_t = jnp.where((rank < SEL_TOPN) & (imp > -jnp.inf), 1.0, 0.0)
        sel_t = jnp.concatenate([sel_t, jnp.zeros((LANES - n_sel, TQ), F32)], axis=0).astype(BF16)
        sel_ref[r0:r0 + TQ, :] = _dot_nt(eye_ref[...], sel_t).astype(BF16)


def _nsa_cmp(kvc, nq, wts, batch, seq):
    nb = seq // CMP_STRIDE
    consts = [wts['pe_a'], wts['pe_b'], wts['w1a'], wts['w1b'], wts['w2k'], wts['w2v'], wts['overlap_t'],
              wts['eye_q']]
    return pl.pallas_call(
        functools.partial(_nsa_cmp_kernel, seq=seq),
        grid=(batch,),
        in_specs=[pl.BlockSpec((None, nb, CMP_STRIDE * LANES), lambda i: (i, 0, 0)),
                  pl.BlockSpec((None, seq, 256), lambda i: (i, 0, 0))] + [_const_spec(c.shape) for c in consts],
        out_specs=[pl.BlockSpec((None, seq, 256), lambda i: (i, 0, 0)),
                   pl.BlockSpec((None, seq, LANES), lambda i: (i, 0, 0))],
        out_shape=[jax.ShapeDtypeStruct((batch, seq, 256), F32),
                   jax.ShapeDtypeStruct((batch, seq, LANES), BF16)],
        compiler_params=_cparams(("parallel",)),
        name="nsa_compress_select",
    )(kvc.reshape(batch, nb, CMP_STRIDE * LANES), nq.reshape(batch, seq, 256), *consts)


LOG2E = 1.4426950408889634
N_HEADS = 4


def _softmax_scratch(n_tiles):
    slab = pltpu.VMEM((N_HEADS, TQ, LANES), F32)
    return [pltpu.VMEM((N_HEADS, n_tiles, TQ, TK), F32), slab, slab, slab, slab]


def _scores_put(h, t, s, s_ref, mx_ref):
    s_ref[h, t] = s
    mx_ref[h] = jnp.maximum(mx_ref[h], jnp.maximum(s[:, :LANES], s[:, LANES:]))


def _row_max(mx_ref, mb_ref):
    for h in range(N_HEADS):
        mb_ref[h] = jnp.broadcast_to(jnp.max(mx_ref[h], axis=-1, keepdims=True), (TQ, LANES))


def _probs_accumulate(h, t, c, v_blk, s_ref, mb_ref, ls_ref, acc_ref):
    s, mb = s_ref[h, t], mb_ref[h]
    pa = jnp.exp2((s[:, :LANES] - mb) * c)
    pb = jnp.exp2((s[:, LANES:] - mb) * c)
    ls_ref[h] += pa + pb
    acc_ref[h] += _dot(jnp.concatenate([pa, pb], axis=1).astype(BF16), v_blk)


def _softmax_out(h, ls_ref, acc_ref):
    return acc_ref[h] / jnp.sum(ls_ref[h], axis=-1, keepdims=True)


def _tile_iotas():
    return (lax.broadcasted_iota(jnp.int32, (TQ, TK), 0), lax.broadcasted_iota(jnp.int32, (TQ, TK), 1))


def _nsa_attn_kernel(q_ref, ks_ref, vs_ref, kw_ref, vw_ref, sel_ref, exp_ref, ocmp_ref, ng_ref, o_ref,
                     qm_ref, ss_ref, mxs_ref, mbs_ref, lss_ref, accs_ref,
                     sw_ref, mxw_ref, mbw_ref, lsw_ref, accw_ref):
    i = pl.program_id(1)
    row, col = _tile_iotas()
    sel = sel_ref[...]
    for ref in (mxs_ref, mxw_ref):
        ref[...] = jnp.full(ref.shape, NEG, F32)
    for ref in (lss_ref, accs_ref, lsw_ref, accw_ref):
        ref[...] = jnp.zeros(ref.shape, F32)
    for pair in range(2):
        qp = q_ref[:, pair * LANES:(pair + 1) * LANES].astype(F32)
        for sub in range(2):
            qm_ref[2 * pair + sub] = _half_select(sub, qp).astype(BF16)

    def selected_scores(kb, causal):
        k0 = pl.multiple_of(kb * TK, TK)
        hit = _dot(sel, exp_ref[kb]) > 0.5
        if causal:
            hit = hit & (col <= row)
        bias = jnp.where(hit, 0.0, NEG)
        k_blk = ks_ref[pl.ds(k0, TK), :]
        for h in range(N_HEADS):
            _scores_put(h, kb, _dot_nt(qm_ref[h], k_blk) + bias, ss_ref, mxs_ref)

    def window_scores(slot, mask):
        k0 = pl.multiple_of((i - 2 + slot) * TK, TK)
        k_blk = kw_ref[pl.ds(k0, TK), :]
        for h in range(N_HEADS):
            s = _dot_nt(qm_ref[h], k_blk)
            if mask is not None:
                s = jnp.where(mask, s, NEG)
            _scores_put(h, slot, s, sw_ref, mxw_ref)

    def off_diagonal(kb, _):
        selected_scores(kb, False)
        return 0

    lax.fori_loop(0, i, off_diagonal, 0)
    selected_scores(i, True)
    pl.when(i >= 2)(lambda: window_scores(0, col > row))
    pl.when(i >= 1)(lambda: window_scores(1, None))
    window_scores(2, col <= row)
    _row_max(mxs_ref, mbs_ref)
    _row_max(mxw_ref, mbw_ref)

    def selected_probs(kb, _):
        v_blk = vs_ref[pl.ds(pl.multiple_of(kb * TK, TK), TK), :]
        for h in range(N_HEADS):
            _probs_accumulate(h, kb, LOG2E, v_blk, ss_ref, mbs_ref, lss_ref, accs_ref)
        return 0

    def window_probs(slot):
        v_blk = vw_ref[pl.ds(pl.multiple_of((i - 2 + slot) * TK, TK), TK), :]
        for h in range(N_HEADS):
            _probs_accumulate(h, slot, LOG2E, v_blk, sw_ref, mbw_ref, lsw_ref, accw_ref)

    lax.fori_loop(0, i + 1, selected_probs, 0)
    pl.when(i >= 2)(lambda: window_probs(0))
    pl.when(i >= 1)(lambda: window_probs(1))
    window_probs(2)

    g = _sigmoid(ng_ref[...])
    lane = lax.broadcasted_iota(jnp.int32, (TQ, LANES), 1)
    for pair in range(2):
        res = []
        for sub in range(2):
            h = 2 * pair + sub
            res.append(g[:, 3 * h + 1:3 * h + 2] * _softmax_out(h, lss_ref, accs_ref)
                       + g[:, 3 * h + 2:3 * h + 3] * _softmax_out(h, lsw_ref, accw_ref)
                       + g[:, 3 * h:3 * h + 1] * ocmp_ref[:, pair * LANES:(pair + 1) * LANES])
        o_ref[:, pair * LANES:(pair + 1) * LANES] = jnp.where(lane < HEAD_DIM, res[0], res[1]).astype(BF16)


def _nsa_attn(nqr, ks, vs, kw, vw, sel, expand, ocmp, ng, batch, seq):
    qspec = lambda w: pl.BlockSpec((None, TQ, w), lambda b, i: (b, i, 0))
    kspec = pl.BlockSpec((None, seq, LANES), lambda b, i: (b, 0, 0))
    r3 = lambda a: a.reshape(batch, seq, a.shape[-1])
    return pl.pallas_call(
        _nsa_attn_kernel,
        grid=(batch, seq // TQ),
        in_specs=[qspec(256), kspec, kspec, kspec, kspec, qspec(LANES), _const_spec(expand.shape),
                  qspec(256), qspec(LANES)],
        out_specs=qspec(256),
        out_shape=jax.ShapeDtypeStruct((batch, seq, 256), BF16),
        scratch_shapes=[pltpu.VMEM((N_HEADS, TQ, LANES), BF16)] + _softmax_scratch(seq // TK)
        + _softmax_scratch(WINDOW // TK + 1),
        compiler_params=_cparams(("parallel", "parallel")),
        name="nsa_select_window",
    )(r3(nqr), r3(ks), r3(vs), r3(kw), r3(vw), sel, expand, ocmp, r3(ng))


def _mla_attn_kernel(q_ref, k_ref, v_ref, o_ref, s_ref, mx_ref, mb_ref, ls_ref, acc_ref):
    i = pl.program_id(1)
    c = (MLA_NOPE + MLA_ROPE) ** -0.5 * LOG2E
    row, col = _tile_iotas()
    mx_ref[...] = jnp.full(mx_ref.shape, NEG, F32)
    ls_ref[...] = jnp.zeros(ls_ref.shape, F32)
    acc_ref[...] = jnp.zeros(acc_ref.shape, F32)

    def scores(kb, causal):
        k0 = pl.multiple_of(kb * TK, TK)
        for h in range(N_HEADS):
            hs = slice(h * LANES, (h + 1) * LANES)
            s = _dot_nt(q_ref[:, hs], k_ref[pl.ds(k0, TK), hs])
            if causal:
                s = jnp.where(col <= row, s, NEG)
            _scores_put(h, kb, s, s_ref, mx_ref)

    def off_diagonal(kb, _):
        scores(kb, False)
        return 0

    lax.fori_loop(0, i, off_diagonal, 0)
    scores(i, True)
    _row_max(mx_ref, mb_ref)

    def probs(kb, _):
        k0 = pl.multiple_of(kb * TK, TK)
        for h in range(N_HEADS):
            v_blk = v_ref[pl.ds(k0, TK), (h // 2) * LANES:(h // 2 + 1) * LANES]
            _probs_accumulate(h, kb, c, v_blk, s_ref, mb_ref, ls_ref, acc_ref)
        return 0

    lax.fori_loop(0, i + 1, probs, 0)
    lane = lax.broadcasted_iota(jnp.int32, (TQ, LANES), 1)
    for pair in range(2):
        o_ref[:, pair * LANES:(pair + 1) * LANES] = jnp.where(
            lane < HEAD_DIM, _softmax_out(2 * pair, ls_ref, acc_ref),
            _softmax_out(2 * pair + 1, ls_ref, acc_ref)).astype(BF16)


def _mla_attn(mq, mk, mv, batch, seq):
    r3 = lambda a: a.reshape(batch, seq, a.shape[-1])
    return pl.pallas_call(
        _mla_attn_kernel,
        grid=(batch, seq // TQ),
        in_specs=[pl.BlockSpec((None, TQ, 512), lambda b, i: (b, i, 0)),
                  pl.BlockSpec((None, seq, 512), lambda b, i: (b, 0, 0)),
                  pl.BlockSpec((None, seq, 256), lambda b, i: (b, 0, 0))],
        out_specs=pl.BlockSpec((None, TQ, 256), lambda b, i: (b, i, 0)),
        out_shape=jax.ShapeDtypeStruct((batch, seq, 256), BF16),
        scratch_shapes=_softmax_scratch(seq // TK),
        compiler_params=_cparams(("parallel", "parallel")),
        name="mla_attention",
    )(r3(mq), r3(mk), r3(mv))


def _sb_attn_kernel(q_ref, k_ref, v_ref, tri_ref, o_ref, qm_ref, e_ref, tail_ref, acc_ref):
    i = pl.program_id(1)
    row, col = _tile_iotas()
    tail_ref[...] = jnp.zeros(tail_ref.shape, F32)
    acc_ref[...] = jnp.zeros(acc_ref.shape, F32)
    for pair in range(2):
        qp = q_ref[:, pair * LANES:(pair + 1) * LANES].astype(F32)
        for sub in range(2):
            qm_ref[2 * pair + sub] = _half_select(sub, qp).astype(BF16)

    def log_weights(kb, diagonal):
        k0 = pl.multiple_of(kb * TK, TK)
        tri2 = tri_ref[...]
        heads = range(N_HEADS)
        zs = [_dot_nt(qm_ref[h], k_ref[pl.ds(k0, TK), (h // 2) * LANES:(h // 2 + 1) * LANES]) for h in heads]
        lks = [jnp.minimum(z, 0.0) - jnp.log(1.0 + jnp.exp(-jnp.abs(z))) - z for z in zs]
        if diagonal:
            lks = [jnp.where(col < row, lk, 0.0) for lk in lks]
        incls = [_dot(jnp.concatenate(_split_bf16(lk), axis=1), tri2) for lk in lks]
        for h in heads:
            tail = tail_ref[h]
            e = zs[h] + incls[h] + jnp.concatenate([tail, tail], axis=1)
            if diagonal:
                e = jnp.where(col < row, e, NEG)
            e_ref[h, kb] = e
            tail_ref[h] = tail + jnp.broadcast_to(incls[h][:, 0:1], (TQ, LANES))

    log_weights(i, True)

    def off_diagonal(step, _):
        log_weights(i - 1 - step, False)
        return 0

    lax.fori_loop(0, i, off_diagonal, 0)

    def weighted_values(kb, _):
        k0 = pl.multiple_of(kb * TK, TK)
        for h in range(N_HEADS):
            ps = slice((h // 2) * LANES, (h // 2 + 1) * LANES)
            acc_ref[h] += _dot(jnp.exp(e_ref[h, kb]).astype(BF16), v_ref[pl.ds(k0, TK), ps])
        return 0

    lax.fori_loop(0, i + 1, weighted_values, 0)
    lane = lax.broadcasted_iota(jnp.int32, (TQ, LANES), 1)
    for pair in range(2):
        o_ref[:, pair * LANES:(pair + 1) * LANES] = jnp.where(
            lane < HEAD_DIM, acc_ref[2 * pair], acc_ref[2 * pair + 1]).astype(BF16)


def _sb_attn(sq, sk, sv, tri, batch, seq):
    r3 = lambda a: a.reshape(batch, seq, a.shape[-1])
    kspec = pl.BlockSpec((None, seq, 256), lambda b, i: (b, 0, 0))
    qspec = pl.BlockSpec((None, TQ, 256), lambda b, i: (b, i, 0))
    return pl.pallas_call(
        _sb_attn_kernel,
        grid=(batch, seq // TQ),
        in_specs=[qspec, kspec, kspec, _const_spec(tri.shape)],
        out_specs=qspec,
        out_shape=jax.ShapeDtypeStruct((batch, seq, 256), BF16),
        scratch_shapes=[pltpu.VMEM((N_HEADS, TQ, LANES), BF16), pltpu.VMEM((N_HEADS, seq // TK, TQ, TK), F32),
                        pltpu.VMEM((N_HEADS, TQ, LANES), F32), pltpu.VMEM((N_HEADS, TQ, LANES), F32)],
        compiler_params=_cparams(("parallel", "parallel")),
        name="stick_breaking_attention",
    )(r3(sq), r3(sk), r3(sv), tri)


def _merge_kernel(x_ref, ya_ref, yb_ref, yc_ref, yd_ref, wg_ref, wb_ref, wo_ref, g_ref, b_ref,
                  o_ref, ob_ref):
    x = x_ref[...]
    xb = x.astype(BF16)
    mixed = jnp.zeros((TM, D_MODEL), F32)
    for n, y_ref in enumerate((ya_ref, yb_ref, yc_ref, yd_ref)):
        gate = _sigmoid(_dot(xb, wg_ref[:, n * D_MODEL:(n + 1) * D_MODEL]))
        mixed = mixed + gate * _dot(y_ref[...], wb_ref[n])
    h = DEEPNORM_ALPHA * x + _dot(mixed.astype(BF16), wo_ref[...])
    out = _layer_norm(h, g_ref[...], b_ref[...])
    o_ref[...] = out
    ob_ref[...] = out.astype(BF16)


def _merge(x2d, ys, wts):
    n = x2d.shape[0]
    row = lambda w: pl.BlockSpec((TM, w), lambda i: (i, 0))
    consts = [wts['w_gate'], wts['w_branch'], wts['w_out'], wts['ln1_g'], wts['ln1_b']]
    return pl.pallas_call(
        _merge_kernel,
        grid=(n // TM,),
        in_specs=[row(D_MODEL)] + [row(BRANCH_W)] * 4 + [_const_spec(c.shape) for c in consts],
        out_specs=[row(D_MODEL), row(D_MODEL)],
        out_shape=[jax.ShapeDtypeStruct((n, D_MODEL), F32), jax.ShapeDtypeStruct((n, D_MODEL), BF16)],
        compiler_params=_cparams(("parallel",)),
        name="branch_merge_ln1",
    )(x2d, *ys, *consts)


def _ple_ln2(x1, x1b, f, p_ref, wpg_ref, wpp_ref, g_ref, b_ref):
    ple = _sigmoid(_dot(x1b, wpg_ref[...])) * _dot(p_ref[...].astype(BF16), wpp_ref[...])
    return _layer_norm(DEEPNORM_ALPHA * x1 + f + ple, g_ref[...], b_ref[...])


FF_CHUNK = 256
TF = 512


def _ffn_dense_kernel(x_ref, xb_ref, p_ref, wi_ref, wo_ref, wpg_ref, wpp_ref, g_ref, b_ref, o_ref, acc_ref):
    xb = xb_ref[...]
    for c in range(D_FF // FF_CHUNK):
        a = _dot(xb, wi_ref[:, c * FF_CHUNK:(c + 1) * FF_CHUNK])
        u = _dot(xb, wi_ref[:, D_FF + c * FF_CHUNK:D_FF + (c + 1) * FF_CHUNK])
        part = _dot((_silu(a) * u).astype(BF16), wo_ref[c * FF_CHUNK:(c + 1) * FF_CHUNK, :])
        if c == 0:
            acc_ref[...] = part
        else:
            acc_ref[...] += part
    o_ref[...] = _ple_ln2(x_ref[...], xb, acc_ref[...], p_ref, wpg_ref, wpp_ref, g_ref, b_ref)


def _ffn_dense(x1, x1b, p2d, wts):
    n = x1.shape[0]
    row = lambda w: pl.BlockSpec((TF, w), lambda i: (i, 0))
    consts = [wts['ffn_w_in'], wts['ffn_w_out'], wts['ple_w_gate'], wts['ple_w_proj'], wts['ln2_g'], wts['ln2_b']]
    return pl.pallas_call(
        _ffn_dense_kernel,
        grid=(n // TF,),
        in_specs=[row(D_MODEL), row(D_MODEL), row(P_DIM)] + [_const_spec(c.shape) for c in consts],
        out_specs=row(D_MODEL),
        out_shape=jax.ShapeDtypeStruct((n, D_MODEL), F32),
        scratch_shapes=[pltpu.VMEM((TF, D_MODEL), F32)],
        compiler_params=_cparams(("parallel",)),
        name="ffn_dense_ple_ln2",
    )(x1, x1b, p2d, *consts)


TR = 512
INFO_LANES = 6


def _moe_route_kernel(x_ref, wr_ref, tri_ref, info_ref, cnt_ref, run_ref):
    @pl.when(pl.program_id(0) == 0)
    def _():
        run_ref[...] = jnp.zeros_like(run_ref)

    logits = jnp.dot(x_ref[...], wr_ref[...], precision=lax.Precision.HIGHEST, preferred_element_type=F32)
    lane = lax.broadcasted_iota(jnp.int32, (TR, LANES), 1)
    lane_f = lane.astype(F32)
    logits = jnp.where(lane < N_EXPERTS, logits, NEG)
    m1 = jnp.max(logits, axis=-1, keepdims=True)
    i1 = jnp.min(jnp.where(logits == m1, lane_f, float(LANES)), axis=-1, keepdims=True)
    rest = jnp.where(lane_f == i1, NEG, logits)
    m2 = jnp.max(rest, axis=-1, keepdims=True)
    i2 = jnp.min(jnp.where(rest == m2, lane_f, float(LANES)), axis=-1, keepdims=True)
    e = jnp.exp(m2 - m1)
    g1 = 1.0 / (1.0 + e)
    g2 = e / (1.0 + e)
    hot1 = lane_f == i1
    hot2 = lane_f == i2
    onehot = jnp.where(hot1 | hot2, 1.0, 0.0)
    before = _dot(tri_ref[...], onehot.astype(BF16)) + run_ref[0:1, :]
    r1 = jnp.sum(jnp.where(hot1, before, 0.0), axis=-1, keepdims=True)
    r2 = jnp.sum(jnp.where(hot2, before, 0.0), axis=-1, keepdims=True)
    run_ref[0:1, :] = run_ref[0:1, :] + jnp.sum(onehot, axis=0, keepdims=True)
    info = jnp.zeros((TR, LANES), F32)
    for k, val in enumerate((i1, i2, r1, r2, g1, g2)):
        info = jnp.where(lane == k, val, info)
    info_ref[...] = info
    cnt_ref[...] = jnp.broadcast_to(run_ref[0:1, :], cnt_ref.shape)


def _moe_route(x1, w_router_pad, tri):
    n = x1.shape[0]
    return pl.pallas_call(
        _moe_route_kernel,
        grid=(n // TR,),
        in_specs=[pl.BlockSpec((TR, D_MODEL), lambda i: (i, 0)), _const_spec(w_router_pad.shape),
                  _const_spec(tri.shape)],
        out_specs=[pl.BlockSpec((TR, LANES), lambda i: (i, 0)), pl.BlockSpec((8, LANES), lambda i: (0, 0))],
        out_shape=[jax.ShapeDtypeStruct((n, LANES), F32), jax.ShapeDtypeStruct((8, LANES), F32)],
        scratch_shapes=[pltpu.VMEM((8, LANES), F32)],
        compiler_params=_cparams(("arbitrary",)),
        name="moe_router_rank",
    )(x1, w_router_pad, tri)


def _moe_rowmap_kernel(d1_ref, d2_ref, rt_ref):
    def clear(r, _):
        rt_ref[r] = 0
        return 0

    lax.fori_loop(0, rt_ref.shape[0], clear, 0, unroll=16)

    def place(t, _):
        rt_ref[d1_ref[t]] = t
        rt_ref[d2_ref[t]] = t
        return 0

    lax.fori_loop(0, d1_ref.shape[0], place, 0, unroll=8)


def _moe_rowmap(dest1, dest2, n_rows):
    smem = pl.BlockSpec(memory_space=pltpu.SMEM)
    return pl.pallas_call(
        _moe_rowmap_kernel,
        in_specs=[smem, smem],
        out_specs=smem,
        out_shape=jax.ShapeDtypeStruct((n_rows,), jnp.int32),
        name="moe_row_map",
    )(dest1, dest2)


EF_CHUNK = 1792
EF_STEPS = D_FF_EXPERT // EF_CHUNK
ROWS_PER_STEP = MOE_BLOCK // EF_STEPS


def _moe_ffn_kernel(be_ref, na_ref, rt_ref, x_hbm, wa_ref, wu_ref, wo_ref, ys_ref, xs_ref, acc_ref, sem):
    del be_ref
    blk, c = pl.program_id(0), pl.program_id(1)
    n_live = na_ref[0]
    last_step = pl.num_programs(1) - 1
    slot = blk % 2

    def row_copy(block, r, s):
        return pltpu.make_async_copy(x_hbm.at[pl.ds(rt_ref[block * MOE_BLOCK + r], 1), :],
                                     xs_ref.at[s, pl.ds(r, 1), :], sem.at[s])

    def wait_block(s):
        pltpu.make_async_copy(x_hbm.at[pl.ds(0, MOE_BLOCK), :], xs_ref.at[s], sem.at[s]).wait()

    @pl.when(blk < n_live)
    def _():
        @pl.when((blk == 0) & (c == 0))
        def _():
            def body(r, _):
                row_copy(0, r, 0).start()
                return 0
            lax.fori_loop(0, MOE_BLOCK, body, 0)

        @pl.when(c == 0)
        def _():
            wait_block(slot)

        xb = xs_ref[slot].astype(BF16)
        h = (_silu(_dot(xb, wa_ref[...])) * _dot(xb, wu_ref[...])).astype(BF16)
        part = _dot(h, wo_ref[...])

        nxt = jnp.minimum(blk + 1, pl.num_programs(0) - 1)
        for r in range(ROWS_PER_STEP):
            row_copy(nxt, c * ROWS_PER_STEP + r, 1 - slot).start()

        @pl.when(c == 0)
        def _():
            acc_ref[...] = part

        @pl.when(c > 0)
        def _():
            acc_ref[...] += part

        @pl.when(c == last_step)
        def _():
            ys_ref[...] = acc_ref[...]

        @pl.when((c == last_step) & (blk == n_live - 1))
        def _():
            wait_block(1 - slot)

    @pl.when((blk >= n_live) & (c == last_step))
    def _():
        ys_ref[...] = jnp.zeros_like(ys_ref)


def _moe_ffn(x1, row_tok, blk_expert, n_active, w_in, w_out):
    n_rows = row_tok.shape[0]
    n_blk = n_rows // MOE_BLOCK
    n_ch = EF_STEPS
    live = lambda b, na: jnp.minimum(b, na[0] - 1)
    chunk = lambda b, c, na: jnp.where(b < na[0], c, n_ch - 1)
    return pl.pallas_call(
        _moe_ffn_kernel,
        grid_spec=pltpu.PrefetchScalarGridSpec(
            num_scalar_prefetch=3,
            grid=(n_blk, n_ch),
            in_specs=[
                pl.BlockSpec(memory_space=pl.ANY),
                pl.BlockSpec((None, D_MODEL, EF_CHUNK),
                             lambda b, c, be, na, rt: (be[live(b, na)], 0, chunk(b, c, na))),
                pl.BlockSpec((None, D_MODEL, EF_CHUNK),
                             lambda b, c, be, na, rt: (be[live(b, na)], 0, n_ch + chunk(b, c, na))),
                pl.BlockSpec((None, EF_CHUNK, D_MODEL),
                             lambda b, c, be, na, rt: (be[live(b, na)], chunk(b, c, na), 0)),
            ],
            out_specs=pl.BlockSpec((MOE_BLOCK, D_MODEL), lambda b, c, be, na, rt: (b, 0)),
            scratch_shapes=[pltpu.VMEM((2, MOE_BLOCK, D_MODEL), F32), pltpu.VMEM((MOE_BLOCK, D_MODEL), F32),
                            pltpu.SemaphoreType.DMA((2,))],
        ),
        out_shape=jax.ShapeDtypeStruct((n_rows, D_MODEL), F32),
        compiler_params=_cparams(("arbitrary", "arbitrary")),
        name="moe_expert_swiglu",
    )(blk_expert, n_active, row_tok, x1, w_in, w_in, w_out)


TC = 256


def _moe_combine_kernel(d1_ref, d2_ref, ys_hbm, x_ref, xb_ref, p_ref, info_ref, wpg_ref, wpp_ref, g_ref,
                        b_ref, o_ref, ya_ref, yb_ref, sem):
    i = pl.program_id(0)
    last = pl.num_programs(0) - 1
    slot = i % 2

    def copies(tile, r, s):
        t = tile * TC + r
        return (pltpu.make_async_copy(ys_hbm.at[pl.ds(d1_ref[t], 1), :], ya_ref.at[s, pl.ds(r, 1), :], sem.at[0, s]),
                pltpu.make_async_copy(ys_hbm.at[pl.ds(d2_ref[t], 1), :], yb_ref.at[s, pl.ds(r, 1), :], sem.at[1, s]))

    def wait_tile(s):
        pltpu.make_async_copy(ys_hbm.at[pl.ds(0, TC), :], ya_ref.at[s], sem.at[0, s]).wait()
        pltpu.make_async_copy(ys_hbm.at[pl.ds(0, TC), :], yb_ref.at[s], sem.at[1, s]).wait()

    @pl.when(i == 0)
    def _():
        def body(r, _):
            for cp in copies(0, r, 0):
                cp.start()
            return 0
        lax.fori_loop(0, TC, body, 0)

    wait_tile(slot)
    nxt = jnp.minimum(i + 1, last)
    for r in range(TC):
        for cp in copies(nxt, r, 1 - slot):
            cp.start()
    info = info_ref[...]
    f = info[:, 4:5] * ya_ref[slot] + info[:, 5:6] * yb_ref[slot]
    o_ref[...] = _ple_ln2(x_ref[...], xb_ref[...], f, p_ref, wpg_ref, wpp_ref, g_ref, b_ref)

    @pl.when(i == last)
    def _():
        wait_tile(1 - slot)


def _moe_combine(ys, dest1, dest2, x1, x1b, p2d, info, wts):
    n = x1.shape[0]
    row = lambda w: pl.BlockSpec((TC, w), lambda i, d1, d2: (i, 0))
    consts = [wts['ple_w_gate'], wts['ple_w_proj'], wts['ln2_g'], wts['ln2_b']]
    cspec = lambda c: pl.BlockSpec(c.shape, lambda i, d1, d2: (0,) * c.ndim, pipeline_mode=pl.Buffered(1))
    return pl.pallas_call(
        _moe_combine_kernel,
        grid_spec=pltpu.PrefetchScalarGridSpec(
            num_scalar_prefetch=2,
            grid=(n // TC,),
            in_specs=[pl.BlockSpec(memory_space=pl.ANY), row(D_MODEL), row(D_MODEL), row(P_DIM), row(LANES)]
            + [cspec(c) for c in consts],
            out_specs=row(D_MODEL),
            scratch_shapes=[pltpu.VMEM((2, TC, D_MODEL), F32), pltpu.VMEM((2, TC, D_MODEL), F32),
                            pltpu.SemaphoreType.DMA((2, 2))],
        ),
        out_shape=jax.ShapeDtypeStruct((n, D_MODEL), F32),
        compiler_params=_cparams(("arbitrary",)),
        name="moe_combine_ple_ln2",
    )(dest1, dest2, ys, x1, x1b, p2d, info, *consts)


def _moe(x1, x1b, p2d, wts):
    n = x1.shape[0]
    n_rows = ((n * 2 + MOE_BLOCK - 1) // MOE_BLOCK) * MOE_BLOCK + N_EXPERTS * MOE_BLOCK
    info, cnt = _moe_route(x1, wts['w_router'], wts['tri_tokens'])
    counts = cnt[0, :N_EXPERTS].astype(jnp.int32)
    padded = ((counts + MOE_BLOCK - 1) // MOE_BLOCK) * MOE_BLOCK
    ends = jnp.cumsum(padded)
    start_pad = ends - padded
    e1, e2 = info[:, 0].astype(jnp.int32), info[:, 1].astype(jnp.int32)
    dest1 = start_pad[e1] + info[:, 2].astype(jnp.int32)
    dest2 = start_pad[e2] + info[:, 3].astype(jnp.int32)
    n_blk = n_rows // MOE_BLOCK
    blk_row0 = jnp.arange(n_blk, dtype=jnp.int32) * MOE_BLOCK
    blk_expert = jnp.minimum(jnp.sum((ends[None, :] <= blk_row0[:, None]).astype(jnp.int32), axis=1),
                             N_EXPERTS - 1)
    n_active = (ends[-1:] // MOE_BLOCK).astype(jnp.int32)
    row_tok = _moe_rowmap(dest1, dest2, n_rows)
    ys = _moe_ffn(x1, row_tok, blk_expert, n_active, wts['moe_w_in'], wts['moe_w_out'])
    return _moe_combine(ys, dest1, dest2, x1, x1b, p2d, info, wts)


def _rot_half_cols(w, heads, dim):
    w3 = w.reshape(w.shape[0], heads, dim)
    half = dim // 2
    return jnp.concatenate([-w3[..., half:], w3[..., :half]], axis=-1).reshape(w.shape[0], heads * dim)


def _prep_layer(i, w_in, conv_w, conv_b, conv_ln_g, conv_ln_b, nsa_cmp_pe, nsa_cmp_w1, nsa_cmp_w2,
                mla_q_norm, mla_kv_norm, mla_w_uq, mla_w_ukv, w_branch, w_out, ln1_g, ln1_b,
                ple_w_gate, ple_w_proj, ln2_g, ln2_b):
    w = w_in[i]
    d = w.shape[0]
    z = lambda n: jnp.zeros((d, n), F32)
    dup = lambda a: jnp.concatenate([a, a], axis=1)
    c_glu, nq = w[:, 0:512], w[:, 512:768]
    nkv = w[:, 768:1152]
    k_cmp, v_cmp, k_slc, v_slc, k_win, v_win = [nkv[:, j * 64:(j + 1) * 64] for j in range(6)]
    ng = w[:, 1152:1164]
    mq, mkv, mkr = w[:, 1164:1420], w[:, 1420:1548], w[:, 1548:1580]
    sb = w[:, 1580:2348]
    bg = w[:, 2348:6444]
    cols = [c_glu, nq, _rot_half_cols(nq, NSA_HEADS, HEAD_DIM),
            dup(k_slc), dup(_rot_half_cols(k_slc, 1, HEAD_DIM)),
            dup(k_win), dup(_rot_half_cols(k_win, 1, HEAD_DIM)),
            dup(v_slc), dup(v_win), k_cmp, v_cmp, ng, z(LANES - 12), mq, mkv,
            z(64), mkr, z(32), z(64), _rot_half_cols(mkr, 1, MLA_ROPE), z(32), z(LANES), sb]
    w1 = jnp.concatenate(cols, axis=1).astype(BF16)
    assert w1.shape[1] == C_TOT

    inv32 = ROPE_THETA ** (-jnp.arange(HEAD_DIM // 2, dtype=F32) / (HEAD_DIM // 2))
    inv16 = ROPE_THETA ** (-jnp.arange(MLA_ROPE // 2, dtype=F32) / (MLA_ROPE // 2))
    inv_nsa = jnp.tile(inv32, 4)[None, :]
    inv_mla = jnp.concatenate([jnp.zeros((64,), F32), inv16, inv16, jnp.zeros((32,), F32)])[None, :]

    wuq = mla_w_uq[i].reshape(MLA_Q_RANK, MLA_HEADS, MLA_NOPE + MLA_ROPE)
    zq = jnp.zeros((MLA_Q_RANK, MLA_HEADS, 32), F32)
    wq = jnp.concatenate([wuq, zq], axis=-1).reshape(MLA_Q_RANK, MLA_HEADS * LANES)
    rope_rot = jnp.concatenate([-wuq[..., MLA_NOPE + 16:], wuq[..., MLA_NOPE:MLA_NOPE + 16]], axis=-1)
    wqr = jnp.concatenate([jnp.zeros((MLA_Q_RANK, MLA_HEADS, MLA_NOPE), F32), rope_rot, zq],
                          axis=-1).reshape(MLA_Q_RANK, MLA_HEADS * LANES)
    wukv = mla_w_ukv[i].reshape(MLA_KV_RANK, MLA_HEADS, MLA_NOPE + MLA_V)
    wk = jnp.concatenate([wukv[..., :MLA_NOPE], jnp.zeros((MLA_KV_RANK, MLA_HEADS, 64), F32)],
                         axis=-1).reshape(MLA_KV_RANK, MLA_HEADS * LANES)
    wv = wukv[..., MLA_NOPE:].reshape(MLA_KV_RANK, MLA_HEADS * MLA_V)

    pe = nsa_cmp_pe[i]
    pe_rows = pe.reshape(CMP_BLOCK, 2 * HEAD_DIM)
    pe_a = pe_rows[:CMP_STRIDE].reshape(1, CMP_STRIDE * LANES)
    pe_b = pe_rows[CMP_STRIDE:].reshape(1, CMP_STRIDE * LANES)
    w1c = nsa_cmp_w1[i].reshape(2, CMP_BLOCK, HEAD_DIM, HEAD_DIM)
    zblk = jnp.zeros((CMP_BLOCK, HEAD_DIM, HEAD_DIM), F32)
    w1full = jnp.concatenate([jnp.concatenate([w1c[0], zblk], axis=2),
                              jnp.concatenate([zblk, w1c[1]], axis=2)], axis=1)
    w1a = w1full[:CMP_STRIDE].reshape(CMP_STRIDE * LANES, LANES).astype(BF16)
    w1b = w1full[CMP_STRIDE:].reshape(CMP_STRIDE * LANES, LANES).astype(BF16)
    w2 = nsa_cmp_w2[i]
    z64 = jnp.zeros((HEAD_DIM, LANES), F32)
    w2k = jnp.concatenate([dup(w2[0]), z64], axis=0).astype(BF16)
    w2v = jnp.concatenate([z64, dup(w2[1])], axis=0).astype(BF16)

    return dict(
        w1=w1, inv_nsa=inv_nsa, inv_mla=inv_mla,
        mla_qn=mla_q_norm[i][None, :], mla_kvn=mla_kv_norm[i][None, :],
        wq=wq.astype(BF16), wqr=wqr.astype(BF16), wk=wk.astype(BF16), wv=wv.astype(BF16),
        conv_w=conv_w[i], conv_b=conv_b[i][None, :], conv_g=conv_ln_g[i][None, :], conv_beta=conv_ln_b[i][None, :],
        pe_a=pe_a, pe_b=pe_b, w1a=w1a, w1b=w1b, w2k=w2k, w2v=w2v,
        w_gate=bg.astype(BF16), w_branch=w_branch[i].astype(BF16), w_out=w_out[i].astype(BF16),
        ln1_g=ln1_g[i][None, :], ln1_b=ln1_b[i][None, :],
        ple_w_gate=ple_w_gate[i].astype(BF16), ple_w_proj=ple_w_proj[i].astype(BF16),
        ln2_g=ln2_g[i][None, :], ln2_b=ln2_b[i][None, :],
    )


def _tables(seq):
    n_cmp_rows = seq // CMP_STRIDE
    n_sel = seq // SEL_BLOCK
    cmp_start = jnp.arange(n_cmp_rows) * CMP_STRIDE
    sel_start = jnp.arange(LANES) * SEL_BLOCK
    n_cmp = (seq - CMP_BLOCK) // CMP_STRIDE + 1
    overlap = ((cmp_start[:, None] < sel_start[None, :] + SEL_BLOCK)
               & (cmp_start[:, None] + CMP_BLOCK > sel_start[None, :])
               & (jnp.arange(n_cmp_rows)[:, None] < n_cmp) & (jnp.arange(LANES)[None, :] < n_sel))
    kb = jnp.arange(seq // TK)[:, None, None]
    nn = jnp.arange(LANES)[None, :, None]
    ll = jnp.arange(TK)[None, None, :]
    expand = (kb * TK + ll) // SEL_BLOCK == nn
    jj = jnp.arange(TK)
    tri_keys = jnp.tile(jj[:, None] >= jj[None, :], (2, 1))
    tt = jnp.arange(TR)
    tri_tokens = tt[None, :] < tt[:, None]
    return dict(overlap_t=overlap.T.astype(BF16), eye_q=jnp.eye(TQ, dtype=BF16),
                expand=expand.astype(BF16), tri_keys=tri_keys.astype(BF16),
                tri_tokens=tri_tokens.astype(BF16))


def kernel(x, p, positions, w_in, conv_w, conv_b, conv_ln_g, conv_ln_b, nsa_cmp_pe, nsa_cmp_w1, nsa_cmp_w2,
           mla_q_norm, mla_kv_norm, mla_w_uq, mla_w_ukv, w_branch, w_out, ln1_g, ln1_b, ffn_w_in, ffn_w_out,
           moe_router, moe_w_in, moe_w_out, ple_w_gate, ple_w_proj, ln2_g, ln2_b):
    batch, seq, _ = x.shape
    n = batch * seq
    tabs = _tables(seq)
    x2d = x.reshape(n, D_MODEL)
    pos2d = positions.reshape(n, 1)
    for i in range(DEPTH):
        wts = _prep_layer(i, w_in, conv_w, conv_b, conv_ln_g, conv_ln_b, nsa_cmp_pe, nsa_cmp_w1, nsa_cmp_w2,
                          mla_q_norm, mla_kv_norm, mla_w_uq, mla_w_ukv, w_branch, w_out, ln1_g, ln1_b,
                          ple_w_gate, ple_w_proj, ln2_g, ln2_b)
        wts['overlap_t'], wts['eye_q'] = tabs['overlap_t'], tabs['eye_q']
        (conv_in, nq, nqr, ks, kw, vs, vw, kvc, ng, mq, mk, mv, sq, sk, sv) = _mixer_in(x2d, pos2d, wts)
        y_a = _conv(conv_in, wts['conv_w'], wts['conv_b'], wts['conv_g'], wts['conv_beta'], batch, seq)
        ocmp, sel = _nsa_cmp(kvc, nq, wts, batch, seq)
        y_b = _nsa_attn(nqr, ks, vs, kw, vw, sel, tabs['expand'], ocmp, ng, batch, seq)
        y_c = _mla_attn(mq, mk, mv, batch, seq)
        y_d = _sb_attn(sq, sk, sv, tabs['tri_keys'], batch, seq)
        ys = [y.reshape(n, BRANCH_W) for y in (y_a, y_b, y_c, y_d)]
        x1, x1b = _merge(x2d, ys, wts)
        p2d = p[i].reshape(n, P_DIM)
        if i % 2 == 0:
            wts['ffn_w_in'] = ffn_w_in[i // 2].astype(BF16)
            wts['ffn_w_out'] = ffn_w_out[i // 2].astype(BF16)
            x2d = _ffn_dense(x1, x1b, p2d, wts)
        else:
            wts['w_router'] = jnp.concatenate(
                [moe_router[i // 2], jnp.zeros((D_MODEL, LANES - N_EXPERTS), F32)], axis=1)
            wts['tri_tokens'] = tabs['tri_tokens']
            wts['moe_w_in'] = moe_w_in[i // 2].astype(BF16)
            wts['moe_w_out'] = moe_w_out[i // 2].astype(BF16)
            x2d = _moe(x1, x1b, p2d, wts)
    return x2d.reshape(batch, seq, D_MODEL)
```

```python
import functools

import jax
import jax.numpy as jnp
from jax import lax
from jax.experimental import pallas as pl
from jax.experimental.pallas import tpu as pltpu

F32 = jnp.float32
BF16 = jnp.bfloat16

D_MODEL = 1024
DEPTH = 2
CONV_CH = 256
CONV_WIDTH = 31
NSA_HEADS = 4
HEAD_DIM = 64
CMP_BLOCK = 32
CMP_STRIDE = 16
SEL_BLOCK = 64
SEL_TOPN = 16
WINDOW = 512
MLA_HEADS = 4
MLA_Q_RANK = 256
MLA_KV_RANK = 128
MLA_NOPE = 64
MLA_ROPE = 32
MLA_V = 64
N_BRANCH = 4
BRANCH_W = 256
ROPE_THETA = 10000.0
LN_EPS = 1e-5
RMS_EPS = 1e-6
D_FF = 2816
N_EXPERTS = 8
D_FF_EXPERT = 3584
MOE_BLOCK = 512
P_DIM = 256
DEEPNORM_ALPHA = (2 * DEPTH) ** 0.25

LANES = 128
SUBLANES = 8
VMEM_LIMIT = 56 * 1024 * 1024

NEG = -1e30

C_CONV = 0
C_NQ = 512
C_NQR = 768
C_KS = 1024
C_KW = 1280
C_VS = 1536
C_KVC = 1792
C_MQ = 2048
C_MKV = 2304
C_MKRR = 2560
C_SB = 2816
C_TOT = 3584

TM = 256
TF = 512
TQ = 256
TK = 256


def _cparams(sem, vmem=VMEM_LIMIT):
    return pltpu.CompilerParams(dimension_semantics=sem, vmem_limit_bytes=vmem)


def _const_spec(shape):
    nd = len(shape)
    return pl.BlockSpec(shape, lambda *_: (0,) * nd, pipeline_mode=pl.Buffered(1))


def _dot(a, b):
    return jnp.dot(a, b, preferred_element_type=F32)


def _dot_nt(a, b):
    return lax.dot_general(a, b, (((1,), (1,)), ((), ())), preferred_element_type=F32)


def _layer_norm(h, g, b):
    mu = jnp.mean(h, axis=-1, keepdims=True)
    d = h - mu
    var = jnp.mean(d * d, axis=-1, keepdims=True)
    return d * lax.rsqrt(var + LN_EPS) * g + b


def _rms_norm(h, g):
    return h * lax.rsqrt(jnp.mean(h * h, axis=-1, keepdims=True) + RMS_EPS) * g


def _sigmoid(x):
    return 1.0 / (1.0 + jnp.exp(-x))


def _silu(x):
    return x * _sigmoid(x)


def _split_bf16(x):
    hi = x.astype(BF16)
    lo = (x - hi.astype(F32)).astype(BF16)
    return hi, lo


def _half_select(sub, x):
    lane = lax.broadcasted_iota(jnp.int32, x.shape, 1)
    keep = (lane < HEAD_DIM) if sub == 0 else (lane >= HEAD_DIM)
    return jnp.where(keep, x, 0.0)


def _mixer_in_kernel(x_ref, pos_ref, w_ref, invn_ref, invm_ref, qn_ref, kvn_ref, wq_ref, wqr_ref,
                     wk_ref, wv_ref,
                     conv_ref, nq_ref, nqr_ref, ks_ref, kw_ref, vs_ref, vw_ref, kvc_ref, ng_ref,
                     mq_ref, mk_ref, mv_ref, sq_ref, sk_ref, sv_ref):
    xb = x_ref[...].astype(BF16)

    def proj(c0, width):
        return _dot(xb, w_ref[:, c0:c0 + width])

    posf = pos_ref[...].astype(F32)
    ang_n = posf * invn_ref[...]
    cos_n, sin_n = jnp.cos(ang_n), jnp.sin(ang_n)
    ang_m = posf * invm_ref[...]
    cos_m, sin_m = jnp.cos(ang_m), jnp.sin(ang_m)

    lo, hi = slice(0, LANES), slice(LANES, 2 * LANES)
    qn = _rms_norm(proj(C_MQ, MLA_Q_RANK), qn_ref[...]).astype(BF16)
    mkv = proj(C_MKV, 256)
    kvn = _rms_norm(mkv[:, lo], kvn_ref[...]).astype(BF16)

    conv_ref[...] = proj(C_CONV, 512)

    scale = HEAD_DIM ** -0.5
    cos_n2 = jnp.concatenate([cos_n, cos_n], axis=1)
    sin_n2 = jnp.concatenate([sin_n, sin_n], axis=1)
    q, qrot = proj(C_NQ, 256), proj(C_NQR, 256)
    nq_ref[...] = (q * scale).astype(BF16)
    nqr_ref[...] = ((q * cos_n2 + qrot * sin_n2) * scale).astype(BF16)
    k = proj(C_KS, 256)
    ks_ref[...] = (k[:, lo] * cos_n + k[:, hi] * sin_n).astype(BF16)
    k = proj(C_KW, 256)
    kw_ref[...] = (k[:, lo] * cos_n + k[:, hi] * sin_n).astype(BF16)
    v = proj(C_VS, 256)
    vs_ref[...] = v[:, lo].astype(BF16)
    vw_ref[...] = v[:, hi].astype(BF16)
    u = proj(C_KVC, 256)
    kvc_ref[...] = u[:, lo]
    ng_ref[...] = u[:, hi]

    kr = mkv[:, hi] * cos_m + proj(C_MKRR, 256)[:, lo] * sin_m
    qa, qr, kk = _dot(qn, wq_ref[...]), _dot(qn, wqr_ref[...]), _dot(kvn, wk_ref[...])
    for h in range(MLA_HEADS):
        sl = slice(h * LANES, (h + 1) * LANES)
        mq_ref[:, sl] = (qa[:, sl] * cos_m + qr[:, sl] * sin_m).astype(BF16)
        mk_ref[:, sl] = (kk[:, sl] + kr).astype(BF16)
    mv_ref[...] = _dot(kvn, wv_ref[...]).astype(BF16)

    sq_ref[...] = (proj(C_SB, 256) * scale).astype(BF16)
    sk_ref[...] = proj(C_SB + 256, 256).astype(BF16)
    sv_ref[...] = proj(C_SB + 512, 256).astype(BF16)


def _mixer_in(x2d, pos2d, wts):
    n = x2d.shape[0]
    row = lambda w: pl.BlockSpec((TF, w), lambda i: (i, 0))
    out_widths = [512, 256, 256, 128, 128, 128, 128, 128, 128, 512, 512, 256, 256, 256, 256]
    out_dtypes = [F32, BF16, BF16, BF16, BF16, BF16, BF16, F32, F32, BF16, BF16, BF16, BF16, BF16, BF16]
    consts = [wts['w1'], wts['inv_nsa'], wts['inv_mla'], wts['mla_qn'], wts['mla_kvn'], wts['wq'],
              wts['wqr'], wts['wk'], wts['wv']]
    return pl.pallas_call(
        _mixer_in_kernel,
        grid=(n // TF,),
        in_specs=[row(D_MODEL), row(1)] + [_const_spec(c.shape) for c in consts],
        out_specs=[row(w) for w in out_widths],
        out_shape=[jax.ShapeDtypeStruct((n, w), d) for w, d in zip(out_widths, out_dtypes)],
        compiler_params=_cparams(("parallel",)),
        name="mixer_in",
    )(x2d, pos2d, *consts)


CONV_PAD = 32
CONV_CHUNK = 128


def _conv_kernel(u_ref, w_ref, b_ref, g_ref, beta_ref, o_ref, hp_ref):
    seq = u_ref.shape[0]
    hp_ref[0, 0:CONV_PAD, :] = jnp.zeros((CONV_PAD, CONV_CH), F32)
    hp_ref[0, CONV_PAD:CONV_PAD + seq, :] = u_ref[:, 0:CONV_CH] * _sigmoid(u_ref[:, CONV_CH:2 * CONV_CH])
    rows = seq + CONV_PAD - SUBLANES
    for s in range(1, SUBLANES):
        hp_ref[s, 0:rows, :] = hp_ref[0, s:s + rows, :]
    first = CONV_PAD - (CONV_WIDTH - 1)
    for c in range(seq // CONV_CHUNK):
        base = c * CONV_CHUNK
        acc = jnp.broadcast_to(b_ref[...], (CONV_CHUNK, CONV_CH))
        for j in range(CONV_WIDTH):
            a, s = divmod(first + j, SUBLANES)
            r0 = base + a * SUBLANES
            acc = acc + hp_ref[s, r0:r0 + CONV_CHUNK, :] * w_ref[j:j + 1, :]
        y = _layer_norm(acc, g_ref[...], beta_ref[...])
        o_ref[base:base + CONV_CHUNK, :] = _silu(y).astype(BF16)


def _conv(conv_in, w, b, g, beta, batch, seq):
    return pl.pallas_call(
        _conv_kernel,
        grid=(batch,),
        in_specs=[pl.BlockSpec((None, seq, 2 * CONV_CH), lambda i: (i, 0, 0)),
                  _const_spec(w.shape), _const_spec(b.shape), _const_spec(g.shape), _const_spec(beta.shape)],
        out_specs=pl.BlockSpec((None, seq, CONV_CH), lambda i: (i, 0, 0)),
        out_shape=jax.ShapeDtypeStruct((batch, seq, CONV_CH), BF16),
        scratch_shapes=[pltpu.VMEM((SUBLANES, CONV_PAD + seq, CONV_CH), F32)],
        compiler_params=_cparams(("parallel",)),
        name="conformer_conv",
    )(conv_in.reshape(batch, seq, 2 * CONV_CH), w, b, g, beta)


def _gelu_tanh(x):
    return 0.5 * x * (1.0 + jnp.tanh(0.7978845608028654 * (x + 0.044715 * x * x * x)))


SEL_SHIFT = 6


def _nsa_cmp_kernel(kvc_ref, q_ref, pea_ref, peb_ref, w1a_ref, w1b_ref, w2k_ref, w2v_ref, ovt_ref, eye_ref,
                    ocmp_ref, sel_ref, *, seq):
    n_cmp = (seq - CMP_BLOCK) // CMP_STRIDE + 1
    n_sel = seq // SEL_BLOCK
    nb = seq // CMP_STRIDE
    x2 = kvc_ref[...]
    xa = (x2 + pea_ref[...]).astype(BF16)
    xb = (x2 + peb_ref[...]).astype(BF16)
    ha = _dot(xa, w1a_ref[...])
    hb = _dot(xb, w1b_ref[...])
    hid = ha + pltpu.roll(hb, nb - 1, 0)
    hid = _gelu_tanh(hid).astype(BF16)
    kk = _dot(hid, w2k_ref[...]).astype(BF16)
    vv = _dot(hid, w2v_ref[...]).astype(BF16)
    ovt = ovt_ref[...]

    for c in range(seq // TQ):
        r0 = c * TQ
        t = r0 + lax.broadcasted_iota(jnp.int32, (TQ, LANES), 0)
        j = lax.broadcasted_iota(jnp.int32, (TQ, LANES), 1)
        valid = (j * CMP_STRIDE + CMP_BLOCK - 1 <= t) & (j < n_cmp)
        psum = jnp.zeros((TQ, LANES), F32)
        for pair in range(2):
            qp = q_ref[r0:r0 + TQ, pair * LANES:(pair + 1) * LANES].astype(F32)
            outs = []
            for sub in range(2):
                qm = _half_select(sub, qp).astype(BF16)
                s = jnp.where(valid, _dot_nt(qm, kk), NEG)
                m = jnp.max(s, axis=-1, keepdims=True)
                e = jnp.where(valid, jnp.exp(s - m), 0.0)
                den = jnp.sum(e, axis=-1, keepdims=True)
                p = e / jnp.where(den > 0, den, 1.0)
                psum = psum + p
                outs.append(_dot(p.astype(BF16), vv))
            ocmp_ref[r0:r0 + TQ, pair * LANES:(pair + 1) * LANES] = jnp.where(
                lax.broadcasted_iota(jnp.int32, (TQ, LANES), 1) < HEAD_DIM, outs[0], outs[1])
        p_hi, p_lo = _split_bf16(psum)
        imp = (_dot_nt(ovt, p_hi) + _dot_nt(ovt, p_lo))[0:n_sel, :]
        n = lax.broadcasted_iota(jnp.int32, (n_sel, TQ), 0)
        cur = jnp.right_shift(r0 + lax.broadcasted_iota(jnp.int32, (n_sel, TQ), 1), SEL_SHIFT)
        forced = (n == 0) | (n == cur) | (n == cur - 1)
        imp = jnp.where(forced, jnp.inf, imp)
        imp = jnp.where(n > cur, -jnp.inf, imp)
        rank = jnp.zeros((n_sel, TQ), F32)
        for n2 in range(n_sel):
            other = imp[n2:n2 + 1, :]
            ahead = (other > imp) | ((other == imp) & (n2 < n))
            rank = rank + jnp.where(ahead, 1.0, 0.0)
        sel_t = jnp.where((rank < SEL_TOPN) & (imp > -jnp.inf), 1.0, 0.0)
        sel_t = jnp.concatenate([sel_t, jnp.zeros((LANES - n_sel, TQ), F32)], axis=0).astype(BF16)
        sel_ref[r0:r0 + TQ, :] = _dot_nt(eye_ref[...], sel_t).astype(BF16)


def _nsa_cmp(kvc, nq, wts, batch, seq):
    nb = seq // CMP_STRIDE
    consts = [wts['pe_a'], wts['pe_b'], wts['w1a'], wts['w1b'], wts['w2k'], wts['w2v'], wts['overlap_t'],
              wts['eye_q']]
    return pl.pallas_call(
        functools.partial(_nsa_cmp_kernel, seq=seq),
        grid=(batch,),
        in_specs=[pl.BlockSpec((None, nb, CMP_STRIDE * LANES), lambda i: (i, 0, 0)),
                  pl.BlockSpec((None, seq, 256), lambda i: (i, 0, 0))] + [_const_spec(c.shape) for c in consts],
        out_specs=[pl.BlockSpec((None, seq, 256), lambda i: (i, 0, 0)),
                   pl.BlockSpec((None, seq, LANES), lambda i: (i, 0, 0))],
        out_shape=[jax.ShapeDtypeStruct((batch, seq, 256), F32),
                   jax.ShapeDtypeStruct((batch, seq, LANES), BF16)],
        compiler_params=_cparams(("parallel",)),
        name="nsa_compress_select",
    )(kvc.reshape(batch, nb, CMP_STRIDE * LANES), nq.reshape(batch, seq, 256), *consts)


LOG2E = 1.4426950408889634
N_HEADS = 4


def _softmax_scratch(n_tiles):
    slab = pltpu.VMEM((N_HEADS, TQ, LANES), F32)
    return [pltpu.VMEM((N_HEADS, n_tiles, TQ, TK), F32), slab, slab, slab, slab]


def _scores_put(h, t, s, s_ref, mx_ref):
    s_ref[h, t] = s
    mx_ref[h] = jnp.maximum(mx_ref[h], jnp.maximum(s[:, :LANES], s[:, LANES:]))


def _row_max(mx_ref, mb_ref):
    for h in range(N_HEADS):
        mb_ref[h] = jnp.broadcast_to(jnp.max(mx_ref[h], axis=-1, keepdims=True), (TQ, LANES))


def _probs_accumulate(h, t, c, v_blk, s_ref, mb_ref, ls_ref, acc_ref):
    s, mb = s_ref[h, t], mb_ref[h]
    pa = jnp.exp2((s[:, :LANES] - mb) * c)
    pb = jnp.exp2((s[:, LANES:] - mb) * c)
    ls_ref[h] += pa + pb
    acc_ref[h] += _dot(jnp.concatenate([pa, pb], axis=1).astype(BF16), v_blk)


def _softmax_out(h, ls_ref, acc_ref):
    return acc_ref[h] / jnp.sum(ls_ref[h], axis=-1, keepdims=True)


def _tile_iotas():
    return (lax.broadcasted_iota(jnp.int32, (TQ, TK), 0), lax.broadcasted_iota(jnp.int32, (TQ, TK), 1))


def _nsa_attn_kernel(q_ref, ks_ref, vs_ref, kw_ref, vw_ref, sel_ref, exp_ref, ocmp_ref, ng_ref, o_ref,
                     qm_ref, ss_ref, mxs_ref, mbs_ref, lss_ref, accs_ref,
                     sw_ref, mxw_ref, mbw_ref, lsw_ref, accw_ref):
    i = pl.program_id(1)
    row, col = _tile_iotas()
    sel = sel_ref[...]
    for ref in (mxs_ref, mxw_ref):
        ref[...] = jnp.full(ref.shape, NEG, F32)
    for ref in (lss_ref, accs_ref, lsw_ref, accw_ref):
        ref[...] = jnp.zeros(ref.shape, F32)
    for pair in range(2):
        qp = q_ref[:, pair * LANES:(pair + 1) * LANES].astype(F32)
        for sub in range(2):
            qm_ref[2 * pair + sub] = _half_select(sub, qp).astype(BF16)

    def selected_scores(kb, causal):
        k0 = pl.multiple_of(kb * TK, TK)
        hit = _dot(sel, exp_ref[kb]) > 0.5
        if causal:
            hit = hit & (col <= row)
        bias = jnp.where(hit, 0.0, NEG)
        k_blk = ks_ref[pl.ds(k0, TK), :]
        for h in range(N_HEADS):
            _scores_put(h, kb, _dot_nt(qm_ref[h], k_blk) + bias, ss_ref, mxs_ref)

    def window_scores(slot, mask):
        k0 = pl.multiple_of((i - 2 + slot) * TK, TK)
        k_blk = kw_ref[pl.ds(k0, TK), :]
        for h in range(N_HEADS):
            s = _dot_nt(qm_ref[h], k_blk)
            if mask is not None:
                s = jnp.where(mask, s, NEG)
            _scores_put(h, slot, s, sw_ref, mxw_ref)

    def off_diagonal(kb, _):
        selected_scores(kb, False)
        return 0

    lax.fori_loop(0, i, off_diagonal, 0)
    selected_scores(i, True)
    pl.when(i >= 2)(lambda: window_scores(0, col > row))
    pl.when(i >= 1)(lambda: window_scores(1, None))
    window_scores(2, col <= row)
    _row_max(mxs_ref, mbs_ref)
    _row_max(mxw_ref, mbw_ref)

    def selected_probs(kb, _):
        v_blk = vs_ref[pl.ds(pl.multiple_of(kb * TK, TK), TK), :]
        for h in range(N_HEADS):
            _probs_accumulate(h, kb, LOG2E, v_blk, ss_ref, mbs_ref, lss_ref, accs_ref)
        return 0

    def window_probs(slot):
        v_blk = vw_ref[pl.ds(pl.multiple_of((i - 2 + slot) * TK, TK), TK), :]
        for h in range(N_HEADS):
            _probs_accumulate(h, slot, LOG2E, v_blk, sw_ref, mbw_ref, lsw_ref, accw_ref)

    lax.fori_loop(0, i + 1, selected_probs, 0)
    pl.when(i >= 2)(lambda: window_probs(0))
    pl.when(i >= 1)(lambda: window_probs(1))
    window_probs(2)

    g = _sigmoid(ng_ref[...])
    lane = lax.broadcasted_iota(jnp.int32, (TQ, LANES), 1)
    for pair in range(2):
        res = []
        for sub in range(2):
            h = 2 * pair + sub
            res.append(g[:, 3 * h + 1:3 * h + 2] * _softmax_out(h, lss_ref, accs_ref)
                       + g[:, 3 * h + 2:3 * h + 3] * _softmax_out(h, lsw_ref, accw_ref)
                       + g[:, 3 * h:3 * h + 1] * ocmp_ref[:, pair * LANES:(pair + 1) * LANES])
        o_ref[:, pair * LANES:(pair + 1) * LANES] = jnp.where(lane < HEAD_DIM, res[0], res[1]).astype(BF16)


def _nsa_attn(nqr, ks, vs, kw, vw, sel, expand, ocmp, ng, batch, seq):
    qspec = lambda w: pl.BlockSpec((None, TQ, w), lambda b, i: (b, i, 0))
    kspec = pl.BlockSpec((None, seq, LANES), lambda b, i: (b, 0, 0))
    r3 = lambda a: a.reshape(batch, seq, a.shape[-1])
    return pl.pallas_call(
        _nsa_attn_kernel,
        grid=(batch, seq // TQ),
        in_specs=[qspec(256), kspec, kspec, kspec, kspec, qspec(LANES), _const_spec(expand.shape),
                  qspec(256), qspec(LANES)],
        out_specs=qspec(256),
        out_shape=jax.ShapeDtypeStruct((batch, seq, 256), BF16),
        scratch_shapes=[pltpu.VMEM((N_HEADS, TQ, LANES), BF16)] + _softmax_scratch(seq // TK)
        + _softmax_scratch(WINDOW // TK + 1),
        compiler_params=_cparams(("parallel", "parallel")),
        name="nsa_select_window",
    )(r3(nqr), r3(ks), r3(vs), r3(kw), r3(vw), sel, expand, ocmp, r3(ng))


def _mla_attn_kernel(q_ref, k_ref, v_ref, o_ref, s_ref, mx_ref, mb_ref, ls_ref, acc_ref):
    i = pl.program_id(1)
    c = (MLA_NOPE + MLA_ROPE) ** -0.5 * LOG2E
    row, col = _tile_iotas()
    mx_ref[...] = jnp.full(mx_ref.shape, NEG, F32)
    ls_ref[...] = jnp.zeros(ls_ref.shape, F32)
    acc_ref[...] = jnp.zeros(acc_ref.shape, F32)

    def scores(kb, causal):
        k0 = pl.multiple_of(kb * TK, TK)
        for h in range(N_HEADS):
            hs = slice(h * LANES, (h + 1) * LANES)
            s = _dot_nt(q_ref[:, hs], k_ref[pl.ds(k0, TK), hs])
            if causal:
                s = jnp.where(col <= row, s, NEG)
            _scores_put(h, kb, s, s_ref, mx_ref)

    def off_diagonal(kb, _):
        scores(kb, False)
        return 0

    lax.fori_loop(0, i, off_diagonal, 0)
    scores(i, True)
    _row_max(mx_ref, mb_ref)

    def probs(kb, _):
        k0 = pl.multiple_of(kb * TK, TK)
        for h in range(N_HEADS):
            v_blk = v_ref[pl.ds(k0, TK), (h // 2) * LANES:(h // 2 + 1) * LANES]
            _probs_accumulate(h, kb, c, v_blk, s_ref, mb_ref, ls_ref, acc_ref)
        return 0

    lax.fori_loop(0, i + 1, probs, 0)
    lane = lax.broadcasted_iota(jnp.int32, (TQ, LANES), 1)
    for pair in range(2):
        o_ref[:, pair * LANES:(pair + 1) * LANES] = jnp.where(
            lane < HEAD_DIM, _softmax_out(2 * pair, ls_ref, acc_ref),
            _softmax_out(2 * pair + 1, ls_ref, acc_ref)).astype(BF16)


def _mla_attn(mq, mk, mv, batch, seq):
    r3 = lambda a: a.reshape(batch, seq, a.shape[-1])
    return pl.pallas_call(
        _mla_attn_kernel,
        grid=(batch, seq // TQ),
        in_specs=[pl.BlockSpec((None, TQ, 512), lambda b, i: (b, i, 0)),
                  pl.BlockSpec((None, seq, 512), lambda b, i: (b, 0, 0)),
                  pl.BlockSpec((None, seq, 256), lambda b, i: (b, 0, 0))],
        out_specs=pl.BlockSpec((None, TQ, 256), lambda b, i: (b, i, 0)),
        out_shape=jax.ShapeDtypeStruct((batch, seq, 256), BF16),
        scratch_shapes=_softmax_scratch(seq // TK),
        compiler_params=_cparams(("parallel", "parallel")),
        name="mla_attention",
    )(r3(mq), r3(mk), r3(mv))


def _sb_attn_kernel(q_ref, k_ref, v_ref, tri_ref, o_ref, qm_ref, e_ref, tail_ref, acc_ref):
    i = pl.program_id(1)
    row, col = _tile_iotas()
    tail_ref[...] = jnp.zeros(tail_ref.shape, F32)
    acc_ref[...] = jnp.zeros(acc_ref.shape, F32)
    for pair in range(2):
        qp = q_ref[:, pair * LANES:(pair + 1) * LANES].astype(F32)
        for sub in range(2):
            qm_ref[2 * pair + sub] = _half_select(sub, qp).astype(BF16)

    def log_weights(kb, diagonal):
        k0 = pl.multiple_of(kb * TK, TK)
        tri2 = tri_ref[...]
        heads = range(N_HEADS)
        zs = [_dot_nt(qm_ref[h], k_ref[pl.ds(k0, TK), (h // 2) * LANES:(h // 2 + 1) * LANES]) for h in heads]
        lks = [jnp.minimum(z, 0.0) - jnp.log(1.0 + jnp.exp(-jnp.abs(z))) - z for z in zs]
        if diagonal:
            lks = [jnp.where(col < row, lk, 0.0) for lk in lks]
        incls = [_dot(jnp.concatenate(_split_bf16(lk), axis=1), tri2) for lk in lks]
        for h in heads:
            tail = tail_ref[h]
            e = zs[h] + incls[h] + jnp.concatenate([tail, tail], axis=1)
            if diagonal:
                e = jnp.where(col < row, e, NEG)
            e_ref[h, kb] = e
            tail_ref[h] = tail + jnp.broadcast_to(incls[h][:, 0:1], (TQ, LANES))

    log_weights(i, True)

    def off_diagonal(step, _):
        log_weights(i - 1 - step, False)
        return 0

    lax.fori_loop(0, i, off_diagonal, 0)

    def weighted_values(kb, _):
        k0 = pl.multiple_of(kb * TK, TK)
        for h in range(N_HEADS):
            ps = slice((h // 2) * LANES, (h // 2 + 1) * LANES)
            acc_ref[h] += _dot(jnp.exp(e_ref[h, kb]).astype(BF16), v_ref[pl.ds(k0, TK), ps])
        return 0

    lax.fori_loop(0, i + 1, weighted_values, 0)
    lane = lax.broadcasted_iota(jnp.int32, (TQ, LANES), 1)
    for pair in range(2):
        o_ref[:, pair * LANES:(pair + 1) * LANES] = jnp.where(
            lane < HEAD_DIM, acc_ref[2 * pair], acc_ref[2 * pair + 1]).astype(BF16)


def _sb_attn(sq, sk, sv, tri, batch, seq):
    r3 = lambda a: a.reshape(batch, seq, a.shape[-1])
    kspec = pl.BlockSpec((None, seq, 256), lambda b, i: (b, 0, 0))
    qspec = pl.BlockSpec((None, TQ, 256), lambda b, i: (b, i, 0))
    return pl.pallas_call(
        _sb_attn_kernel,
        grid=(batch, seq // TQ),
        in_specs=[qspec, kspec, kspec, _const_spec(tri.shape)],
        out_specs=qspec,
        out_shape=jax.ShapeDtypeStruct((batch, seq, 256), BF16),
        scratch_shapes=[pltpu.VMEM((N_HEADS, TQ, LANES), BF16), pltpu.VMEM((N_HEADS, seq // TK, TQ, TK), F32),
                        pltpu.VMEM((N_HEADS, TQ, LANES), F32), pltpu.VMEM((N_HEADS, TQ, LANES), F32)],
        compiler_params=_cparams(("parallel", "parallel")),
        name="stick_breaking_attention",
    )(r3(sq), r3(sk), r3(sv), tri)


def _merge_kernel(x_ref, ya_ref, yb_ref, yc_ref, yd_ref, wg_ref, wb_ref, wo_ref, g_ref, b_ref,
                  o_ref, ob_ref):
    x = x_ref[...]
    xb = x.astype(BF16)
    mixed = jnp.zeros((TM, D_MODEL), F32)
    for n, y_ref in enumerate((ya_ref, yb_ref, yc_ref, yd_ref)):
        gate = _sigmoid(_dot(xb, wg_ref[:, n * D_MODEL:(n + 1) * D_MODEL]))
        mixed = mixed + gate * _dot(y_ref[...], wb_ref[n])
    h = DEEPNORM_ALPHA * x + _dot(mixed.astype(BF16), wo_ref[...])
    out = _layer_norm(h, g_ref[...], b_ref[...])
    o_ref[...] = out
    ob_ref[...] = out.astype(BF16)


def _merge(x2d, ys, wts):
    n = x2d.shape[0]
    row = lambda w: pl.BlockSpec((TM, w), lambda i: (i, 0))
    consts = [wts['w_gate'], wts['w_branch'], wts['w_out'], wts['ln1_g'], wts['ln1_b']]
    return pl.pallas_call(
        _merge_kernel,
        grid=(n // TM,),
        in_specs=[row(D_MODEL)] + [row(BRANCH_W)] * 4 + [_const_spec(c.shape) for c in consts],
        out_specs=[row(D_MODEL), row(D_MODEL)],
        out_shape=[jax.ShapeDtypeStruct((n, D_MODEL), F32), jax.ShapeDtypeStruct((n, D_MODEL), BF16)],
        compiler_params=_cparams(("parallel",)),
        name="branch_merge_ln1",
    )(x2d, *ys, *consts)


def _ple_ln2(x1, x1b, f, p_ref, wpg_ref, wpp_ref, g_ref, b_ref):
    ple = _sigmoid(_dot(x1b, wpg_ref[...])) * _dot(p_ref[...].astype(BF16), wpp_ref[...])
    return _layer_norm(DEEPNORM_ALPHA * x1 + f + ple, g_ref[...], b_ref[...])


FF_CHUNK = 256


def _ffn_dense_kernel(x_ref, xb_ref, p_ref, wi_ref, wo_ref, wpg_ref, wpp_ref, g_ref, b_ref, o_ref, acc_ref):
    xb = xb_ref[...]
    for c in range(D_FF // FF_CHUNK):
        a = _dot(xb, wi_ref[:, c * FF_CHUNK:(c + 1) * FF_CHUNK])
        u = _dot(xb, wi_ref[:, D_FF + c * FF_CHUNK:D_FF + (c + 1) * FF_CHUNK])
        part = _dot((_silu(a) * u).astype(BF16), wo_ref[c * FF_CHUNK:(c + 1) * FF_CHUNK, :])
        if c == 0:
            acc_ref[...] = part
        else:
            acc_ref[...] += part
    o_ref[...] = _ple_ln2(x_ref[...], xb, acc_ref[...], p_ref, wpg_ref, wpp_ref, g_ref, b_ref)


def _ffn_dense(x1, x1b, p2d, wts):
    n = x1.shape[0]
    row = lambda w: pl.BlockSpec((TF, w), lambda i: (i, 0))
    consts = [wts['ffn_w_in'], wts['ffn_w_out'], wts['ple_w_gate'], wts['ple_w_proj'], wts['ln2_g'], wts['ln2_b']]
    return pl.pallas_call(
        _ffn_dense_kernel,
        grid=(n // TF,),
        in_specs=[row(D_MODEL), row(D_MODEL), row(P_DIM)] + [_const_spec(c.shape) for c in consts],
        out_specs=row(D_MODEL),
        out_shape=jax.ShapeDtypeStruct((n, D_MODEL), F32),
        scratch_shapes=[pltpu.VMEM((TF, D_MODEL), F32)],
        compiler_params=_cparams(("parallel",)),
        name="ffn_dense_ple_ln2",
    )(x1, x1b, p2d, *consts)


TR = 512
INFO_LANES = 6


def _moe_route_kernel(x_ref, wr_ref, tri_ref, info_ref, cnt_ref, run_ref):
    @pl.when(pl.program_id(0) == 0)
    def _():
        run_ref[...] = jnp.zeros_like(run_ref)

    xh, xl = _split_bf16(x_ref[...])
    wh, wl = _split_bf16(wr_ref[...])
    logits = _dot(xh, wh) + (_dot(xl, wh) + _dot(xh, wl))
    lane = lax.broadcasted_iota(jnp.int32, (TR, LANES), 1)
    lane_f = lane.astype(F32)
    logits = jnp.where(lane < N_EXPERTS, logits, NEG)
    m1 = jnp.max(logits, axis=-1, keepdims=True)
    i1 = jnp.min(jnp.where(logits == m1, lane_f, float(LANES)), axis=-1, keepdims=True)
    rest = jnp.where(lane_f == i1, NEG, logits)
    m2 = jnp.max(rest, axis=-1, keepdims=True)
    i2 = jnp.min(jnp.where(rest == m2, lane_f, float(LANES)), axis=-1, keepdims=True)
    e = jnp.exp(m2 - m1)
    g1 = 1.0 / (1.0 + e)
    g2 = e / (1.0 + e)
    hot1 = lane_f == i1
    hot2 = lane_f == i2
    onehot = jnp.where(hot1 | hot2, 1.0, 0.0)
    before = _dot(tri_ref[...], onehot.astype(BF16)) + run_ref[0:1, :]
    r1 = jnp.sum(jnp.where(hot1, before, 0.0), axis=-1, keepdims=True)
    r2 = jnp.sum(jnp.where(hot2, before, 0.0), axis=-1, keepdims=True)
    run_ref[0:1, :] = run_ref[0:1, :] + jnp.sum(onehot, axis=0, keepdims=True)
    info = jnp.zeros((TR, LANES), F32)
    for k, val in enumerate((i1, i2, r1, r2, g1, g2)):
        info = jnp.where(lane == k, val, info)
    info_ref[...] = info
    cnt_ref[...] = jnp.broadcast_to(run_ref[0:1, :], cnt_ref.shape)


def _moe_route(x1, w_router_pad, tri):
    n = x1.shape[0]
    return pl.pallas_call(
        _moe_route_kernel,
        grid=(n // TR,),
        in_specs=[pl.BlockSpec((TR, D_MODEL), lambda i: (i, 0)), _const_spec(w_router_pad.shape),
                  _const_spec(tri.shape)],
        out_specs=[pl.BlockSpec((TR, LANES), lambda i: (i, 0)), pl.BlockSpec((8, LANES), lambda i: (0, 0))],
        out_shape=[jax.ShapeDtypeStruct((n, LANES), F32), jax.ShapeDtypeStruct((8, LANES), F32)],
        scratch_shapes=[pltpu.VMEM((8, LANES), F32)],
        compiler_params=_cparams(("arbitrary",)),
        name="moe_router_rank",
    )(x1, w_router_pad, tri)


def _moe_rowmap_kernel(d1_ref, d2_ref, rt_ref):
    def clear(r, _):
        rt_ref[r] = 0
        return 0

    lax.fori_loop(0, rt_ref.shape[0], clear, 0, unroll=16)

    def place(t, _):
        rt_ref[d1_ref[t]] = t
        rt_ref[d2_ref[t]] = t
        return 0

    lax.fori_loop(0, d1_ref.shape[0], place, 0, unroll=8)


def _moe_rowmap(dest1, dest2, n_rows):
    smem = pl.BlockSpec(memory_space=pltpu.SMEM)
    return pl.pallas_call(
        _moe_rowmap_kernel,
        in_specs=[smem, smem],
        out_specs=smem,
        out_shape=jax.ShapeDtypeStruct((n_rows,), jnp.int32),
        name="moe_row_map",
    )(dest1, dest2)


EF_CHUNK = 1792
EF_STEPS = D_FF_EXPERT // EF_CHUNK
assert EF_STEPS >= 2
ROWS_PER_STEP = MOE_BLOCK // EF_STEPS


def _moe_ffn_kernel(be_ref, na_ref, rt_ref, x_hbm, wa_ref, wu_ref, wo_ref, ys_ref, xs_ref, acc_ref, sem):
    del be_ref
    blk, c = pl.program_id(0), pl.program_id(1)
    n_live = na_ref[0]
    last_step = pl.num_programs(1) - 1
    slot = blk % 2

    def row_copy(block, r, s):
        return pltpu.make_async_copy(x_hbm.at[pl.ds(rt_ref[block * MOE_BLOCK + r], 1), :],
                                     xs_ref.at[s, pl.ds(r, 1), :], sem.at[s])

    def wait_block(s):
        pltpu.make_async_copy(x_hbm.at[pl.ds(0, MOE_BLOCK), :], xs_ref.at[s], sem.at[s]).wait()

    @pl.when(blk < n_live)
    def _():
        @pl.when((blk == 0) & (c == 0))
        def _():
            def body(r, _):
                row_copy(0, r, 0).start()
                return 0
            lax.fori_loop(0, MOE_BLOCK, body, 0)

        @pl.when(c == 0)
        def _():
            wait_block(slot)

        xb = xs_ref[slot].astype(BF16)
        h = (_silu(_dot(xb, wa_ref[...])) * _dot(xb, wu_ref[...])).astype(BF16)
        part = _dot(h, wo_ref[...])

        nxt = jnp.minimum(blk + 1, pl.num_programs(0) - 1)
        for r in range(ROWS_PER_STEP):
            row_copy(nxt, c * ROWS_PER_STEP + r, 1 - slot).start()

        @pl.when(c == 0)
        def _():
            acc_ref[...] = part

        @pl.when((c > 0) & (c < last_step))
        def _():
            acc_ref[...] += part

        @pl.when(c == last_step)
        def _():
            ys_ref[...] = acc_ref[...] + part

        @pl.when((c == last_step) & (blk == n_live - 1))
        def _():
            wait_block(1 - slot)

    @pl.when((blk >= n_live) & (c == last_step))
    def _():
        ys_ref[...] = jnp.zeros_like(ys_ref)


def _moe_ffn(x1, row_tok, blk_expert, n_active, w_in, w_out):
    n_rows = row_tok.shape[0]
    n_blk = n_rows // MOE_BLOCK
    n_ch = EF_STEPS
    live = lambda b, na: jnp.minimum(b, na[0] - 1)
    chunk = lambda b, c, na: jnp.where(b < na[0], c, n_ch - 1)
    return pl.pallas_call(
        _moe_ffn_kernel,
        grid_spec=pltpu.PrefetchScalarGridSpec(
            num_scalar_prefetch=3,
            grid=(n_blk, n_ch),
            in_specs=[
                pl.BlockSpec(memory_space=pl.ANY),
                pl.BlockSpec((None, D_MODEL, EF_CHUNK),
                             lambda b, c, be, na, rt: (be[live(b, na)], 0, chunk(b, c, na))),
                pl.BlockSpec((None, D_MODEL, EF_CHUNK),
                             lambda b, c, be, na, rt: (be[live(b, na)], 0, n_ch + chunk(b, c, na))),
                pl.BlockSpec((None, EF_CHUNK, D_MODEL),
                             lambda b, c, be, na, rt: (be[live(b, na)], chunk(b, c, na), 0)),
            ],
            out_specs=pl.BlockSpec((MOE_BLOCK, D_MODEL), lambda b, c, be, na, rt: (b, 0)),
            scratch_shapes=[pltpu.VMEM((2, MOE_BLOCK, D_MODEL), F32), pltpu.VMEM((MOE_BLOCK, D_MODEL), F32),
                            pltpu.SemaphoreType.DMA((2,))],
        ),
        out_shape=jax.ShapeDtypeStruct((n_rows, D_MODEL), F32),
        compiler_params=_cparams(("arbitrary", "arbitrary")),
        name="moe_expert_swiglu",
    )(blk_expert, n_active, row_tok, x1, w_in, w_in, w_out)


TC = 256


def _moe_combine_kernel(d1_ref, d2_ref, ys_hbm, x_ref, xb_ref, p_ref, info_ref, wpg_ref, wpp_ref, g_ref,
                        b_ref, o_ref, ya_ref, yb_ref, sem):
    i = pl.program_id(0)
    last = pl.num_programs(0) - 1
    slot = i % 2

    def copies(tile, r, s):
        t = tile * TC + r
        return (pltpu.make_async_copy(ys_hbm.at[pl.ds(d1_ref[t], 1), :], ya_ref.at[s, pl.ds(r, 1), :], sem.at[0, s]),
                pltpu.make_async_copy(ys_hbm.at[pl.ds(d2_ref[t], 1), :], yb_ref.at[s, pl.ds(r, 1), :], sem.at[1, s]))

    def wait_tile(s):
        pltpu.make_async_copy(ys_hbm.at[pl.ds(0, TC), :], ya_ref.at[s], sem.at[0, s]).wait()
        pltpu.make_async_copy(ys_hbm.at[pl.ds(0, TC), :], yb_ref.at[s], sem.at[1, s]).wait()

    @pl.when(i == 0)
    def _():
        def body(r, _):
            for cp in copies(0, r, 0):
                cp.start()
            return 0
        lax.fori_loop(0, TC, body, 0)

    wait_tile(slot)
    nxt = jnp.minimum(i + 1, last)
    for r in range(TC):
        for cp in copies(nxt, r, 1 - slot):
            cp.start()
    info = info_ref[...]
    f = info[:, 4:5] * ya_ref[slot] + info[:, 5:6] * yb_ref[slot]
    o_ref[...] = _ple_ln2(x_ref[...], xb_ref[...], f, p_ref, wpg_ref, wpp_ref, g_ref, b_ref)

    @pl.when(i == last)
    def _():
        wait_tile(1 - slot)


def _moe_combine(ys, dest1, dest2, x1, x1b, p2d, info, wts):
    n = x1.shape[0]
    row = lambda w: pl.BlockSpec((TC, w), lambda i, d1, d2: (i, 0))
    consts = [wts['ple_w_gate'], wts['ple_w_proj'], wts['ln2_g'], wts['ln2_b']]
    cspec = lambda c: pl.BlockSpec(c.shape, lambda i, d1, d2: (0,) * c.ndim, pipeline_mode=pl.Buffered(1))
    return pl.pallas_call(
        _moe_combine_kernel,
        grid_spec=pltpu.PrefetchScalarGridSpec(
            num_scalar_prefetch=2,
            grid=(n // TC,),
            in_specs=[pl.BlockSpec(memory_space=pl.ANY), row(D_MODEL), row(D_MODEL), row(P_DIM), row(LANES)]
            + [cspec(c) for c in consts],
            out_specs=row(D_MODEL),
            scratch_shapes=[pltpu.VMEM((2, TC, D_MODEL), F32), pltpu.VMEM((2, TC, D_MODEL), F32),
                            pltpu.SemaphoreType.DMA((2, 2))],
        ),
        out_shape=jax.ShapeDtypeStruct((n, D_MODEL), F32),
        compiler_params=_cparams(("arbitrary",)),
        name="moe_combine_ple_ln2",
    )(dest1, dest2, ys, x1, x1b, p2d, info, *consts)


def _moe(x1, x1b, p2d, wts):
    n = x1.shape[0]
    n_rows = ((n * 2 + MOE_BLOCK - 1) // MOE_BLOCK) * MOE_BLOCK + N_EXPERTS * MOE_BLOCK
    info, cnt = _moe_route(x1, wts['w_router'], wts['tri_tokens'])
    counts = cnt[0, :N_EXPERTS].astype(jnp.int32)
    padded = ((counts + MOE_BLOCK - 1) // MOE_BLOCK) * MOE_BLOCK
    ends = jnp.cumsum(padded)
    start_pad = ends - padded
    e1, e2 = info[:, 0].astype(jnp.int32), info[:, 1].astype(jnp.int32)
    dest1 = start_pad[e1] + info[:, 2].astype(jnp.int32)
    dest2 = start_pad[e2] + info[:, 3].astype(jnp.int32)
    n_blk = n_rows // MOE_BLOCK
    blk_row0 = jnp.arange(n_blk, dtype=jnp.int32) * MOE_BLOCK
    blk_expert = jnp.minimum(jnp.sum((ends[None, :] <= blk_row0[:, None]).astype(jnp.int32), axis=1),
                             N_EXPERTS - 1)
    n_active = (ends[-1:] // MOE_BLOCK).astype(jnp.int32)
    row_tok = _moe_rowmap(dest1, dest2, n_rows)
    ys = _moe_ffn(x1, row_tok, blk_expert, n_active, wts['moe_w_in'], wts['moe_w_out'])
    return _moe_combine(ys, dest1, dest2, x1, x1b, p2d, info, wts)


def _rot_half_cols(w, heads, dim):
    w3 = w.reshape(w.shape[0], heads, dim)
    half = dim // 2
    return jnp.concatenate([-w3[..., half:], w3[..., :half]], axis=-1).reshape(w.shape[0], heads * dim)


def _prep_layer(i, w_in, conv_w, conv_b, conv_ln_g, conv_ln_b, nsa_cmp_pe, nsa_cmp_w1, nsa_cmp_w2,
                mla_q_norm, mla_kv_norm, mla_w_uq, mla_w_ukv, w_branch, w_out, ln1_g, ln1_b,
                ple_w_gate, ple_w_proj, ln2_g, ln2_b):
    w = w_in[i]
    d = w.shape[0]
    z = lambda n: jnp.zeros((d, n), F32)
    dup = lambda a: jnp.concatenate([a, a], axis=1)
    c_glu, nq = w[:, 0:512], w[:, 512:768]
    nkv = w[:, 768:1152]
    k_cmp, v_cmp, k_slc, v_slc, k_win, v_win = [nkv[:, j * 64:(j + 1) * 64] for j in range(6)]
    ng = w[:, 1152:1164]
    mq, mkv, mkr = w[:, 1164:1420], w[:, 1420:1548], w[:, 1548:1580]
    sb = w[:, 1580:2348]
    bg = w[:, 2348:6444]
    cols = [c_glu, nq, _rot_half_cols(nq, NSA_HEADS, HEAD_DIM),
            dup(k_slc), dup(_rot_half_cols(k_slc, 1, HEAD_DIM)),
            dup(k_win), dup(_rot_half_cols(k_win, 1, HEAD_DIM)),
            dup(v_slc), dup(v_win), k_cmp, v_cmp, ng, z(LANES - 12), mq, mkv,
            z(64), mkr, z(32), z(64), _rot_half_cols(mkr, 1, MLA_ROPE), z(32), z(LANES), sb]
    w1 = jnp.concatenate(cols, axis=1).astype(BF16)
    assert w1.shape[1] == C_TOT

    inv32 = ROPE_THETA ** (-jnp.arange(HEAD_DIM // 2, dtype=F32) / (HEAD_DIM // 2))
    inv16 = ROPE_THETA ** (-jnp.arange(MLA_ROPE // 2, dtype=F32) / (MLA_ROPE // 2))
    inv_nsa = jnp.tile(inv32, 4)[None, :]
    inv_mla = jnp.concatenate([jnp.zeros((64,), F32), inv16, inv16, jnp.zeros((32,), F32)])[None, :]

    wuq = mla_w_uq[i].reshape(MLA_Q_RANK, MLA_HEADS, MLA_NOPE + MLA_ROPE)
    zq = jnp.zeros((MLA_Q_RANK, MLA_HEADS, 32), F32)
    wq = jnp.concatenate([wuq, zq], axis=-1).reshape(MLA_Q_RANK, MLA_HEADS * LANES)
    rope_rot = jnp.concatenate([-wuq[..., MLA_NOPE + 16:], wuq[..., MLA_NOPE:MLA_NOPE + 16]], axis=-1)
    wqr = jnp.concatenate([jnp.zeros((MLA_Q_RANK, MLA_HEADS, MLA_NOPE), F32), rope_rot, zq],
                          axis=-1).reshape(MLA_Q_RANK, MLA_HEADS * LANES)
    wukv = mla_w_ukv[i].reshape(MLA_KV_RANK, MLA_HEADS, MLA_NOPE + MLA_V)
    wk = jnp.concatenate([wukv[..., :MLA_NOPE], jnp.zeros((MLA_KV_RANK, MLA_HEADS, 64), F32)],
                         axis=-1).reshape(MLA_KV_RANK, MLA_HEADS * LANES)
    wv = wukv[..., MLA_NOPE:].reshape(MLA_KV_RANK, MLA_HEADS * MLA_V)

    pe = nsa_cmp_pe[i]
    pe_rows = pe.reshape(CMP_BLOCK, 2 * HEAD_DIM)
    pe_a = pe_rows[:CMP_STRIDE].reshape(1, CMP_STRIDE * LANES)
    pe_b = pe_rows[CMP_STRIDE:].reshape(1, CMP_STRIDE * LANES)
    w1c = nsa_cmp_w1[i].reshape(2, CMP_BLOCK, HEAD_DIM, HEAD_DIM)
    zblk = jnp.zeros((CMP_BLOCK, HEAD_DIM, HEAD_DIM), F32)
    w1full = jnp.concatenate([jnp.concatenate([w1c[0], zblk], axis=2),
                              jnp.concatenate([zblk, w1c[1]], axis=2)], axis=1)
    w1a = w1full[:CMP_STRIDE].reshape(CMP_STRIDE * LANES, LANES).astype(BF16)
    w1b = w1full[CMP_STRIDE:].reshape(CMP_STRIDE * LANES, LANES).astype(BF16)
    w2 = nsa_cmp_w2[i]
    z64 = jnp.zeros((HEAD_DIM, LANES), F32)
    w2k = jnp.concatenate([dup(w2[0]), z64], axis=0).astype(BF16)
    w2v = jnp.concatenate([z64, dup(w2[1])], axis=0).astype(BF16)

    return dict(
        w1=w1, inv_nsa=inv_nsa, inv_mla=inv_mla,
        mla_qn=mla_q_norm[i][None, :], mla_kvn=mla_kv_norm[i][None, :],
        wq=wq.astype(BF16), wqr=wqr.astype(BF16), wk=wk.astype(BF16), wv=wv.astype(BF16),
        conv_w=conv_w[i], conv_b=conv_b[i][None, :], conv_g=conv_ln_g[i][None, :], conv_beta=conv_ln_b[i][None, :],
        pe_a=pe_a, pe_b=pe_b, w1a=w1a, w1b=w1b, w2k=w2k, w2v=w2v,
        w_gate=bg.astype(BF16), w_branch=w_branch[i].astype(BF16), w_out=w_out[i].astype(BF16),
        ln1_g=ln1_g[i][None, :], ln1_b=ln1_b[i][None, :],
        ple_w_gate=ple_w_gate[i].astype(BF16), ple_w_proj=ple_w_proj[i].astype(BF16),
        ln2_g=ln2_g[i][None, :], ln2_b=ln2_b[i][None, :],
    )


def _tables(seq):
    n_cmp_rows = seq // CMP_STRIDE
    n_sel = seq // SEL_BLOCK
    cmp_start = jnp.arange(n_cmp_rows) * CMP_STRIDE
    sel_start = jnp.arange(LANES) * SEL_BLOCK
    n_cmp = (seq - CMP_BLOCK) // CMP_STRIDE + 1
    overlap = ((cmp_start[:, None] < sel_start[None, :] + SEL_BLOCK)
               & (cmp_start[:, None] + CMP_BLOCK > sel_start[None, :])
               & (jnp.arange(n_cmp_rows)[:, None] < n_cmp) & (jnp.arange(LANES)[None, :] < n_sel))
    kb = jnp.arange(seq // TK)[:, None, None]
    nn = jnp.arange(LANES)[None, :, None]
    ll = jnp.arange(TK)[None, None, :]
    expand = (kb * TK + ll) // SEL_BLOCK == nn
    jj = jnp.arange(TK)
    tri_keys = jnp.tile(jj[:, None] >= jj[None, :], (2, 1))
    tt = jnp.arange(TR)
    tri_tokens = tt[None, :] < tt[:, None]
    return dict(overlap_t=overlap.T.astype(BF16), eye_q=jnp.eye(TQ, dtype=BF16),
                expand=expand.astype(BF16), tri_keys=tri_keys.astype(BF16),
                tri_tokens=tri_tokens.astype(BF16))


def kernel(x, p, positions, w_in, conv_w, conv_b, conv_ln_g, conv_ln_b, nsa_cmp_pe, nsa_cmp_w1, nsa_cmp_w2,
           mla_q_norm, mla_kv_norm, mla_w_uq, mla_w_ukv, w_branch, w_out, ln1_g, ln1_b, ffn_w_in, ffn_w_out,
           moe_router, moe_w_in, moe_w_out, ple_w_gate, ple_w_proj, ln2_g, ln2_b):
    batch, seq, _ = x.shape
    n = batch * seq
    tabs = _tables(seq)
    x2d = x.reshape(n, D_MODEL)
    pos2d = positions.reshape(n, 1)
    for i in range(DEPTH):
        wts = _prep_layer(i, w_in, conv_w, conv_b, conv_ln_g, conv_ln_b, nsa_cmp_pe, nsa_cmp_w1, nsa_cmp_w2,
                          mla_q_norm, mla_kv_norm, mla_w_uq, mla_w_ukv, w_branch, w_out, ln1_g, ln1_b,
                          ple_w_gate, ple_w_proj, ln2_g, ln2_b)
        wts['overlap_t'], wts['eye_q'] = tabs['overlap_t'], tabs['eye_q']
        (conv_in, nq, nqr, ks, kw, vs, vw, kvc, ng, mq, mk, mv, sq, sk, sv) = _mixer_in(x2d, pos2d, wts)
        y_a = _conv(conv_in, wts['conv_w'], wts['conv_b'], wts['conv_g'], wts['conv_beta'], batch, seq)
        ocmp, sel = _nsa_cmp(kvc, nq, wts, batch, seq)
        y_b = _nsa_attn(nqr, ks, vs, kw, vw, sel, tabs['expand'], ocmp, ng, batch, seq)
        y_c = _mla_attn(mq, mk, mv, batch, seq)
        y_d = _sb_attn(sq, sk, sv, tabs['tri_keys'], batch, seq)
        ys = [y.reshape(n, BRANCH_W) for y in (y_a, y_b, y_c, y_d)]
        x1, x1b = _merge(x2d, ys, wts)
        p2d = p[i].reshape(n, P_DIM)
        if i % 2 == 0:
            wts['ffn_w_in'] = ffn_w_in[i // 2].astype(BF16)
            wts['ffn_w_out'] = ffn_w_out[i // 2].astype(BF16)
            x2d = _ffn_dense(x1, x1b, p2d, wts)
        else:
            wts['w_router'] = jnp.concatenate(
                [moe_router[i // 2], jnp.zeros((D_MODEL, LANES - N_EXPERTS), F32)], axis=1)
            wts['tri_tokens'] = tabs['tri_tokens']
            wts['moe_w_in'] = moe_w_in[i // 2].astype(BF16)
            wts['moe_w_out'] = moe_w_out[i // 2].astype(BF16)
            x2d = _moe(x1, x1b, p2d, wts)
    return x2d.reshape(batch, seq, D_MODEL)
```

```python
import functools

import jax
import jax.numpy as jnp
from jax import lax
from jax.experimental import pallas as pl
from jax.experimental.pallas import tpu as pltpu

F32 = jnp.float32
BF16 = jnp.bfloat16

D_MODEL = 1024
DEPTH = 2
CONV_CH = 256
CONV_WIDTH = 31
NSA_HEADS = 4
HEAD_DIM = 64
CMP_BLOCK = 32
CMP_STRIDE = 16
SEL_BLOCK = 64
SEL_TOPN = 16
WINDOW = 512
MLA_HEADS = 4
MLA_Q_RANK = 256
MLA_KV_RANK = 128
MLA_NOPE = 64
MLA_ROPE = 32
MLA_V = 64
N_BRANCH = 4
BRANCH_W = 256
ROPE_THETA = 10000.0
LN_EPS = 1e-5
RMS_EPS = 1e-6
D_FF = 2816
N_EXPERTS = 8
D_FF_EXPERT = 3584
MOE_BLOCK = 512
P_DIM = 256
DEEPNORM_ALPHA = (2 * DEPTH) ** 0.25

LANES = 128
SUBLANES = 8
VMEM_LIMIT = 56 * 1024 * 1024

NEG = -1e30

C_CONV = 0
C_NQ = 512
C_NQR = 768
C_KS = 1024
C_KW = 1280
C_VS = 1536
C_KVC = 1792
C_MQ = 2048
C_MKV = 2304
C_MKRR = 2560
C_SB = 2816
C_TOT = 3584

TM = 256
TF = 512
TQ = 256
TK = 256


def _cparams(sem, vmem=VMEM_LIMIT):
    return pltpu.CompilerParams(dimension_semantics=sem, vmem_limit_bytes=vmem)


def _const_spec(shape):
    nd = len(shape)
    return pl.BlockSpec(shape, lambda *_: (0,) * nd, pipeline_mode=pl.Buffered(1))


def _dot(a, b):
    return jnp.dot(a, b, preferred_element_type=F32)


def _dot_nt(a, b):
    return lax.dot_general(a, b, (((1,), (1,)), ((), ())), preferred_element_type=F32)


def _layer_norm(h, g, b):
    mu = jnp.mean(h, axis=-1, keepdims=True)
    d = h - mu
    var = jnp.mean(d * d, axis=-1, keepdims=True)
    return d * lax.rsqrt(var + LN_EPS) * g + b


def _rms_norm(h, g):
    return h * lax.rsqrt(jnp.mean(h * h, axis=-1, keepdims=True) + RMS_EPS) * g


def _sigmoid(x):
    return 1.0 / (1.0 + jnp.exp(-x))


def _silu(x):
    return x * _sigmoid(x)


def _split_bf16(x):
    hi = x.astype(BF16)
    lo = (x - hi.astype(F32)).astype(BF16)
    return hi, lo


def _half_select(sub, x):
    lane = lax.broadcasted_iota(jnp.int32, x.shape, 1)
    keep = (lane < HEAD_DIM) if sub == 0 else (lane >= HEAD_DIM)
    return jnp.where(keep, x, 0.0)


def _mixer_in_kernel(x_ref, pos_ref, w_ref, invn_ref, invm_ref, qn_ref, kvn_ref, wq_ref, wqr_ref,
                     wk_ref, wv_ref,
                     conv_ref, nq_ref, nqr_ref, ks_ref, kw_ref, vs_ref, vw_ref, kvc_ref, ng_ref,
                     mq_ref, mk_ref, mv_ref, sq_ref, sk_ref, sv_ref):
    xb = x_ref[...].astype(BF16)

    def proj(c0, width):
        return _dot(xb, w_ref[:, c0:c0 + width])

    posf = pos_ref[...].astype(F32)
    ang_n = posf * invn_ref[...]
    cos_n, sin_n = jnp.cos(ang_n), jnp.sin(ang_n)
    ang_m = posf * invm_ref[...]
    cos_m, sin_m = jnp.cos(ang_m), jnp.sin(ang_m)

    lo, hi = slice(0, LANES), slice(LANES, 2 * LANES)
    qn = _rms_norm(proj(C_MQ, MLA_Q_RANK), qn_ref[...]).astype(BF16)
    mkv = proj(C_MKV, 256)
    kvn = _rms_norm(mkv[:, lo], kvn_ref[...]).astype(BF16)

    conv_ref[...] = proj(C_CONV, 512)

    scale = HEAD_DIM ** -0.5
    cos_n2 = jnp.concatenate([cos_n, cos_n], axis=1)
    sin_n2 = jnp.concatenate([sin_n, sin_n], axis=1)
    q, qrot = proj(C_NQ, 256), proj(C_NQR, 256)
    nq_ref[...] = (q * scale).astype(BF16)
    nqr_ref[...] = ((q * cos_n2 + qrot * sin_n2) * scale).astype(BF16)
    k = proj(C_KS, 256)
    ks_ref[...] = (k[:, lo] * cos_n + k[:, hi] * sin_n).astype(BF16)
    k = proj(C_KW, 256)
    kw_ref[...] = (k[:, lo] * cos_n + k[:, hi] * sin_n).astype(BF16)
    v = proj(C_VS, 256)
    vs_ref[...] = v[:, lo].astype(BF16)
    vw_ref[...] = v[:, hi].astype(BF16)
    u = proj(C_KVC, 256)
    kvc_ref[...] = u[:, lo]
    ng_ref[...] = u[:, hi]

    kr = mkv[:, hi] * cos_m + proj(C_MKRR, 256)[:, lo] * sin_m
    qa, qr, kk = _dot(qn, wq_ref[...]), _dot(qn, wqr_ref[...]), _dot(kvn, wk_ref[...])
    for h in range(MLA_HEADS):
        sl = slice(h * LANES, (h + 1) * LANES)
        mq_ref[:, sl] = (qa[:, sl] * cos_m + qr[:, sl] * sin_m).astype(BF16)
        mk_ref[:, sl] = (kk[:, sl] + kr).astype(BF16)
    mv_ref[...] = _dot(kvn, wv_ref[...]).astype(BF16)

    sq_ref[...] = (proj(C_SB, 256) * scale).astype(BF16)
    sk_ref[...] = proj(C_SB + 256, 256).astype(BF16)
    sv_ref[...] = proj(C_SB + 512, 256).astype(BF16)


def _mixer_in(x2d, pos2d, wts):
    n = x2d.shape[0]
    row = lambda w: pl.BlockSpec((TF, w), lambda i: (i, 0))
    out_widths = [512, 256, 256, 128, 128, 128, 128, 128, 128, 512, 512, 256, 256, 256, 256]
    out_dtypes = [F32, BF16, BF16, BF16, BF16, BF16, BF16, F32, F32, BF16, BF16, BF16, BF16, BF16, BF16]
    consts = [wts['w1'], wts['inv_nsa'], wts['inv_mla'], wts['mla_qn'], wts['mla_kvn'], wts['wq'],
              wts['wqr'], wts['wk'], wts['wv']]
    return pl.pallas_call(
        _mixer_in_kernel,
        grid=(n // TF,),
        in_specs=[row(D_MODEL), row(1)] + [_const_spec(c.shape) for c in consts],
        out_specs=[row(w) for w in out_widths],
        out_shape=[jax.ShapeDtypeStruct((n, w), d) for w, d in zip(out_widths, out_dtypes)],
        compiler_params=_cparams(("parallel",)),
        name="mixer_in",
    )(x2d, pos2d, *consts)


CONV_PAD = 32
CONV_CHUNK = 128


def _conv_kernel(u_ref, w_ref, b_ref, g_ref, beta_ref, o_ref, hp_ref):
    seq = u_ref.shape[0]
    hp_ref[0, 0:CONV_PAD, :] = jnp.zeros((CONV_PAD, CONV_CH), F32)
    hp_ref[0, CONV_PAD:CONV_PAD + seq, :] = u_ref[:, 0:CONV_CH] * _sigmoid(u_ref[:, CONV_CH:2 * CONV_CH])
    rows = seq + CONV_PAD - SUBLANES
    for s in range(1, SUBLANES):
        hp_ref[s, 0:rows, :] = hp_ref[0, s:s + rows, :]
    first = CONV_PAD - (CONV_WIDTH - 1)
    for c in range(seq // CONV_CHUNK):
        base = c * CONV_CHUNK
        acc = jnp.broadcast_to(b_ref[...], (CONV_CHUNK, CONV_CH))
        for j in range(CONV_WIDTH):
            a, s = divmod(first + j, SUBLANES)
            r0 = base + a * SUBLANES
            acc = acc + hp_ref[s, r0:r0 + CONV_CHUNK, :] * w_ref[j:j + 1, :]
        y = _layer_norm(acc, g_ref[...], beta_ref[...])
        o_ref[base:base + CONV_CHUNK, :] = _silu(y).astype(BF16)


def _conv(conv_in, w, b, g, beta, batch, seq):
    return pl.pallas_call(
        _conv_kernel,
        grid=(batch,),
        in_specs=[pl.BlockSpec((None, seq, 2 * CONV_CH), lambda i: (i, 0, 0)),
                  _const_spec(w.shape), _const_spec(b.shape), _const_spec(g.shape), _const_spec(beta.shape)],
        out_specs=pl.BlockSpec((None, seq, CONV_CH), lambda i: (i, 0, 0)),
        out_shape=jax.ShapeDtypeStruct((batch, seq, CONV_CH), BF16),
        scratch_shapes=[pltpu.VMEM((SUBLANES, CONV_PAD + seq, CONV_CH), F32)],
        compiler_params=_cparams(("parallel",)),
        name="conformer_conv",
    )(conv_in.reshape(batch, seq, 2 * CONV_CH), w, b, g, beta)


def _gelu_tanh(x):
    return 0.5 * x * (1.0 + jnp.tanh(0.7978845608028654 * (x + 0.044715 * x * x * x)))


SEL_SHIFT = 6


def _nsa_cmp_kernel(kvc_ref, q_ref, pea_ref, peb_ref, w1a_ref, w1b_ref, w2k_ref, w2v_ref, ovt_ref, eye_ref,
                    ocmp_ref, sel_ref, *, seq):
    n_cmp = (seq - CMP_BLOCK) // CMP_STRIDE + 1
    n_sel = seq // SEL_BLOCK
    nb = seq // CMP_STRIDE
    x2 = kvc_ref[...]
    xa = (x2 + pea_ref[...]).astype(BF16)
    xb = (x2 + peb_ref[...]).astype(BF16)
    ha = _dot(xa, w1a_ref[...])
    hb = _dot(xb, w1b_ref[...])
    hid = ha + pltpu.roll(hb, nb - 1, 0)
    hid = _gelu_tanh(hid).astype(BF16)
    kk = _dot(hid, w2k_ref[...]).astype(BF16)
    vv = _dot(hid, w2v_ref[...]).astype(BF16)
    ovt = ovt_ref[...]

    for c in range(seq // TQ):
        r0 = c * TQ
        t = r0 + lax.broadcasted_iota(jnp.int32, (TQ, LANES), 0)
        j = lax.broadcasted_iota(jnp.int32, (TQ, LANES), 1)
        valid = (j * CMP_STRIDE + CMP_BLOCK - 1 <= t) & (j < n_cmp)
        psum = jnp.zeros((TQ, LANES), F32)
        for pair in range(2):
            qp = q_ref[r0:r0 + TQ, pair * LANES:(pair + 1) * LANES].astype(F32)
            outs = []
            for sub in range(2):
                qm = _half_select(sub, qp).astype(BF16)
                s = jnp.where(valid, _dot_nt(qm, kk), NEG)
                m = jnp.max(s, axis=-1, keepdims=True)
                e = jnp.where(valid, jnp.exp(s - m), 0.0)
                den = jnp.sum(e, axis=-1, keepdims=True)
                p = e / jnp.where(den > 0, den, 1.0)
                psum = psum + p
                outs.append(_dot(p.astype(BF16), vv))
            ocmp_ref[r0:r0 + TQ, pair * LANES:(pair + 1) * LANES] = jnp.where(
                lax.broadcasted_iota(jnp.int32, (TQ, LANES), 1) < HEAD_DIM, outs[0], outs[1])
        p_hi, p_lo = _split_bf16(psum)
        imp = (_dot_nt(ovt, p_hi) + _dot_nt(ovt, p_lo))[0:n_sel, :]
        n = lax.broadcasted_iota(jnp.int32, (n_sel, TQ), 0)
        cur = jnp.right_shift(r0 + lax.broadcasted_iota(jnp.int32, (n_sel, TQ), 1), SEL_SHIFT)
        forced = (n == 0) | (n == cur) | (n == cur - 1)
        imp = jnp.where(forced, jnp.inf, imp)
        imp = jnp.where(n > cur, -jnp.inf, imp)
        rank = jnp.zeros((n_sel, TQ), F32)
        for n2 in range(n_sel):
            other = imp[n2:n2 + 1, :]
            ahead = (other > imp) | ((other == imp) & (n2 < n))
            rank = rank + jnp.where(ahead, 1.0, 0.0)
        sel_t = jnp.where((rank < SEL_TOPN) & (imp > -jnp.inf), 1.0, 0.0)
        sel_t = jnp.concatenate([sel_t, jnp.zeros((LANES - n_sel, TQ), F32)], axis=0).astype(BF16)
        sel_ref[r0:r0 + TQ, :] = _dot_nt(eye_ref[...], sel_t).astype(BF16)


def _nsa_cmp(kvc, nq, wts, batch, seq):
    nb = seq // CMP_STRIDE
    consts = [wts['pe_a'], wts['pe_b'], wts['w1a'], wts['w1b'], wts['w2k'], wts['w2v'], wts['overlap_t'],
              wts['eye_q']]
    return pl.pallas_call(
        functools.partial(_nsa_cmp_kernel, seq=seq),
        grid=(batch,),
        in_specs=[pl.BlockSpec((None, nb, CMP_STRIDE * LANES), lambda i: (i, 0, 0)),
                  pl.BlockSpec((None, seq, 256), lambda i: (i, 0, 0))] + [_const_spec(c.shape) for c in consts],
        out_specs=[pl.BlockSpec((None, seq, 256), lambda i: (i, 0, 0)),
                   pl.BlockSpec((None, seq, LANES), lambda i: (i, 0, 0))],
        out_shape=[jax.ShapeDtypeStruct((batch, seq, 256), F32),
                   jax.ShapeDtypeStruct((batch, seq, LANES), BF16)],
        compiler_params=_cparams(("parallel",)),
        name="nsa_compress_select",
    )(kvc.reshape(batch, nb, CMP_STRIDE * LANES), nq.reshape(batch, seq, 256), *consts)


LOG2E = 1.4426950408889634
N_HEADS = 4


def _softmax_scratch(n_tiles):
    slab = pltpu.VMEM((N_HEADS, TQ, LANES), F32)
    return [pltpu.VMEM((N_HEADS, n_tiles, TQ, TK), F32), slab, slab, slab, slab]


def _scores_put(h, t, s, s_ref, mx_ref, first):
    s_ref[h, t] = s
    m = jnp.maximum(s[:, :LANES], s[:, LANES:])
    mx_ref[h] = m if first else jnp.maximum(mx_ref[h], m)


def _row_max(mx_ref, mb_ref):
    for h in range(N_HEADS):
        mb_ref[h] = jnp.broadcast_to(jnp.max(mx_ref[h], axis=-1, keepdims=True), (TQ, LANES))


def _probs_accumulate(h, t, c, v_blk, s_ref, mb_ref, ls_ref, acc_ref, first):
    s, mb = s_ref[h, t], mb_ref[h]
    pa = jnp.exp2((s[:, :LANES] - mb) * c)
    pb = jnp.exp2((s[:, LANES:] - mb) * c)
    pv = _dot(jnp.concatenate([pa, pb], axis=1).astype(BF16), v_blk)
    if first:
        ls_ref[h] = pa + pb
        acc_ref[h] = pv
    else:
        ls_ref[h] += pa + pb
        acc_ref[h] += pv


def _lane_sum_dense(x):
    ones = jnp.ones((2 * LANES, LANES), BF16)
    return _dot(jnp.concatenate(_split_bf16(x), axis=1), ones)


def _softmax_out(h, ls_ref, acc_ref):
    return acc_ref[h] / _lane_sum_dense(ls_ref[h])


def _tile_iotas():
    return (lax.broadcasted_iota(jnp.int32, (TQ, TK), 0), lax.broadcasted_iota(jnp.int32, (TQ, TK), 1))


def _nsa_attn_kernel(q_ref, ks_ref, vs_ref, kw_ref, vw_ref, sel_ref, exp_ref, ocmp_ref, ng_ref, gate_ref, o_ref,
                     qm_ref, ss_ref, mxs_ref, mbs_ref, lss_ref, accs_ref,
                     sw_ref, mxw_ref, mbw_ref, lsw_ref, accw_ref):
    i = pl.program_id(1)
    row, col = _tile_iotas()
    sel = sel_ref[...]
    for pair in range(2):
        qp = q_ref[:, pair * LANES:(pair + 1) * LANES].astype(F32)
        for sub in range(2):
            qm_ref[2 * pair + sub] = _half_select(sub, qp).astype(BF16)

    def selected_scores(kb, diagonal):
        k0 = pl.multiple_of(kb * TK, TK)
        hit = _dot(sel, exp_ref[kb]) > 0.5
        if diagonal:
            hit = hit & (col <= row)
        bias = jnp.where(hit, 0.0, NEG)
        k_blk = ks_ref[pl.ds(k0, TK), :]
        for h in range(N_HEADS):
            _scores_put(h, kb, _dot_nt(qm_ref[h], k_blk) + bias, ss_ref, mxs_ref, diagonal)

    def window_scores(slot, mask):
        k0 = pl.multiple_of((i - 2 + slot) * TK, TK)
        k_blk = kw_ref[pl.ds(k0, TK), :]
        for h in range(N_HEADS):
            s = _dot_nt(qm_ref[h], k_blk)
            if mask is not None:
                s = jnp.where(mask, s, NEG)
            _scores_put(h, slot, s, sw_ref, mxw_ref, slot == 2)

    def off_diagonal(kb, _):
        selected_scores(kb, False)
        return 0

    selected_scores(i, True)
    window_scores(2, col <= row)
    lax.fori_loop(0, i, off_diagonal, 0)
    pl.when(i >= 2)(lambda: window_scores(0, col > row))
    pl.when(i >= 1)(lambda: window_scores(1, None))
    _row_max(mxs_ref, mbs_ref)
    _row_max(mxw_ref, mbw_ref)

    def selected_probs(kb, first):
        v_blk = vs_ref[pl.ds(pl.multiple_of(kb * TK, TK), TK), :]
        for h in range(N_HEADS):
            _probs_accumulate(h, kb, LOG2E, v_blk, ss_ref, mbs_ref, lss_ref, accs_ref, first)

    def window_probs(slot):
        v_blk = vw_ref[pl.ds(pl.multiple_of((i - 2 + slot) * TK, TK), TK), :]
        for h in range(N_HEADS):
            _probs_accumulate(h, slot, LOG2E, v_blk, sw_ref, mbw_ref, lsw_ref, accw_ref, slot == 2)

    def off_diagonal_probs(kb, _):
        selected_probs(kb, False)
        return 0

    selected_probs(i, True)
    window_probs(2)
    lax.fori_loop(0, i, off_diagonal_probs, 0)
    pl.when(i >= 2)(lambda: window_probs(0))
    pl.when(i >= 1)(lambda: window_probs(1))

    g = _sigmoid(ng_ref[...])
    gx = _dot(jnp.concatenate(_split_bf16(g), axis=1), gate_ref[...])
    lane = lax.broadcasted_iota(jnp.int32, (TQ, LANES), 1)

    def heads_out(ls_ref, acc_ref):
        return jnp.concatenate(
            [jnp.where(lane < HEAD_DIM, _softmax_out(2 * pair, ls_ref, acc_ref),
                       _softmax_out(2 * pair + 1, ls_ref, acc_ref)) for pair in range(2)], axis=1)

    width = N_HEADS * HEAD_DIM
    o_ref[...] = (gx[:, 0:width] * ocmp_ref[...] + gx[:, width:2 * width] * heads_out(lss_ref, accs_ref)
                  + gx[:, 2 * width:3 * width] * heads_out(lsw_ref, accw_ref)).astype(BF16)


def _nsa_attn(nqr, ks, vs, kw, vw, sel, expand, ocmp, ng, gate_expand, batch, seq):
    qspec = lambda w: pl.BlockSpec((None, TQ, w), lambda b, i: (b, i, 0))
    kspec = pl.BlockSpec((None, seq, LANES), lambda b, i: (b, 0, 0))
    r3 = lambda a: a.reshape(batch, seq, a.shape[-1])
    return pl.pallas_call(
        _nsa_attn_kernel,
        grid=(batch, seq // TQ),
        in_specs=[qspec(256), kspec, kspec, kspec, kspec, qspec(LANES), _const_spec(expand.shape),
                  qspec(256), qspec(LANES), _const_spec(gate_expand.shape)],
        out_specs=qspec(256),
        out_shape=jax.ShapeDtypeStruct((batch, seq, 256), BF16),
        scratch_shapes=[pltpu.VMEM((N_HEADS, TQ, LANES), BF16)] + _softmax_scratch(seq // TK)
        + _softmax_scratch(WINDOW // TK + 1),
        compiler_params=_cparams(("parallel", "parallel")),
        name="nsa_select_window",
    )(r3(nqr), r3(ks), r3(vs), r3(kw), r3(vw), sel, expand, ocmp, r3(ng), gate_expand)


def _mla_attn_kernel(q_ref, k_ref, v_ref, o_ref, s_ref, mx_ref, mb_ref, ls_ref, acc_ref):
    i = pl.program_id(1)
    c = (MLA_NOPE + MLA_ROPE) ** -0.5 * LOG2E
    row, col = _tile_iotas()

    def scores(kb, diagonal):
        k0 = pl.multiple_of(kb * TK, TK)
        for h in range(N_HEADS):
            hs = slice(h * LANES, (h + 1) * LANES)
            s = _dot_nt(q_ref[:, hs], k_ref[pl.ds(k0, TK), hs])
            if diagonal:
                s = jnp.where(col <= row, s, NEG)
            _scores_put(h, kb, s, s_ref, mx_ref, diagonal)

    def off_diagonal(kb, _):
        scores(kb, False)
        return 0

    scores(i, True)
    lax.fori_loop(0, i, off_diagonal, 0)
    _row_max(mx_ref, mb_ref)

    def probs(kb, first):
        k0 = pl.multiple_of(kb * TK, TK)
        for h in range(N_HEADS):
            v_blk = v_ref[pl.ds(k0, TK), (h // 2) * LANES:(h // 2 + 1) * LANES]
            _probs_accumulate(h, kb, c, v_blk, s_ref, mb_ref, ls_ref, acc_ref, first)

    def off_diagonal_probs(kb, _):
        probs(kb, False)
        return 0

    probs(i, True)
    lax.fori_loop(0, i, off_diagonal_probs, 0)
    lane = lax.broadcasted_iota(jnp.int32, (TQ, LANES), 1)
    for pair in range(2):
        o_ref[:, pair * LANES:(pair + 1) * LANES] = jnp.where(
            lane < HEAD_DIM, _softmax_out(2 * pair, ls_ref, acc_ref),
            _softmax_out(2 * pair + 1, ls_ref, acc_ref)).astype(BF16)


def _mla_attn(mq, mk, mv, batch, seq):
    r3 = lambda a: a.reshape(batch, seq, a.shape[-1])
    return pl.pallas_call(
        _mla_attn_kernel,
        grid=(batch, seq // TQ),
        in_specs=[pl.BlockSpec((None, TQ, 512), lambda b, i: (b, i, 0)),
                  pl.BlockSpec((None, seq, 512), lambda b, i: (b, 0, 0)),
                  pl.BlockSpec((None, seq, 256), lambda b, i: (b, 0, 0))],
        out_specs=pl.BlockSpec((None, TQ, 256), lambda b, i: (b, i, 0)),
        out_shape=jax.ShapeDtypeStruct((batch, seq, 256), BF16),
        scratch_shapes=_softmax_scratch(seq // TK),
        compiler_params=_cparams(("parallel", "parallel")),
        name="mla_attention",
    )(r3(mq), r3(mk), r3(mv))


def _sb_attn_kernel(q_ref, k_ref, v_ref, tri_ref, o_ref, qm_ref, e_ref, tail_ref, acc_ref):
    i = pl.program_id(1)
    row, col = _tile_iotas()
    for pair in range(2):
        qp = q_ref[:, pair * LANES:(pair + 1) * LANES].astype(F32)
        for sub in range(2):
            qm_ref[2 * pair + sub] = _half_select(sub, qp).astype(BF16)

    def log_weights(kb, diagonal):
        k0 = pl.multiple_of(kb * TK, TK)
        tri2 = tri_ref[...]
        heads = range(N_HEADS)
        zs = [_dot_nt(qm_ref[h], k_ref[pl.ds(k0, TK), (h // 2) * LANES:(h // 2 + 1) * LANES]) for h in heads]
        lks = [jnp.minimum(z, 0.0) - jnp.log(1.0 + jnp.exp(-jnp.abs(z))) - z for z in zs]
        if diagonal:
            lks = [jnp.where(col < row, lk, 0.0) for lk in lks]
        incls = [_dot(jnp.concatenate(_split_bf16(lk), axis=1), tri2) for lk in lks]
        for h in heads:
            total = jnp.broadcast_to(incls[h][:, 0:1], (TQ, LANES))
            if diagonal:
                e_ref[h, kb] = jnp.where(col < row, zs[h] + incls[h], NEG)
                tail_ref[h] = total
            else:
                tail = tail_ref[h]
                e_ref[h, kb] = zs[h] + incls[h] + jnp.concatenate([tail, tail], axis=1)
                tail_ref[h] = tail + total

    log_weights(i, True)

    def off_diagonal(step, _):
        log_weights(i - 1 - step, False)
        return 0

    lax.fori_loop(0, i, off_diagonal, 0)

    def weighted_values(kb, first):
        k0 = pl.multiple_of(kb * TK, TK)
        for h in range(N_HEADS):
            ps = slice((h // 2) * LANES, (h // 2 + 1) * LANES)
            av = _dot(jnp.exp(e_ref[h, kb]).astype(BF16), v_ref[pl.ds(k0, TK), ps])
            if first:
                acc_ref[h] = av
            else:
                acc_ref[h] += av

    def off_diagonal_values(kb, _):
        weighted_values(kb, False)
        return 0

    weighted_values(i, True)
    lax.fori_loop(0, i, off_diagonal_values, 0)
    lane = lax.broadcasted_iota(jnp.int32, (TQ, LANES), 1)
    for pair in range(2):
        o_ref[:, pair * LANES:(pair + 1) * LANES] = jnp.where(
            lane < HEAD_DIM, acc_ref[2 * pair], acc_ref[2 * pair + 1]).astype(BF16)


def _sb_attn(sq, sk, sv, tri, batch, seq):
    r3 = lambda a: a.reshape(batch, seq, a.shape[-1])
    kspec = pl.BlockSpec((None, seq, 256), lambda b, i: (b, 0, 0))
    qspec = pl.BlockSpec((None, TQ, 256), lambda b, i: (b, i, 0))
    return pl.pallas_call(
        _sb_attn_kernel,
        grid=(batch, seq // TQ),
        in_specs=[qspec, kspec, kspec, _const_spec(tri.shape)],
        out_specs=qspec,
        out_shape=jax.ShapeDtypeStruct((batch, seq, 256), BF16),
        scratch_shapes=[pltpu.VMEM((N_HEADS, TQ, LANES), BF16), pltpu.VMEM((N_HEADS, seq // TK, TQ, TK), F32),
                        pltpu.VMEM((N_HEADS, TQ, LANES), F32), pltpu.VMEM((N_HEADS, TQ, LANES), F32)],
        compiler_params=_cparams(("parallel", "parallel")),
        name="stick_breaking_attention",
    )(r3(sq), r3(sk), r3(sv), tri)


def _merge_kernel(x_ref, ya_ref, yb_ref, yc_ref, yd_ref, wg_ref, wb_ref, wo_ref, g_ref, b_ref,
                  o_ref):
    x = x_ref[...]
    xb = x.astype(BF16)
    mixed = jnp.zeros((TM, D_MODEL), F32)
    for n, y_ref in enumerate((ya_ref, yb_ref, yc_ref, yd_ref)):
        gate = _sigmoid(_dot(xb, wg_ref[:, n * D_MODEL:(n + 1) * D_MODEL]))
        mixed = mixed + gate * _dot(y_ref[...], wb_ref[n])
    h = DEEPNORM_ALPHA * x + _dot(mixed.astype(BF16), wo_ref[...])
    o_ref[...] = _layer_norm(h, g_ref[...], b_ref[...])


def _merge(x2d, ys, wts):
    n = x2d.shape[0]
    row = lambda w: pl.BlockSpec((TM, w), lambda i: (i, 0))
    consts = [wts['w_gate'], wts['w_branch'], wts['w_out'], wts['ln1_g'], wts['ln1_b']]
    return pl.pallas_call(
        _merge_kernel,
        grid=(n // TM,),
        in_specs=[row(D_MODEL)] + [row(BRANCH_W)] * 4 + [_const_spec(c.shape) for c in consts],
        out_specs=row(D_MODEL),
        out_shape=jax.ShapeDtypeStruct((n, D_MODEL), F32),
        compiler_params=_cparams(("parallel",)),
        name="branch_merge_ln1",
    )(x2d, *ys, *consts)


def _ple_ln2(x1, x1b, f, p_ref, wpg_ref, wpp_ref, g_ref, b_ref):
    ple = _sigmoid(_dot(x1b, wpg_ref[...])) * _dot(p_ref[...].astype(BF16), wpp_ref[...])
    return _layer_norm(DEEPNORM_ALPHA * x1 + f + ple, g_ref[...], b_ref[...])


FF_CHUNK = 256


def _ffn_dense_kernel(x_ref, p_ref, wi_ref, wo_ref, wpg_ref, wpp_ref, g_ref, b_ref, o_ref, acc_ref):
    xb = x_ref[...].astype(BF16)
    for c in range(D_FF // FF_CHUNK):
        a = _dot(xb, wi_ref[:, c * FF_CHUNK:(c + 1) * FF_CHUNK])
        u = _dot(xb, wi_ref[:, D_FF + c * FF_CHUNK:D_FF + (c + 1) * FF_CHUNK])
        part = _dot((_silu(a) * u).astype(BF16), wo_ref[c * FF_CHUNK:(c + 1) * FF_CHUNK, :])
        if c == 0:
            acc_ref[...] = part
        else:
            acc_ref[...] += part
    o_ref[...] = _ple_ln2(x_ref[...], xb, acc_ref[...], p_ref, wpg_ref, wpp_ref, g_ref, b_ref)


def _ffn_dense(x1, p2d, wts):
    n = x1.shape[0]
    row = lambda w: pl.BlockSpec((TF, w), lambda i: (i, 0))
    consts = [wts['ffn_w_in'], wts['ffn_w_out'], wts['ple_w_gate'], wts['ple_w_proj'], wts['ln2_g'], wts['ln2_b']]
    return pl.pallas_call(
        _ffn_dense_kernel,
        grid=(n // TF,),
        in_specs=[row(D_MODEL), row(P_DIM)] + [_const_spec(c.shape) for c in consts],
        out_specs=row(D_MODEL),
        out_shape=jax.ShapeDtypeStruct((n, D_MODEL), F32),
        scratch_shapes=[pltpu.VMEM((TF, D_MODEL), F32)],
        compiler_params=_cparams(("parallel",)),
        name="ffn_dense_ple_ln2",
    )(x1, p2d, *consts)


TR = 512
INFO_LANES = 6


def _moe_route_kernel(x_ref, wr_ref, tri_ref, info_ref, cnt_ref, run_ref):
    @pl.when(pl.program_id(0) == 0)
    def _():
        run_ref[...] = jnp.zeros_like(run_ref)

    xh, xl = _split_bf16(x_ref[...])
    wh, wl = _split_bf16(wr_ref[...])
    logits = _dot(xh, wh) + (_dot(xl, wh) + _dot(xh, wl))
    lane = lax.broadcasted_iota(jnp.int32, (TR, LANES), 1)
    lane_f = lane.astype(F32)
    logits = jnp.where(lane < N_EXPERTS, logits, NEG)
    m1 = jnp.max(logits, axis=-1, keepdims=True)
    i1 = jnp.min(jnp.where(logits == m1, lane_f, float(LANES)), axis=-1, keepdims=True)
    rest = jnp.where(lane_f == i1, NEG, logits)
    m2 = jnp.max(rest, axis=-1, keepdims=True)
    i2 = jnp.min(jnp.where(rest == m2, lane_f, float(LANES)), axis=-1, keepdims=True)
    e = jnp.exp(m2 - m1)
    g1 = 1.0 / (1.0 + e)
    g2 = e / (1.0 + e)
    hot1 = lane_f == i1
    hot2 = lane_f == i2
    onehot = jnp.where(hot1 | hot2, 1.0, 0.0)
    before = _dot(tri_ref[...], onehot.astype(BF16)) + run_ref[0:1, :]
    r1 = jnp.sum(jnp.where(hot1, before, 0.0), axis=-1, keepdims=True)
    r2 = jnp.sum(jnp.where(hot2, before, 0.0), axis=-1, keepdims=True)
    run_ref[0:1, :] = run_ref[0:1, :] + jnp.sum(onehot, axis=0, keepdims=True)
    info = jnp.zeros((TR, LANES), F32)
    for k, val in enumerate((i1, i2, r1, r2, g1, g2)):
        info = jnp.where(lane == k, val, info)
    info_ref[...] = info
    cnt_ref[...] = jnp.broadcast_to(run_ref[0:1, :], cnt_ref.shape)


def _moe_route(x1, w_router_pad, tri):
    n = x1.shape[0]
    return pl.pallas_call(
        _moe_route_kernel,
        grid=(n // TR,),
        in_specs=[pl.BlockSpec((TR, D_MODEL), lambda i: (i, 0)), _const_spec(w_router_pad.shape),
                  _const_spec(tri.shape)],
        out_specs=[pl.BlockSpec((TR, LANES), lambda i: (i, 0)), pl.BlockSpec((8, LANES), lambda i: (0, 0))],
        out_shape=[jax.ShapeDtypeStruct((n, LANES), F32), jax.ShapeDtypeStruct((8, LANES), F32)],
        scratch_shapes=[pltpu.VMEM((8, LANES), F32)],
        compiler_params=_cparams(("arbitrary",)),
        name="moe_router_rank",
    )(x1, w_router_pad, tri)


def _moe_rowmap_kernel(d1_ref, d2_ref, rt_ref):
    def clear(r, _):
        rt_ref[r] = 0
        return 0

    lax.fori_loop(0, rt_ref.shape[0], clear, 0, unroll=16)

    def place(t, _):
        rt_ref[d1_ref[t]] = t
        rt_ref[d2_ref[t]] = t
        return 0

    lax.fori_loop(0, d1_ref.shape[0], place, 0, unroll=8)


def _moe_rowmap(dest1, dest2, n_rows):
    smem = pl.BlockSpec(memory_space=pltpu.SMEM)
    return pl.pallas_call(
        _moe_rowmap_kernel,
        in_specs=[smem, smem],
        out_specs=smem,
        out_shape=jax.ShapeDtypeStruct((n_rows,), jnp.int32),
        name="moe_row_map",
    )(dest1, dest2)


EF_CHUNK = 1792
EF_STEPS = D_FF_EXPERT // EF_CHUNK
assert EF_STEPS >= 2
ROWS_PER_STEP = MOE_BLOCK // EF_STEPS


def _moe_ffn_kernel(be_ref, na_ref, rt_ref, x_hbm, wa_ref, wu_ref, wo_ref, ys_ref, xs_ref, acc_ref, sem):
    del be_ref
    blk, c = pl.program_id(0), pl.program_id(1)
    n_live = na_ref[0]
    last_step = pl.num_programs(1) - 1
    slot = blk % 2

    def row_copy(block, r, s):
        return pltpu.make_async_copy(x_hbm.at[pl.ds(rt_ref[block * MOE_BLOCK + r], 1), :],
                                     xs_ref.at[s, pl.ds(r, 1), :], sem.at[s])

    def wait_block(s):
        pltpu.make_async_copy(x_hbm.at[pl.ds(0, MOE_BLOCK), :], xs_ref.at[s], sem.at[s]).wait()

    @pl.when(blk < n_live)
    def _():
        @pl.when((blk == 0) & (c == 0))
        def _():
            def body(r, _):
                row_copy(0, r, 0).start()
                return 0
            lax.fori_loop(0, MOE_BLOCK, body, 0)

        @pl.when(c == 0)
        def _():
            wait_block(slot)

        xb = xs_ref[slot].astype(BF16)
        h = (_silu(_dot(xb, wa_ref[...])) * _dot(xb, wu_ref[...])).astype(BF16)
        part = _dot(h, wo_ref[...])

        nxt = jnp.minimum(blk + 1, pl.num_programs(0) - 1)
        for r in range(ROWS_PER_STEP):
            row_copy(nxt, c * ROWS_PER_STEP + r, 1 - slot).start()

        @pl.when(c == 0)
        def _():
            acc_ref[...] = part

        @pl.when((c > 0) & (c < last_step))
        def _():
            acc_ref[...] += part

        @pl.when(c == last_step)
        def _():
            ys_ref[...] = acc_ref[...] + part

        @pl.when((c == last_step) & (blk == n_live - 1))
        def _():
            wait_block(1 - slot)

    @pl.when((blk >= n_live) & (c == last_step))
    def _():
        ys_ref[...] = jnp.zeros_like(ys_ref)


def _moe_ffn(x1, row_tok, blk_expert, n_active, w_in, w_out):
    n_rows = row_tok.shape[0]
    n_blk = n_rows // MOE_BLOCK
    n_ch = EF_STEPS
    live = lambda b, na: jnp.minimum(b, na[0] - 1)
    chunk = lambda b, c, na: jnp.where(b < na[0], c, n_ch - 1)
    return pl.pallas_call(
        _moe_ffn_kernel,
        grid_spec=pltpu.PrefetchScalarGridSpec(
            num_scalar_prefetch=3,
            grid=(n_blk, n_ch),
            in_specs=[
                pl.BlockSpec(memory_space=pl.ANY),
                pl.BlockSpec((None, D_MODEL, EF_CHUNK),
                             lambda b, c, be, na, rt: (be[live(b, na)], 0, chunk(b, c, na))),
                pl.BlockSpec((None, D_MODEL, EF_CHUNK),
                             lambda b, c, be, na, rt: (be[live(b, na)], 0, n_ch + chunk(b, c, na))),
                pl.BlockSpec((None, EF_CHUNK, D_MODEL),
                             lambda b, c, be, na, rt: (be[live(b, na)], chunk(b, c, na), 0)),
            ],
            out_specs=pl.BlockSpec((MOE_BLOCK, D_MODEL), lambda b, c, be, na, rt: (b, 0)),
            scratch_shapes=[pltpu.VMEM((2, MOE_BLOCK, D_MODEL), F32), pltpu.VMEM((MOE_BLOCK, D_MODEL), F32),
                            pltpu.SemaphoreType.DMA((2,))],
        ),
        out_shape=jax.ShapeDtypeStruct((n_rows, D_MODEL), F32),
        compiler_params=_cparams(("arbitrary", "arbitrary")),
        name="moe_expert_swiglu",
    )(blk_expert, n_active, row_tok, x1, w_in, w_in, w_out)


TC = 256


def _moe_combine_kernel(d1_ref, d2_ref, ys_hbm, x_ref, p_ref, info_ref, wpg_ref, wpp_ref, g_ref,
                        b_ref, o_ref, ya_ref, yb_ref, sem):
    i = pl.program_id(0)
    last = pl.num_programs(0) - 1
    slot = i % 2

    def copies(tile, r, s):
        t = tile * TC + r
        return (pltpu.make_async_copy(ys_hbm.at[pl.ds(d1_ref[t], 1), :], ya_ref.at[s, pl.ds(r, 1), :], sem.at[0, s]),
                pltpu.make_async_copy(ys_hbm.at[pl.ds(d2_ref[t], 1), :], yb_ref.at[s, pl.ds(r, 1), :], sem.at[1, s]))

    def wait_tile(s):
        pltpu.make_async_copy(ys_hbm.at[pl.ds(0, TC), :], ya_ref.at[s], sem.at[0, s]).wait()
        pltpu.make_async_copy(ys_hbm.at[pl.ds(0, TC), :], yb_ref.at[s], sem.at[1, s]).wait()

    @pl.when(i == 0)
    def _():
        def body(r, _):
            for cp in copies(0, r, 0):
                cp.start()
            return 0
        lax.fori_loop(0, TC, body, 0)

    wait_tile(slot)
    nxt = jnp.minimum(i + 1, last)
    for r in range(TC):
        for cp in copies(nxt, r, 1 - slot):
            cp.start()
    info = info_ref[...]
    f = info[:, 4:5] * ya_ref[slot] + info[:, 5:6] * yb_ref[slot]
    x = x_ref[...]
    o_ref[...] = _ple_ln2(x, x.astype(BF16), f, p_ref, wpg_ref, wpp_ref, g_ref, b_ref)

    @pl.when(i == last)
    def _():
        wait_tile(1 - slot)


def _moe_combine(ys, dest1, dest2, x1, p2d, info, wts):
    n = x1.shape[0]
    row = lambda w: pl.BlockSpec((TC, w), lambda i, d1, d2: (i, 0))
    consts = [wts['ple_w_gate'], wts['ple_w_proj'], wts['ln2_g'], wts['ln2_b']]
    cspec = lambda c: pl.BlockSpec(c.shape, lambda i, d1, d2: (0,) * c.ndim, pipeline_mode=pl.Buffered(1))
    return pl.pallas_call(
        _moe_combine_kernel,
        grid_spec=pltpu.PrefetchScalarGridSpec(
            num_scalar_prefetch=2,
            grid=(n // TC,),
            in_specs=[pl.BlockSpec(memory_space=pl.ANY), row(D_MODEL), row(P_DIM), row(LANES)]
            + [cspec(c) for c in consts],
            out_specs=row(D_MODEL),
            scratch_shapes=[pltpu.VMEM((2, TC, D_MODEL), F32), pltpu.VMEM((2, TC, D_MODEL), F32),
                            pltpu.SemaphoreType.DMA((2, 2))],
        ),
        out_shape=jax.ShapeDtypeStruct((n, D_MODEL), F32),
        compiler_params=_cparams(("arbitrary",)),
        name="moe_combine_ple_ln2",
    )(dest1, dest2, ys, x1, p2d, info, *consts)


def _moe(x1, p2d, wts):
    n = x1.shape[0]
    n_rows = ((n * 2 + MOE_BLOCK - 1) // MOE_BLOCK) * MOE_BLOCK + N_EXPERTS * MOE_BLOCK
    info, cnt = _moe_route(x1, wts['w_router'], wts['tri_tokens'])
    counts = cnt[0, :N_EXPERTS].astype(jnp.int32)
    padded = ((counts + MOE_BLOCK - 1) // MOE_BLOCK) * MOE_BLOCK
    ends = jnp.cumsum(padded)
    start_pad = ends - padded
    e1, e2 = info[:, 0].astype(jnp.int32), info[:, 1].astype(jnp.int32)
    dest1 = start_pad[e1] + info[:, 2].astype(jnp.int32)
    dest2 = start_pad[e2] + info[:, 3].astype(jnp.int32)
    n_blk = n_rows // MOE_BLOCK
    blk_row0 = jnp.arange(n_blk, dtype=jnp.int32) * MOE_BLOCK
    blk_expert = jnp.minimum(jnp.sum((ends[None, :] <= blk_row0[:, None]).astype(jnp.int32), axis=1),
                             N_EXPERTS - 1)
    n_active = (ends[-1:] // MOE_BLOCK).astype(jnp.int32)
    row_tok = _moe_rowmap(dest1, dest2, n_rows)
    ys = _moe_ffn(x1, row_tok, blk_expert, n_active, wts['moe_w_in'], wts['moe_w_out'])
    return _moe_combine(ys, dest1, dest2, x1, p2d, info, wts)


def _rot_half_cols(w, heads, dim):
    w3 = w.reshape(w.shape[0], heads, dim)
    half = dim // 2
    return jnp.concatenate([-w3[..., half:], w3[..., :half]], axis=-1).reshape(w.shape[0], heads * dim)


def _prep_layer(i, w_in, conv_w, conv_b, conv_ln_g, conv_ln_b, nsa_cmp_pe, nsa_cmp_w1, nsa_cmp_w2,
                mla_q_norm, mla_kv_norm, mla_w_uq, mla_w_ukv, w_branch, w_out, ln1_g, ln1_b,
                ple_w_gate, ple_w_proj, ln2_g, ln2_b):
    w = w_in[i]
    d = w.shape[0]
    z = lambda n: jnp.zeros((d, n), F32)
    dup = lambda a: jnp.concatenate([a, a], axis=1)
    c_glu, nq = w[:, 0:512], w[:, 512:768]
    nkv = w[:, 768:1152]
    k_cmp, v_cmp, k_slc, v_slc, k_win, v_win = [nkv[:, j * 64:(j + 1) * 64] for j in range(6)]
    ng = w[:, 1152:1164]
    mq, mkv, mkr = w[:, 1164:1420], w[:, 1420:1548], w[:, 1548:1580]
    sb = w[:, 1580:2348]
    bg = w[:, 2348:6444]
    cols = [c_glu, nq, _rot_half_cols(nq, NSA_HEADS, HEAD_DIM),
            dup(k_slc), dup(_rot_half_cols(k_slc, 1, HEAD_DIM)),
            dup(k_win), dup(_rot_half_cols(k_win, 1, HEAD_DIM)),
            dup(v_slc), dup(v_win), k_cmp, v_cmp, ng, z(LANES - 12), mq, mkv,
            z(64), mkr, z(32), z(64), _rot_half_cols(mkr, 1, MLA_ROPE), z(32), z(LANES), sb]
    w1 = jnp.concatenate(cols, axis=1).astype(BF16)
    assert w1.shape[1] == C_TOT

    inv32 = ROPE_THETA ** (-jnp.arange(HEAD_DIM // 2, dtype=F32) / (HEAD_DIM // 2))
    inv16 = ROPE_THETA ** (-jnp.arange(MLA_ROPE // 2, dtype=F32) / (MLA_ROPE // 2))
    inv_nsa = jnp.tile(inv32, 4)[None, :]
    inv_mla = jnp.concatenate([jnp.zeros((64,), F32), inv16, inv16, jnp.zeros((32,), F32)])[None, :]

    wuq = mla_w_uq[i].reshape(MLA_Q_RANK, MLA_HEADS, MLA_NOPE + MLA_ROPE)
    zq = jnp.zeros((MLA_Q_RANK, MLA_HEADS, 32), F32)
    wq = jnp.concatenate([wuq, zq], axis=-1).reshape(MLA_Q_RANK, MLA_HEADS * LANES)
    rope_rot = jnp.concatenate([-wuq[..., MLA_NOPE + 16:], wuq[..., MLA_NOPE:MLA_NOPE + 16]], axis=-1)
    wqr = jnp.concatenate([jnp.zeros((MLA_Q_RANK, MLA_HEADS, MLA_NOPE), F32), rope_rot, zq],
                          axis=-1).reshape(MLA_Q_RANK, MLA_HEADS * LANES)
    wukv = mla_w_ukv[i].reshape(MLA_KV_RANK, MLA_HEADS, MLA_NOPE + MLA_V)
    wk = jnp.concatenate([wukv[..., :MLA_NOPE], jnp.zeros((MLA_KV_RANK, MLA_HEADS, 64), F32)],
                         axis=-1).reshape(MLA_KV_RANK, MLA_HEADS * LANES)
    wv = wukv[..., MLA_NOPE:].reshape(MLA_KV_RANK, MLA_HEADS * MLA_V)

    pe = nsa_cmp_pe[i]
    pe_rows = pe.reshape(CMP_BLOCK, 2 * HEAD_DIM)
    pe_a = pe_rows[:CMP_STRIDE].reshape(1, CMP_STRIDE * LANES)
    pe_b = pe_rows[CMP_STRIDE:].reshape(1, CMP_STRIDE * LANES)
    w1c = nsa_cmp_w1[i].reshape(2, CMP_BLOCK, HEAD_DIM, HEAD_DIM)
    zblk = jnp.zeros((CMP_BLOCK, HEAD_DIM, HEAD_DIM), F32)
    w1full = jnp.concatenate([jnp.concatenate([w1c[0], zblk], axis=2),
                              jnp.concatenate([zblk, w1c[1]], axis=2)], axis=1)
    w1a = w1full[:CMP_STRIDE].reshape(CMP_STRIDE * LANES, LANES).astype(BF16)
    w1b = w1full[CMP_STRIDE:].reshape(CMP_STRIDE * LANES, LANES).astype(BF16)
    w2 = nsa_cmp_w2[i]
    z64 = jnp.zeros((HEAD_DIM, LANES), F32)
    w2k = jnp.concatenate([dup(w2[0]), z64], axis=0).astype(BF16)
    w2v = jnp.concatenate([z64, dup(w2[1])], axis=0).astype(BF16)

    return dict(
        w1=w1, inv_nsa=inv_nsa, inv_mla=inv_mla,
        mla_qn=mla_q_norm[i][None, :], mla_kvn=mla_kv_norm[i][None, :],
        wq=wq.astype(BF16), wqr=wqr.astype(BF16), wk=wk.astype(BF16), wv=wv.astype(BF16),
        conv_w=conv_w[i], conv_b=conv_b[i][None, :], conv_g=conv_ln_g[i][None, :], conv_beta=conv_ln_b[i][None, :],
        pe_a=pe_a, pe_b=pe_b, w1a=w1a, w1b=w1b, w2k=w2k, w2v=w2v,
        w_gate=bg.astype(BF16), w_branch=w_branch[i].astype(BF16), w_out=w_out[i].astype(BF16),
        ln1_g=ln1_g[i][None, :], ln1_b=ln1_b[i][None, :],
        ple_w_gate=ple_w_gate[i].astype(BF16), ple_w_proj=ple_w_proj[i].astype(BF16),
        ln2_g=ln2_g[i][None, :], ln2_b=ln2_b[i][None, :],
    )


def _tables(seq):
    n_cmp_rows = seq // CMP_STRIDE
    n_sel = seq // SEL_BLOCK
    cmp_start = jnp.arange(n_cmp_rows) * CMP_STRIDE
    sel_start = jnp.arange(LANES) * SEL_BLOCK
    n_cmp = (seq - CMP_BLOCK) // CMP_STRIDE + 1
    overlap = ((cmp_start[:, None] < sel_start[None, :] + SEL_BLOCK)
               & (cmp_start[:, None] + CMP_BLOCK > sel_start[None, :])
               & (jnp.arange(n_cmp_rows)[:, None] < n_cmp) & (jnp.arange(LANES)[None, :] < n_sel))
    kb = jnp.arange(seq // TK)[:, None, None]
    nn = jnp.arange(LANES)[None, :, None]
    ll = jnp.arange(TK)[None, None, :]
    expand = (kb * TK + ll) // SEL_BLOCK == nn
    jj = jnp.arange(TK)
    tri_keys = jnp.tile(jj[:, None] >= jj[None, :], (2, 1))
    tt = jnp.arange(TR)
    tri_tokens = tt[None, :] < tt[:, None]
    src = jnp.arange(LANES)[:, None]
    dst = jnp.arange(3 * N_HEADS * HEAD_DIM)[None, :]
    width = N_HEADS * HEAD_DIM
    gate_expand = jnp.tile((src < 3 * N_HEADS) & (src % 3 == dst // width) & (src // 3 == (dst % width) // HEAD_DIM),
                           (2, 1))
    return dict(overlap_t=overlap.T.astype(BF16), eye_q=jnp.eye(TQ, dtype=BF16), gate_expand=gate_expand.astype(BF16),
                expand=expand.astype(BF16), tri_keys=tri_keys.astype(BF16),
                tri_tokens=tri_tokens.astype(BF16))


def kernel(x, p, positions, w_in, conv_w, conv_b, conv_ln_g, conv_ln_b, nsa_cmp_pe, nsa_cmp_w1, nsa_cmp_w2,
           mla_q_norm, mla_kv_norm, mla_w_uq, mla_w_ukv, w_branch, w_out, ln1_g, ln1_b, ffn_w_in, ffn_w_out,
           moe_router, moe_w_in, moe_w_out, ple_w_gate, ple_w_proj, ln2_g, ln2_b):
    batch, seq, _ = x.shape
    n = batch * seq
    tabs = _tables(seq)
    x2d = x.reshape(n, D_MODEL)
    pos2d = positions.reshape(n, 1)
    for i in range(DEPTH):
        wts = _prep_layer(i, w_in, conv_w, conv_b, conv_ln_g, conv_ln_b, nsa_cmp_pe, nsa_cmp_w1, nsa_cmp_w2,
                          mla_q_norm, mla_kv_norm, mla_w_uq, mla_w_ukv, w_branch, w_out, ln1_g, ln1_b,
                          ple_w_gate, ple_w_proj, ln2_g, ln2_b)
        wts['overlap_t'], wts['eye_q'] = tabs['overlap_t'], tabs['eye_q']
        (conv_in, nq, nqr, ks, kw, vs, vw, kvc, ng, mq, mk, mv, sq, sk, sv) = _mixer_in(x2d, pos2d, wts)
        y_a = _conv(conv_in, wts['conv_w'], wts['conv_b'], wts['conv_g'], wts['conv_beta'], batch, seq)
        ocmp, sel = _nsa_cmp(kvc, nq, wts, batch, seq)
        y_b = _nsa_attn(nqr, ks, vs, kw, vw, sel, tabs['expand'], ocmp, ng, tabs['gate_expand'], batch, seq)
        y_c = _mla_attn(mq, mk, mv, batch, seq)
        y_d = _sb_attn(sq, sk, sv, tabs['tri_keys'], batch, seq)
        ys = [y.reshape(n, BRANCH_W) for y in (y_a, y_b, y_c, y_d)]
        x1 = _merge(x2d, ys, wts)
        p2d = p[i].reshape(n, P_DIM)
        if i % 2 == 0:
            wts['ffn_w_in'] = ffn_w_in[i // 2].astype(BF16)
            wts['ffn_w_out'] = ffn_w_out[i // 2].astype(BF16)
            x2d = _ffn_dense(x1, p2d, wts)
        else:
            wts['w_router'] = jnp.concatenate(
                [moe_router[i // 2], jnp.zeros((D_MODEL, LANES - N_EXPERTS), F32)], axis=1)
            wts['tri_tokens'] = tabs['tri_tokens']
            wts['moe_w_in'] = moe_w_in[i // 2].astype(BF16)
            wts['moe_w_out'] = moe_w_out[i // 2].astype(BF16)
            x2d = _moe(x1, p2d, wts)
    return x2d.reshape(batch, seq, D_MODEL)
```

```python
import functools

import jax
import jax.numpy as jnp
from jax import lax
from jax.experimental import pallas as pl
from jax.experimental.pallas import tpu as pltpu

F32 = jnp.float32
BF16 = jnp.bfloat16

D_MODEL = 1024
DEPTH = 2
CONV_CH = 256
CONV_WIDTH = 31
NSA_HEADS = 4
HEAD_DIM = 64
CMP_BLOCK = 32
CMP_STRIDE = 16
SEL_BLOCK = 64
SEL_TOPN = 16
WINDOW = 512
MLA_HEADS = 4
MLA_Q_RANK = 256
MLA_KV_RANK = 128
MLA_NOPE = 64
MLA_ROPE = 32
MLA_V = 64
BRANCH_W = 256
ROPE_THETA = 10000.0
LN_EPS = 1e-5
RMS_EPS = 1e-6
D_FF = 2816
N_EXPERTS = 8
D_FF_EXPERT = 3584
MOE_BLOCK = 512
P_DIM = 256
DEEPNORM_ALPHA = (2 * DEPTH) ** 0.25

LANES = 128
SUBLANES = 8
MXU_W = 256
HEADS_W = 256
VMEM_LIMIT = 56 * 1024 * 1024

NEG = -1e30

C_CONV = 0
C_NQ = 512
C_NQR = 768
C_KS = 1024
C_KW = 1280
C_VS = 1536
C_KVC = 1792
C_MQ = 2048
C_MKV = 2304
C_MKRR = 2560
C_SB = 2816
C_TOT = 3584

TM = 256
TF = 512
TQ = 256
TK = 256


def _cparams(sem, vmem=VMEM_LIMIT):
    return pltpu.CompilerParams(dimension_semantics=sem, vmem_limit_bytes=vmem)


def _const_spec(shape):
    nd = len(shape)
    return pl.BlockSpec(shape, lambda *_: (0,) * nd, pipeline_mode=pl.Buffered(1))


def _dot(a, b):
    return jnp.dot(a, b, preferred_element_type=F32)


def _dot_nt(a, b):
    return lax.dot_general(a, b, (((1,), (1,)), ((), ())), preferred_element_type=F32)


def _layer_norm(h, g, b):
    mu = jnp.mean(h, axis=-1, keepdims=True)
    d = h - mu
    var = jnp.mean(d * d, axis=-1, keepdims=True)
    return d * lax.rsqrt(var + LN_EPS) * g + b


def _rms_norm(h, g):
    return h * lax.rsqrt(jnp.mean(h * h, axis=-1, keepdims=True) + RMS_EPS) * g


def _sigmoid(x):
    return 1.0 / (1.0 + jnp.exp(-x))


def _silu(x):
    return x * _sigmoid(x)


def _split_bf16(x):
    hi = x.astype(BF16)
    lo = (x - hi.astype(F32)).astype(BF16)
    return hi, lo


def _half_select(sub, x):
    lane = lax.broadcasted_iota(jnp.int32, x.shape, 1)
    keep = (lane < HEAD_DIM) if sub == 0 else (lane >= HEAD_DIM)
    return jnp.where(keep, x, 0.0)


def _mixer_in_kernel(x_ref, pos_ref, w_ref, invn_ref, invm_ref, qn_ref, kvn_ref, wq_ref, wqr_ref,
                     wk_ref, wv_ref,
                     conv_ref, nq_ref, nqr_ref, ks_ref, kw_ref, vs_ref, vw_ref, kvc_ref, ng_ref,
                     mq_ref, mk_ref, mv_ref, sq_ref, sk_ref, sv_ref):
    xb = x_ref[...].astype(BF16)

    def proj(c0, width):
        return _dot(xb, w_ref[:, c0:c0 + width])

    posf = pos_ref[...].astype(F32)
    ang_n = posf * invn_ref[...]
    cos_n, sin_n = jnp.cos(ang_n), jnp.sin(ang_n)
    ang_m = posf * invm_ref[...]
    cos_m, sin_m = jnp.cos(ang_m), jnp.sin(ang_m)

    lo, hi = slice(0, LANES), slice(LANES, 2 * LANES)
    qn = _rms_norm(proj(C_MQ, MLA_Q_RANK), qn_ref[...]).astype(BF16)
    mkv = proj(C_MKV, MXU_W)
    kvn = _rms_norm(mkv[:, lo], kvn_ref[...]).astype(BF16)

    conv_ref[...] = proj(C_CONV, 2 * CONV_CH)

    scale = HEAD_DIM ** -0.5
    cos_n2 = jnp.concatenate([cos_n, cos_n], axis=1)
    sin_n2 = jnp.concatenate([sin_n, sin_n], axis=1)
    q, qrot = proj(C_NQ, MXU_W), proj(C_NQR, MXU_W)
    nq_ref[...] = (q * scale).astype(BF16)
    nqr_ref[...] = ((q * cos_n2 + qrot * sin_n2) * scale).astype(BF16)
    k = proj(C_KS, MXU_W)
    ks_ref[...] = (k[:, lo] * cos_n + k[:, hi] * sin_n).astype(BF16)
    k = proj(C_KW, MXU_W)
    kw_ref[...] = (k[:, lo] * cos_n + k[:, hi] * sin_n).astype(BF16)
    v = proj(C_VS, MXU_W)
    vs_ref[...] = v[:, lo].astype(BF16)
    vw_ref[...] = v[:, hi].astype(BF16)
    u = proj(C_KVC, MXU_W)
    kvc_ref[...] = u[:, lo]
    ng_ref[...] = u[:, hi]

    kr = mkv[:, hi] * cos_m + proj(C_MKRR, MXU_W)[:, lo] * sin_m
    qa, qr, kk = _dot(qn, wq_ref[...]), _dot(qn, wqr_ref[...]), _dot(kvn, wk_ref[...])
    for h in range(MLA_HEADS):
        sl = slice(h * LANES, (h + 1) * LANES)
        mq_ref[:, sl] = (qa[:, sl] * cos_m + qr[:, sl] * sin_m).astype(BF16)
        mk_ref[:, sl] = (kk[:, sl] + kr).astype(BF16)
    mv_ref[...] = _dot(kvn, wv_ref[...]).astype(BF16)

    sq_ref[...] = (proj(C_SB, HEADS_W) * scale).astype(BF16)
    sk_ref[...] = proj(C_SB + HEADS_W, HEADS_W).astype(BF16)
    sv_ref[...] = proj(C_SB + 2 * HEADS_W, HEADS_W).astype(BF16)


def _mixer_in(x2d, pos2d, wts):
    n = x2d.shape[0]
    row = lambda w: pl.BlockSpec((TF, w), lambda i: (i, 0))
    mla_w = MLA_HEADS * LANES
    out_widths = [2 * CONV_CH, HEADS_W, HEADS_W, LANES, LANES, LANES, LANES, LANES, LANES, mla_w, mla_w,
                  HEADS_W, HEADS_W, HEADS_W, HEADS_W]
    out_dtypes = [F32, BF16, BF16, BF16, BF16, BF16, BF16, F32, F32, BF16, BF16, BF16, BF16, BF16, BF16]
    consts = [wts['w1'], wts['inv_nsa'], wts['inv_mla'], wts['mla_qn'], wts['mla_kvn'], wts['wq'],
              wts['wqr'], wts['wk'], wts['wv']]
    return pl.pallas_call(
        _mixer_in_kernel,
        grid=(n // TF,),
        in_specs=[row(D_MODEL), row(1)] + [_const_spec(c.shape) for c in consts],
        out_specs=[row(w) for w in out_widths],
        out_shape=[jax.ShapeDtypeStruct((n, w), d) for w, d in zip(out_widths, out_dtypes)],
        compiler_params=_cparams(("parallel",)),
        name="mixer_in",
    )(x2d, pos2d, *consts)


CONV_PAD = 32
CONV_CHUNK = 128


def _conv_kernel(u_ref, w_ref, b_ref, g_ref, beta_ref, o_ref, hp_ref):
    seq = u_ref.shape[0]
    hp_ref[0, 0:CONV_PAD, :] = jnp.zeros((CONV_PAD, CONV_CH), F32)
    hp_ref[0, CONV_PAD:CONV_PAD + seq, :] = u_ref[:, 0:CONV_CH] * _sigmoid(u_ref[:, CONV_CH:2 * CONV_CH])
    rows = seq + CONV_PAD - SUBLANES
    for s in range(1, SUBLANES):
        hp_ref[s, 0:rows, :] = hp_ref[0, s:s + rows, :]
    first = CONV_PAD - (CONV_WIDTH - 1)
    for c in range(seq // CONV_CHUNK):
        base = c * CONV_CHUNK
        acc = jnp.broadcast_to(b_ref[...], (CONV_CHUNK, CONV_CH))
        for j in range(CONV_WIDTH):
            a, s = divmod(first + j, SUBLANES)
            r0 = base + a * SUBLANES
            acc = acc + hp_ref[s, r0:r0 + CONV_CHUNK, :] * w_ref[j:j + 1, :]
        y = _layer_norm(acc, g_ref[...], beta_ref[...])
        o_ref[base:base + CONV_CHUNK, :] = _silu(y).astype(BF16)


def _conv(conv_in, w, b, g, beta, batch, seq):
    return pl.pallas_call(
        _conv_kernel,
        grid=(batch,),
        in_specs=[pl.BlockSpec((None, seq, 2 * CONV_CH), lambda i: (i, 0, 0)),
                  _const_spec(w.shape), _const_spec(b.shape), _const_spec(g.shape), _const_spec(beta.shape)],
        out_specs=pl.BlockSpec((None, seq, CONV_CH), lambda i: (i, 0, 0)),
        out_shape=jax.ShapeDtypeStruct((batch, seq, CONV_CH), BF16),
        scratch_shapes=[pltpu.VMEM((SUBLANES, CONV_PAD + seq, CONV_CH), F32)],
        compiler_params=_cparams(("parallel",)),
        name="conformer_conv",
    )(conv_in.reshape(batch, seq, 2 * CONV_CH), w, b, g, beta)


def _gelu_tanh(x):
    return 0.5 * x * (1.0 + jnp.tanh(0.7978845608028654 * (x + 0.044715 * x * x * x)))


SEL_SHIFT = 6


def _nsa_cmp_kernel(kvc_ref, q_ref, pea_ref, peb_ref, w1a_ref, w1b_ref, w2k_ref, w2v_ref, ovt_ref, eye_ref,
                    ocmp_ref, sel_ref, *, seq):
    n_cmp = (seq - CMP_BLOCK) // CMP_STRIDE + 1
    n_sel = seq // SEL_BLOCK
    nb = seq // CMP_STRIDE
    x2 = kvc_ref[...]
    xa = (x2 + pea_ref[...]).astype(BF16)
    xb = (x2 + peb_ref[...]).astype(BF16)
    ha = _dot(xa, w1a_ref[...])
    hb = _dot(xb, w1b_ref[...])
    hid = ha + pltpu.roll(hb, nb - 1, 0)
    hid = _gelu_tanh(hid).astype(BF16)
    kk = _dot(hid, w2k_ref[...]).astype(BF16)
    vv = _dot(hid, w2v_ref[...]).astype(BF16)
    ovt = ovt_ref[...]

    for c in range(seq // TQ):
        r0 = c * TQ
        t = r0 + lax.broadcasted_iota(jnp.int32, (TQ, LANES), 0)
        j = lax.broadcasted_iota(jnp.int32, (TQ, LANES), 1)
        valid = (j * CMP_STRIDE + CMP_BLOCK - 1 <= t) & (j < n_cmp)
        psum = jnp.zeros((TQ, LANES), F32)
        for pair in range(2):
            qp = q_ref[r0:r0 + TQ, pair * LANES:(pair + 1) * LANES].astype(F32)
            outs = []
            for sub in range(2):
                qm = _half_select(sub, qp).astype(BF16)
                s = jnp.where(valid, _dot_nt(qm, kk), NEG)
                m = jnp.max(s, axis=-1, keepdims=True)
                e = jnp.where(valid, jnp.exp(s - m), 0.0)
                den = jnp.sum(e, axis=-1, keepdims=True)
                p = e / jnp.where(den > 0, den, 1.0)
                psum = psum + p
                outs.append(_dot(p.astype(BF16), vv))
            ocmp_ref[r0:r0 + TQ, pair * LANES:(pair + 1) * LANES] = jnp.where(
                lax.broadcasted_iota(jnp.int32, (TQ, LANES), 1) < HEAD_DIM, outs[0], outs[1])
        p_hi, p_lo = _split_bf16(psum)
        imp = (_dot_nt(ovt, p_hi) + _dot_nt(ovt, p_lo))[0:n_sel, :]
        n = lax.broadcasted_iota(jnp.int32, (n_sel, TQ), 0)
        cur = jnp.right_shift(r0 + lax.broadcasted_iota(jnp.int32, (n_sel, TQ), 1), SEL_SHIFT)
        forced = (n == 0) | (n == cur) | (n == cur - 1)
        imp = jnp.where(forced, jnp.inf, imp)
        imp = jnp.where(n > cur, -jnp.inf, imp)
        rank = jnp.zeros((n_sel, TQ), F32)
        for n2 in range(n_sel):
            other = imp[n2:n2 + 1, :]
            ahead = (other > imp) | ((other == imp) & (n2 < n))
            rank = rank + jnp.where(ahead, 1.0, 0.0)
        sel_t = jnp.where((rank < SEL_TOPN) & (imp > -jnp.inf), 1.0, 0.0)
        sel_t = jnp.concatenate([sel_t, jnp.zeros((LANES - n_sel, TQ), F32)], axis=0).astype(BF16)
        sel_ref[r0:r0 + TQ, :] = _dot_nt(eye_ref[...], sel_t).astype(BF16)


def _nsa_cmp(kvc, nq, wts, batch, seq):
    nb = seq // CMP_STRIDE
    consts = [wts['pe_a'], wts['pe_b'], wts['w1a'], wts['w1b'], wts['w2k'], wts['w2v'], wts['overlap_t'],
              wts['eye_q']]
    return pl.pallas_call(
        functools.partial(_nsa_cmp_kernel, seq=seq),
        grid=(batch,),
        in_specs=[pl.BlockSpec((None, nb, CMP_STRIDE * LANES), lambda i: (i, 0, 0)),
                  pl.BlockSpec((None, seq, HEADS_W), lambda i: (i, 0, 0))] + [_const_spec(c.shape) for c in consts],
        out_specs=[pl.BlockSpec((None, seq, HEADS_W), lambda i: (i, 0, 0)),
                   pl.BlockSpec((None, seq, LANES), lambda i: (i, 0, 0))],
        out_shape=[jax.ShapeDtypeStruct((batch, seq, HEADS_W), F32),
                   jax.ShapeDtypeStruct((batch, seq, LANES), BF16)],
        compiler_params=_cparams(("parallel",)),
        name="nsa_compress_select",
    )(kvc.reshape(batch, nb, CMP_STRIDE * LANES), nq.reshape(batch, seq, HEADS_W), *consts)


LOG2E = 1.4426950408889634
N_HEADS = 4


def _softmax_scratch(n_tiles):
    slab = pltpu.VMEM((N_HEADS, TQ, LANES), F32)
    return [pltpu.VMEM((N_HEADS, n_tiles, TQ, TK), F32), slab, slab, slab, slab]


def _scores_put(h, t, s, s_ref, mx_ref, first):
    s_ref[h, t] = s
    m = jnp.maximum(s[:, :LANES], s[:, LANES:])
    mx_ref[h] = m if first else jnp.maximum(mx_ref[h], m)


def _row_max(mx_ref, mb_ref):
    for h in range(N_HEADS):
        mb_ref[h] = jnp.broadcast_to(jnp.max(mx_ref[h], axis=-1, keepdims=True), (TQ, LANES))


def _probs_accumulate(h, t, c, v_blk, s_ref, mb_ref, ls_ref, acc_ref, first):
    s, mb = s_ref[h, t], mb_ref[h]
    pa = jnp.exp2((s[:, :LANES] - mb) * c)
    pb = jnp.exp2((s[:, LANES:] - mb) * c)
    pv = _dot(jnp.concatenate([pa, pb], axis=1).astype(BF16), v_blk)
    if first:
        ls_ref[h] = pa + pb
        acc_ref[h] = pv
    else:
        ls_ref[h] += pa + pb
        acc_ref[h] += pv


def _lane_sum_dense(x):
    ones = jnp.ones((2 * LANES, LANES), BF16)
    return _dot(jnp.concatenate(_split_bf16(x), axis=1), ones)


def _softmax_out(h, ls_ref, acc_ref):
    return acc_ref[h] / _lane_sum_dense(ls_ref[h])


def _tile_iotas():
    return (lax.broadcasted_iota(jnp.int32, (TQ, TK), 0), lax.broadcasted_iota(jnp.int32, (TQ, TK), 1))


def _nsa_attn_kernel(q_ref, ks_ref, vs_ref, kw_ref, vw_ref, sel_ref, exp_ref, ocmp_ref, ng_ref, gate_ref, o_ref,
                     qm_ref, ss_ref, mxs_ref, mbs_ref, lss_ref, accs_ref,
                     sw_ref, mxw_ref, mbw_ref, lsw_ref, accw_ref):
    i = pl.program_id(1)
    row, col = _tile_iotas()
    sel = sel_ref[...]
    for pair in range(2):
        qp = q_ref[:, pair * LANES:(pair + 1) * LANES].astype(F32)
        for sub in range(2):
            qm_ref[2 * pair + sub] = _half_select(sub, qp).astype(BF16)

    def selected_scores(kb, diagonal):
        k0 = pl.multiple_of(kb * TK, TK)
        hit = _dot(sel, exp_ref[kb]) > 0.5
        if diagonal:
            hit = hit & (col <= row)
        bias = jnp.where(hit, 0.0, NEG)
        k_blk = ks_ref[pl.ds(k0, TK), :]
        for h in range(N_HEADS):
            _scores_put(h, kb, _dot_nt(qm_ref[h], k_blk) + bias, ss_ref, mxs_ref, diagonal)

    def window_scores(slot, mask):
        k0 = pl.multiple_of((i - 2 + slot) * TK, TK)
        k_blk = kw_ref[pl.ds(k0, TK), :]
        for h in range(N_HEADS):
            s = _dot_nt(qm_ref[h], k_blk)
            if mask is not None:
                s = jnp.where(mask, s, NEG)
            _scores_put(h, slot, s, sw_ref, mxw_ref, slot == 2)

    def off_diagonal(kb, _):
        selected_scores(kb, False)
        return 0

    selected_scores(i, True)
    window_scores(2, col <= row)
    lax.fori_loop(0, i, off_diagonal, 0)
    pl.when(i >= 2)(lambda: window_scores(0, col > row))
    pl.when(i >= 1)(lambda: window_scores(1, None))
    _row_max(mxs_ref, mbs_ref)
    _row_max(mxw_ref, mbw_ref)

    def selected_probs(kb, first):
        v_blk = vs_ref[pl.ds(pl.multiple_of(kb * TK, TK), TK), :]
        for h in range(N_HEADS):
            _probs_accumulate(h, kb, LOG2E, v_blk, ss_ref, mbs_ref, lss_ref, accs_ref, first)

    def window_probs(slot):
        v_blk = vw_ref[pl.ds(pl.multiple_of((i - 2 + slot) * TK, TK), TK), :]
        for h in range(N_HEADS):
            _probs_accumulate(h, slot, LOG2E, v_blk, sw_ref, mbw_ref, lsw_ref, accw_ref, slot == 2)

    def off_diagonal_probs(kb, _):
        selected_probs(kb, False)
        return 0

    selected_probs(i, True)
    window_probs(2)
    lax.fori_loop(0, i, off_diagonal_probs, 0)
    pl.when(i >= 2)(lambda: window_probs(0))
    pl.when(i >= 1)(lambda: window_probs(1))

    g = _sigmoid(ng_ref[...])
    gx = _dot(jnp.concatenate(_split_bf16(g), axis=1), gate_ref[...])
    lane = lax.broadcasted_iota(jnp.int32, (TQ, LANES), 1)

    def heads_out(ls_ref, acc_ref):
        return jnp.concatenate(
            [jnp.where(lane < HEAD_DIM, _softmax_out(2 * pair, ls_ref, acc_ref),
                       _softmax_out(2 * pair + 1, ls_ref, acc_ref)) for pair in range(2)], axis=1)

    width = N_HEADS * HEAD_DIM
    o_ref[...] = (gx[:, 0:width] * ocmp_ref[...] + gx[:, width:2 * width] * heads_out(lss_ref, accs_ref)
                  + gx[:, 2 * width:3 * width] * heads_out(lsw_ref, accw_ref)).astype(BF16)


def _nsa_attn(nqr, ks, vs, kw, vw, sel, expand, ocmp, ng, gate_expand, batch, seq):
    qspec = lambda w: pl.BlockSpec((None, TQ, w), lambda b, i: (b, i, 0))
    kspec = pl.BlockSpec((None, seq, LANES), lambda b, i: (b, 0, 0))
    r3 = lambda a: a.reshape(batch, seq, a.shape[-1])
    return pl.pallas_call(
        _nsa_attn_kernel,
        grid=(batch, seq // TQ),
        in_specs=[qspec(HEADS_W), kspec, kspec, kspec, kspec, qspec(LANES), _const_spec(expand.shape),
                  qspec(HEADS_W), qspec(LANES), _const_spec(gate_expand.shape)],
        out_specs=qspec(HEADS_W),
        out_shape=jax.ShapeDtypeStruct((batch, seq, HEADS_W), BF16),
        scratch_shapes=[pltpu.VMEM((N_HEADS, TQ, LANES), BF16)] + _softmax_scratch(seq // TK)
        + _softmax_scratch(WINDOW // TK + 1),
        compiler_params=_cparams(("parallel", "parallel")),
        name="nsa_select_window",
    )(r3(nqr), r3(ks), r3(vs), r3(kw), r3(vw), sel, expand, ocmp, r3(ng), gate_expand)


def _mla_attn_kernel(q_ref, k_ref, v_ref, o_ref, s_ref, mx_ref, mb_ref, ls_ref, acc_ref):
    i = pl.program_id(1)
    c = (MLA_NOPE + MLA_ROPE) ** -0.5 * LOG2E
    row, col = _tile_iotas()

    def scores(kb, diagonal):
        k0 = pl.multiple_of(kb * TK, TK)
        for h in range(N_HEADS):
            hs = slice(h * LANES, (h + 1) * LANES)
            s = _dot_nt(q_ref[:, hs], k_ref[pl.ds(k0, TK), hs])
            if diagonal:
                s = jnp.where(col <= row, s, NEG)
            _scores_put(h, kb, s, s_ref, mx_ref, diagonal)

    def off_diagonal(kb, _):
        scores(kb, False)
        return 0

    scores(i, True)
    lax.fori_loop(0, i, off_diagonal, 0)
    _row_max(mx_ref, mb_ref)

    def probs(kb, first):
        k0 = pl.multiple_of(kb * TK, TK)
        for h in range(N_HEADS):
            v_blk = v_ref[pl.ds(k0, TK), (h // 2) * LANES:(h // 2 + 1) * LANES]
            _probs_accumulate(h, kb, c, v_blk, s_ref, mb_ref, ls_ref, acc_ref, first)

    def off_diagonal_probs(kb, _):
        probs(kb, False)
        return 0

    probs(i, True)
    lax.fori_loop(0, i, off_diagonal_probs, 0)
    lane = lax.broadcasted_iota(jnp.int32, (TQ, LANES), 1)
    for pair in range(2):
        o_ref[:, pair * LANES:(pair + 1) * LANES] = jnp.where(
            lane < HEAD_DIM, _softmax_out(2 * pair, ls_ref, acc_ref),
            _softmax_out(2 * pair + 1, ls_ref, acc_ref)).astype(BF16)


def _mla_attn(mq, mk, mv, batch, seq):
    r3 = lambda a: a.reshape(batch, seq, a.shape[-1])
    return pl.pallas_call(
        _mla_attn_kernel,
        grid=(batch, seq // TQ),
        in_specs=[pl.BlockSpec((None, TQ, MLA_HEADS * LANES), lambda b, i: (b, i, 0)),
                  pl.BlockSpec((None, seq, MLA_HEADS * LANES), lambda b, i: (b, 0, 0)),
                  pl.BlockSpec((None, seq, HEADS_W), lambda b, i: (b, 0, 0))],
        out_specs=pl.BlockSpec((None, TQ, HEADS_W), lambda b, i: (b, i, 0)),
        out_shape=jax.ShapeDtypeStruct((batch, seq, HEADS_W), BF16),
        scratch_shapes=_softmax_scratch(seq // TK),
        compiler_params=_cparams(("parallel", "parallel")),
        name="mla_attention",
    )(r3(mq), r3(mk), r3(mv))


def _sb_attn_kernel(q_ref, k_ref, v_ref, tri_ref, o_ref, qm_ref, e_ref, tail_ref, acc_ref):
    i = pl.program_id(1)
    row, col = _tile_iotas()
    for pair in range(2):
        qp = q_ref[:, pair * LANES:(pair + 1) * LANES].astype(F32)
        for sub in range(2):
            qm_ref[2 * pair + sub] = _half_select(sub, qp).astype(BF16)

    def log_weights(kb, diagonal):
        k0 = pl.multiple_of(kb * TK, TK)
        tri2 = tri_ref[...]
        heads = range(N_HEADS)
        zs = [_dot_nt(qm_ref[h], k_ref[pl.ds(k0, TK), (h // 2) * LANES:(h // 2 + 1) * LANES]) for h in heads]
        drops = [jnp.maximum(z, 0.0) + jnp.log(1.0 + jnp.exp(-jnp.abs(z))) for z in zs]
        if diagonal:
            drops = [jnp.where(col < row, d, 0.0) for d in drops]
        incls = [_dot(jnp.concatenate(_split_bf16(d), axis=1), tri2) for d in drops]
        for h in heads:
            total = jnp.broadcast_to(incls[h][:, 0:1], (TQ, LANES))
            if diagonal:
                e_ref[h, kb] = jnp.where(col < row, zs[h] - incls[h], NEG)
                tail_ref[h] = total
            else:
                tail = tail_ref[h]
                e_ref[h, kb] = zs[h] - incls[h] - jnp.concatenate([tail, tail], axis=1)
                tail_ref[h] = tail + total

    log_weights(i, True)

    def off_diagonal(step, _):
        log_weights(i - 1 - step, False)
        return 0

    lax.fori_loop(0, i, off_diagonal, 0)

    def weighted_values(kb, first):
        k0 = pl.multiple_of(kb * TK, TK)
        for h in range(N_HEADS):
            ps = slice((h // 2) * LANES, (h // 2 + 1) * LANES)
            av = _dot(jnp.exp(e_ref[h, kb]).astype(BF16), v_ref[pl.ds(k0, TK), ps])
            if first:
                acc_ref[h] = av
            else:
                acc_ref[h] += av

    def off_diagonal_values(kb, _):
        weighted_values(kb, False)
        return 0

    weighted_values(i, True)
    lax.fori_loop(0, i, off_diagonal_values, 0)
    lane = lax.broadcasted_iota(jnp.int32, (TQ, LANES), 1)
    for pair in range(2):
        o_ref[:, pair * LANES:(pair + 1) * LANES] = jnp.where(
            lane < HEAD_DIM, acc_ref[2 * pair], acc_ref[2 * pair + 1]).astype(BF16)


def _sb_attn(sq, sk, sv, tri, batch, seq):
    r3 = lambda a: a.reshape(batch, seq, a.shape[-1])
    kspec = pl.BlockSpec((None, seq, HEADS_W), lambda b, i: (b, 0, 0))
    qspec = pl.BlockSpec((None, TQ, HEADS_W), lambda b, i: (b, i, 0))
    return pl.pallas_call(
        _sb_attn_kernel,
        grid=(batch, seq // TQ),
        in_specs=[qspec, kspec, kspec, _const_spec(tri.shape)],
        out_specs=qspec,
        out_shape=jax.ShapeDtypeStruct((batch, seq, HEADS_W), BF16),
        scratch_shapes=[pltpu.VMEM((N_HEADS, TQ, LANES), BF16), pltpu.VMEM((N_HEADS, seq // TK, TQ, TK), F32),
                        pltpu.VMEM((N_HEADS, TQ, LANES), F32), pltpu.VMEM((N_HEADS, TQ, LANES), F32)],
        compiler_params=_cparams(("parallel", "parallel")),
        name="stick_breaking_attention",
    )(r3(sq), r3(sk), r3(sv), tri)


def _merge_kernel(x_ref, ya_ref, yb_ref, yc_ref, yd_ref, wg_ref, wb_ref, wo_ref, g_ref, b_ref,
                  o_ref):
    x = x_ref[...]
    xb = x.astype(BF16)
    mixed = jnp.zeros((TM, D_MODEL), F32)
    for n, y_ref in enumerate((ya_ref, yb_ref, yc_ref, yd_ref)):
        gate = _sigmoid(_dot(xb, wg_ref[:, n * D_MODEL:(n + 1) * D_MODEL]))
        mixed = mixed + gate * _dot(y_ref[...], wb_ref[n])
    h = DEEPNORM_ALPHA * x + _dot(mixed.astype(BF16), wo_ref[...])
    o_ref[...] = _layer_norm(h, g_ref[...], b_ref[...])


def _merge(x2d, ys, wts):
    n = x2d.shape[0]
    row = lambda w: pl.BlockSpec((TM, w), lambda i: (i, 0))
    consts = [wts['w_gate'], wts['w_branch'], wts['w_out'], wts['ln1_g'], wts['ln1_b']]
    return pl.pallas_call(
        _merge_kernel,
        grid=(n // TM,),
        in_specs=[row(D_MODEL)] + [row(BRANCH_W)] * 4 + [_const_spec(c.shape) for c in consts],
        out_specs=row(D_MODEL),
        out_shape=jax.ShapeDtypeStruct((n, D_MODEL), F32),
        compiler_params=_cparams(("parallel",)),
        name="branch_merge_ln1",
    )(x2d, *ys, *consts)


def _ple_ln2(x1, x1b, f, p_ref, wpg_ref, wpp_ref, g_ref, b_ref):
    ple = _sigmoid(_dot(x1b, wpg_ref[...])) * _dot(p_ref[...].astype(BF16), wpp_ref[...])
    return _layer_norm(DEEPNORM_ALPHA * x1 + f + ple, g_ref[...], b_ref[...])


FF_CHUNK = 256


def _ffn_dense_kernel(x_ref, p_ref, wi_ref, wo_ref, wpg_ref, wpp_ref, g_ref, b_ref, o_ref, acc_ref):
    xb = x_ref[...].astype(BF16)
    for c in range(D_FF // FF_CHUNK):
        a = _dot(xb, wi_ref[:, c * FF_CHUNK:(c + 1) * FF_CHUNK])
        u = _dot(xb, wi_ref[:, D_FF + c * FF_CHUNK:D_FF + (c + 1) * FF_CHUNK])
        part = _dot((_silu(a) * u).astype(BF16), wo_ref[c * FF_CHUNK:(c + 1) * FF_CHUNK, :])
        if c == 0:
            acc_ref[...] = part
        else:
            acc_ref[...] += part
    o_ref[...] = _ple_ln2(x_ref[...], xb, acc_ref[...], p_ref, wpg_ref, wpp_ref, g_ref, b_ref)


def _ffn_dense(x1, p2d, wts):
    n = x1.shape[0]
    row = lambda w: pl.BlockSpec((TF, w), lambda i: (i, 0))
    consts = [wts['ffn_w_in'], wts['ffn_w_out'], wts['ple_w_gate'], wts['ple_w_proj'], wts['ln2_g'], wts['ln2_b']]
    return pl.pallas_call(
        _ffn_dense_kernel,
        grid=(n // TF,),
        in_specs=[row(D_MODEL), row(P_DIM)] + [_const_spec(c.shape) for c in consts],
        out_specs=row(D_MODEL),
        out_shape=jax.ShapeDtypeStruct((n, D_MODEL), F32),
        scratch_shapes=[pltpu.VMEM((TF, D_MODEL), F32)],
        compiler_params=_cparams(("parallel",)),
        name="ffn_dense_ple_ln2",
    )(x1, p2d, *consts)


TR = 512


def _moe_route_kernel(x_ref, wr_ref, tri_ref, info_ref, cnt_ref, run_ref):
    @pl.when(pl.program_id(0) == 0)
    def _():
        run_ref[...] = jnp.zeros_like(run_ref)

    xh, xl = _split_bf16(x_ref[...])
    wh, wl = _split_bf16(wr_ref[...])
    logits = _dot(xh, wh) + (_dot(xl, wh) + _dot(xh, wl))
    lane = lax.broadcasted_iota(jnp.int32, (TR, LANES), 1)
    lane_f = lane.astype(F32)
    logits = jnp.where(lane < N_EXPERTS, logits, NEG)
    m1 = jnp.max(logits, axis=-1, keepdims=True)
    i1 = jnp.min(jnp.where(logits == m1, lane_f, float(LANES)), axis=-1, keepdims=True)
    rest = jnp.where(lane_f == i1, NEG, logits)
    m2 = jnp.max(rest, axis=-1, keepdims=True)
    i2 = jnp.min(jnp.where(rest == m2, lane_f, float(LANES)), axis=-1, keepdims=True)
    e = jnp.exp(m2 - m1)
    g1 = 1.0 / (1.0 + e)
    g2 = e / (1.0 + e)
    hot1 = lane_f == i1
    hot2 = lane_f == i2
    onehot = jnp.where(hot1 | hot2, 1.0, 0.0)
    before = _dot(tri_ref[...], onehot.astype(BF16)) + run_ref[0:1, :]
    r1 = jnp.sum(jnp.where(hot1, before, 0.0), axis=-1, keepdims=True)
    r2 = jnp.sum(jnp.where(hot2, before, 0.0), axis=-1, keepdims=True)
    run_ref[0:1, :] = run_ref[0:1, :] + jnp.sum(onehot, axis=0, keepdims=True)
    info = jnp.zeros((TR, LANES), F32)
    for k, val in enumerate((i1, i2, r1, r2, g1, g2)):
        info = jnp.where(lane == k, val, info)
    info_ref[...] = info
    cnt_ref[...] = jnp.broadcast_to(run_ref[0:1, :], cnt_ref.shape)


def _moe_route(x1, w_router_pad, tri):
    n = x1.shape[0]
    return pl.pallas_call(
        _moe_route_kernel,
        grid=(n // TR,),
        in_specs=[pl.BlockSpec((TR, D_MODEL), lambda i: (i, 0)), _const_spec(w_router_pad.shape),
                  _const_spec(tri.shape)],
        out_specs=[pl.BlockSpec((TR, LANES), lambda i: (i, 0)), pl.BlockSpec((8, LANES), lambda i: (0, 0))],
        out_shape=[jax.ShapeDtypeStruct((n, LANES), F32), jax.ShapeDtypeStruct((8, LANES), F32)],
        scratch_shapes=[pltpu.VMEM((8, LANES), F32)],
        compiler_params=_cparams(("arbitrary",)),
        name="moe_router_rank",
    )(x1, w_router_pad, tri)


def _moe_rowmap_kernel(d1_ref, d2_ref, rt_ref):
    def clear(r, _):
        rt_ref[r] = 0
        return 0

    lax.fori_loop(0, rt_ref.shape[0], clear, 0, unroll=16)

    def place(t, _):
        rt_ref[d1_ref[t]] = t
        rt_ref[d2_ref[t]] = t
        return 0

    lax.fori_loop(0, d1_ref.shape[0], place, 0, unroll=16)


def _moe_rowmap(dest1, dest2, n_rows):
    smem = pl.BlockSpec(memory_space=pltpu.SMEM)
    return pl.pallas_call(
        _moe_rowmap_kernel,
        in_specs=[smem, smem],
        out_specs=smem,
        out_shape=jax.ShapeDtypeStruct((n_rows,), jnp.int32),
        name="moe_row_map",
    )(dest1, dest2)


EF_CHUNK = 1792
EF_STEPS = D_FF_EXPERT // EF_CHUNK
assert EF_STEPS >= 2
ROWS_PER_STEP = MOE_BLOCK // EF_STEPS


def _moe_ffn_kernel(be_ref, na_ref, rt_ref, x_hbm, wa_ref, wu_ref, wo_ref, ys_ref, xs_ref, acc_ref, sem):
    del be_ref
    blk, c = pl.program_id(0), pl.program_id(1)
    n_live = na_ref[0]
    last_step = pl.num_programs(1) - 1
    slot = blk % 2

    def row_copy(block, r, s):
        return pltpu.make_async_copy(x_hbm.at[pl.ds(rt_ref[block * MOE_BLOCK + r], 1), :],
                                     xs_ref.at[s, pl.ds(r, 1), :], sem.at[s])

    def wait_block(s):
        pltpu.make_async_copy(x_hbm.at[pl.ds(0, MOE_BLOCK), :], xs_ref.at[s], sem.at[s]).wait()

    @pl.when(blk < n_live)
    def _():
        @pl.when((blk == 0) & (c == 0))
        def _():
            def body(r, _):
                row_copy(0, r, 0).start()
                return 0
            lax.fori_loop(0, MOE_BLOCK, body, 0)

        @pl.when(c == 0)
        def _():
            wait_block(slot)

        xb = xs_ref[slot].astype(BF16)
        h = (_silu(_dot(xb, wa_ref[...])) * _dot(xb, wu_ref[...])).astype(BF16)
        part = _dot(h, wo_ref[...])

        nxt = jnp.minimum(blk + 1, pl.num_programs(0) - 1)
        for r in range(ROWS_PER_STEP):
            row_copy(nxt, c * ROWS_PER_STEP + r, 1 - slot).start()

        @pl.when(c == 0)
        def _():
            acc_ref[...] = part

        @pl.when((c > 0) & (c < last_step))
        def _():
            acc_ref[...] += part

        @pl.when(c == last_step)
        def _():
            ys_ref[...] = acc_ref[...] + part

        @pl.when((c == last_step) & (blk == n_live - 1))
        def _():
            wait_block(1 - slot)

    @pl.when((blk >= n_live) & (c == last_step))
    def _():
        ys_ref[...] = jnp.zeros_like(ys_ref)


def _moe_ffn(x1, row_tok, blk_expert, n_active, w_in, w_out):
    n_rows = row_tok.shape[0]
    n_blk = n_rows // MOE_BLOCK
    n_ch = EF_STEPS
    live = lambda b, na: jnp.minimum(b, na[0] - 1)
    chunk = lambda b, c, na: jnp.where(b < na[0], c, n_ch - 1)
    return pl.pallas_call(
        _moe_ffn_kernel,
        grid_spec=pltpu.PrefetchScalarGridSpec(
            num_scalar_prefetch=3,
            grid=(n_blk, n_ch),
            in_specs=[
                pl.BlockSpec(memory_space=pl.ANY),
                pl.BlockSpec((None, D_MODEL, EF_CHUNK),
                             lambda b, c, be, na, rt: (be[live(b, na)], 0, chunk(b, c, na))),
                pl.BlockSpec((None, D_MODEL, EF_CHUNK),
                             lambda b, c, be, na, rt: (be[live(b, na)], 0, n_ch + chunk(b, c, na))),
                pl.BlockSpec((None, EF_CHUNK, D_MODEL),
                             lambda b, c, be, na, rt: (be[live(b, na)], chunk(b, c, na), 0)),
            ],
            out_specs=pl.BlockSpec((MOE_BLOCK, D_MODEL), lambda b, c, be, na, rt: (b, 0)),
            scratch_shapes=[pltpu.VMEM((2, MOE_BLOCK, D_MODEL), F32), pltpu.VMEM((MOE_BLOCK, D_MODEL), F32),
                            pltpu.SemaphoreType.DMA((2,))],
        ),
        out_shape=jax.ShapeDtypeStruct((n_rows, D_MODEL), F32),
        compiler_params=_cparams(("arbitrary", "arbitrary")),
        name="moe_expert_swiglu",
    )(blk_expert, n_active, row_tok, x1, w_in, w_in, w_out)


TC = 256


def _moe_combine_kernel(d1_ref, d2_ref, ys_hbm, x_ref, p_ref, info_ref, wpg_ref, wpp_ref, g_ref,
                        b_ref, o_ref, ya_ref, yb_ref, sem):
    i = pl.program_id(0)
    last = pl.num_programs(0) - 1
    slot = i % 2

    def copies(tile, r, s):
        t = tile * TC + r
        return (pltpu.make_async_copy(ys_hbm.at[pl.ds(d1_ref[t], 1), :], ya_ref.at[s, pl.ds(r, 1), :], sem.at[0, s]),
                pltpu.make_async_copy(ys_hbm.at[pl.ds(d2_ref[t], 1), :], yb_ref.at[s, pl.ds(r, 1), :], sem.at[1, s]))

    def wait_tile(s):
        pltpu.make_async_copy(ys_hbm.at[pl.ds(0, TC), :], ya_ref.at[s], sem.at[0, s]).wait()
        pltpu.make_async_copy(ys_hbm.at[pl.ds(0, TC), :], yb_ref.at[s], sem.at[1, s]).wait()

    @pl.when(i == 0)
    def _():
        def body(r, _):
            for cp in copies(0, r, 0):
                cp.start()
            return 0
        lax.fori_loop(0, TC, body, 0)

    wait_tile(slot)
    nxt = jnp.minimum(i + 1, last)
    for r in range(TC):
        for cp in copies(nxt, r, 1 - slot):
            cp.start()
    info = info_ref[...]
    f = info[:, 4:5] * ya_ref[slot] + info[:, 5:6] * yb_ref[slot]
    x = x_ref[...]
    o_ref[...] = _ple_ln2(x, x.astype(BF16), f, p_ref, wpg_ref, wpp_ref, g_ref, b_ref)

    @pl.when(i == last)
    def _():
        wait_tile(1 - slot)


def _moe_combine(ys, dest1, dest2, x1, p2d, info, wts):
    n = x1.shape[0]
    row = lambda w: pl.BlockSpec((TC, w), lambda i, d1, d2: (i, 0))
    consts = [wts['ple_w_gate'], wts['ple_w_proj'], wts['ln2_g'], wts['ln2_b']]
    cspec = lambda c: pl.BlockSpec(c.shape, lambda i, d1, d2: (0,) * c.ndim, pipeline_mode=pl.Buffered(1))
    return pl.pallas_call(
        _moe_combine_kernel,
        grid_spec=pltpu.PrefetchScalarGridSpec(
            num_scalar_prefetch=2,
            grid=(n // TC,),
            in_specs=[pl.BlockSpec(memory_space=pl.ANY), row(D_MODEL), row(P_DIM), row(LANES)]
            + [cspec(c) for c in consts],
            out_specs=row(D_MODEL),
            scratch_shapes=[pltpu.VMEM((2, TC, D_MODEL), F32), pltpu.VMEM((2, TC, D_MODEL), F32),
                            pltpu.SemaphoreType.DMA((2, 2))],
        ),
        out_shape=jax.ShapeDtypeStruct((n, D_MODEL), F32),
        compiler_params=_cparams(("arbitrary",)),
        name="moe_combine_ple_ln2",
    )(dest1, dest2, ys, x1, p2d, info, *consts)


def _moe(x1, p2d, wts):
    n = x1.shape[0]
    n_rows = ((n * 2 + MOE_BLOCK - 1) // MOE_BLOCK) * MOE_BLOCK + N_EXPERTS * MOE_BLOCK
    info, cnt = _moe_route(x1, wts['w_router'], wts['tri_tokens'])
    counts = cnt[0, :N_EXPERTS].astype(jnp.int32)
    padded = ((counts + MOE_BLOCK - 1) // MOE_BLOCK) * MOE_BLOCK
    ends = jnp.cumsum(padded)
    start_pad = ends - padded
    e1, e2 = info[:, 0].astype(jnp.int32), info[:, 1].astype(jnp.int32)
    dest1 = start_pad[e1] + info[:, 2].astype(jnp.int32)
    dest2 = start_pad[e2] + info[:, 3].astype(jnp.int32)
    n_blk = n_rows // MOE_BLOCK
    blk_row0 = jnp.arange(n_blk, dtype=jnp.int32) * MOE_BLOCK
    blk_expert = jnp.minimum(jnp.sum((ends[None, :] <= blk_row0[:, None]).astype(jnp.int32), axis=1),
                             N_EXPERTS - 1)
    n_active = (ends[-1:] // MOE_BLOCK).astype(jnp.int32)
    row_tok = _moe_rowmap(dest1, dest2, n_rows)
    ys = _moe_ffn(x1, row_tok, blk_expert, n_active, wts['moe_w_in'], wts['moe_w_out'])
    return _moe_combine(ys, dest1, dest2, x1, p2d, info, wts)


def _rot_half_cols(w, heads, dim):
    w3 = w.reshape(w.shape[0], heads, dim)
    half = dim // 2
    return jnp.concatenate([-w3[..., half:], w3[..., :half]], axis=-1).reshape(w.shape[0], heads * dim)


def _prep_layer(i, w_in, conv_w, conv_b, conv_ln_g, conv_ln_b, nsa_cmp_pe, nsa_cmp_w1, nsa_cmp_w2,
                mla_q_norm, mla_kv_norm, mla_w_uq, mla_w_ukv, w_branch, w_out, ln1_g, ln1_b,
                ple_w_gate, ple_w_proj, ln2_g, ln2_b):
    w = w_in[i]
    d = w.shape[0]
    z = lambda n: jnp.zeros((d, n), F32)
    dup = lambda a: jnp.concatenate([a, a], axis=1)
    c_glu, nq = w[:, 0:512], w[:, 512:768]
    nkv = w[:, 768:1152]
    k_cmp, v_cmp, k_slc, v_slc, k_win, v_win = [nkv[:, j * 64:(j + 1) * 64] for j in range(6)]
    ng = w[:, 1152:1164]
    mq, mkv, mkr = w[:, 1164:1420], w[:, 1420:1548], w[:, 1548:1580]
    sb = w[:, 1580:2348]
    bg = w[:, 2348:6444]
    cols = [c_glu, nq, _rot_half_cols(nq, NSA_HEADS, HEAD_DIM),
            dup(k_slc), dup(_rot_half_cols(k_slc, 1, HEAD_DIM)),
            dup(k_win), dup(_rot_half_cols(k_win, 1, HEAD_DIM)),
            dup(v_slc), dup(v_win), k_cmp, v_cmp, ng, z(LANES - 12), mq, mkv,
            z(64), mkr, z(32), z(64), _rot_half_cols(mkr, 1, MLA_ROPE), z(32), z(LANES), sb]
    w1 = jnp.concatenate(cols, axis=1).astype(BF16)
    assert w1.shape[1] == C_TOT

    inv32 = ROPE_THETA ** (-jnp.arange(HEAD_DIM // 2, dtype=F32) / (HEAD_DIM // 2))
    inv16 = ROPE_THETA ** (-jnp.arange(MLA_ROPE // 2, dtype=F32) / (MLA_ROPE // 2))
    inv_nsa = jnp.tile(inv32, 4)[None, :]
    inv_mla = jnp.concatenate([jnp.zeros((64,), F32), inv16, inv16, jnp.zeros((32,), F32)])[None, :]

    wuq = mla_w_uq[i].reshape(MLA_Q_RANK, MLA_HEADS, MLA_NOPE + MLA_ROPE)
    zq = jnp.zeros((MLA_Q_RANK, MLA_HEADS, 32), F32)
    wq = jnp.concatenate([wuq, zq], axis=-1).reshape(MLA_Q_RANK, MLA_HEADS * LANES)
    rope_rot = jnp.concatenate([-wuq[..., MLA_NOPE + 16:], wuq[..., MLA_NOPE:MLA_NOPE + 16]], axis=-1)
    wqr = jnp.concatenate([jnp.zeros((MLA_Q_RANK, MLA_HEADS, MLA_NOPE), F32), rope_rot, zq],
                          axis=-1).reshape(MLA_Q_RANK, MLA_HEADS * LANES)
    wukv = mla_w_ukv[i].reshape(MLA_KV_RANK, MLA_HEADS, MLA_NOPE + MLA_V)
    wk = jnp.concatenate([wukv[..., :MLA_NOPE], jnp.zeros((MLA_KV_RANK, MLA_HEADS, 64), F32)],
                         axis=-1).reshape(MLA_KV_RANK, MLA_HEADS * LANES)
    wv = wukv[..., MLA_NOPE:].reshape(MLA_KV_RANK, MLA_HEADS * MLA_V)

    pe = nsa_cmp_pe[i]
    pe_rows = pe.reshape(CMP_BLOCK, 2 * HEAD_DIM)
    pe_a = pe_rows[:CMP_STRIDE].reshape(1, CMP_STRIDE * LANES)
    pe_b = pe_rows[CMP_STRIDE:].reshape(1, CMP_STRIDE * LANES)
    w1c = nsa_cmp_w1[i].reshape(2, CMP_BLOCK, HEAD_DIM, HEAD_DIM)
    zblk = jnp.zeros((CMP_BLOCK, HEAD_DIM, HEAD_DIM), F32)
    w1full = jnp.concatenate([jnp.concatenate([w1c[0], zblk], axis=2),
                              jnp.concatenate([zblk, w1c[1]], axis=2)], axis=1)
    w1a = w1full[:CMP_STRIDE].reshape(CMP_STRIDE * LANES, LANES).astype(BF16)
    w1b = w1full[CMP_STRIDE:].reshape(CMP_STRIDE * LANES, LANES).astype(BF16)
    w2 = nsa_cmp_w2[i]
    z64 = jnp.zeros((HEAD_DIM, LANES), F32)
    w2k = jnp.concatenate([dup(w2[0]), z64], axis=0).astype(BF16)
    w2v = jnp.concatenate([z64, dup(w2[1])], axis=0).astype(BF16)

    return dict(
        w1=w1, inv_nsa=inv_nsa, inv_mla=inv_mla,
        mla_qn=mla_q_norm[i][None, :], mla_kvn=mla_kv_norm[i][None, :],
        wq=wq.astype(BF16), wqr=wqr.astype(BF16), wk=wk.astype(BF16), wv=wv.astype(BF16),
        conv_w=conv_w[i], conv_b=conv_b[i][None, :], conv_g=conv_ln_g[i][None, :], conv_beta=conv_ln_b[i][None, :],
        pe_a=pe_a, pe_b=pe_b, w1a=w1a, w1b=w1b, w2k=w2k, w2v=w2v,
        w_gate=bg.astype(BF16), w_branch=w_branch[i].astype(BF16), w_out=w_out[i].astype(BF16),
        ln1_g=ln1_g[i][None, :], ln1_b=ln1_b[i][None, :],
        ple_w_gate=ple_w_gate[i].astype(BF16), ple_w_proj=ple_w_proj[i].astype(BF16),
        ln2_g=ln2_g[i][None, :], ln2_b=ln2_b[i][None, :],
    )


def _tables(seq):
    n_cmp_rows = seq // CMP_STRIDE
    n_sel = seq // SEL_BLOCK
    cmp_start = jnp.arange(n_cmp_rows) * CMP_STRIDE
    sel_start = jnp.arange(LANES) * SEL_BLOCK
    n_cmp = (seq - CMP_BLOCK) // CMP_STRIDE + 1
    overlap = ((cmp_start[:, None] < sel_start[None, :] + SEL_BLOCK)
               & (cmp_start[:, None] + CMP_BLOCK > sel_start[None, :])
               & (jnp.arange(n_cmp_rows)[:, None] < n_cmp) & (jnp.arange(LANES)[None, :] < n_sel))
    kb = jnp.arange(seq // TK)[:, None, None]
    nn = jnp.arange(LANES)[None, :, None]
    ll = jnp.arange(TK)[None, None, :]
    expand = (kb * TK + ll) // SEL_BLOCK == nn
    jj = jnp.arange(TK)
    tri_keys = jnp.tile(jj[:, None] >= jj[None, :], (2, 1))
    tt = jnp.arange(TR)
    tri_tokens = tt[None, :] < tt[:, None]
    src = jnp.arange(LANES)[:, None]
    dst = jnp.arange(3 * N_HEADS * HEAD_DIM)[None, :]
    width = N_HEADS * HEAD_DIM
    gate_expand = jnp.tile((src < 3 * N_HEADS) & (src % 3 == dst // width) & (src // 3 == (dst % width) // HEAD_DIM),
                           (2, 1))
    return dict(overlap_t=overlap.T.astype(BF16), eye_q=jnp.eye(TQ, dtype=BF16), gate_expand=gate_expand.astype(BF16),
                expand=expand.astype(BF16), tri_keys=tri_keys.astype(BF16),
                tri_tokens=tri_tokens.astype(BF16))


def kernel(x, p, positions, w_in, conv_w, conv_b, conv_ln_g, conv_ln_b, nsa_cmp_pe, nsa_cmp_w1, nsa_cmp_w2,
           mla_q_norm, mla_kv_norm, mla_w_uq, mla_w_ukv, w_branch, w_out, ln1_g, ln1_b, ffn_w_in, ffn_w_out,
           moe_router, moe_w_in, moe_w_out, ple_w_gate, ple_w_proj, ln2_g, ln2_b):
    batch, seq, _ = x.shape
    n = batch * seq
    tabs = _tables(seq)
    x2d = x.reshape(n, D_MODEL)
    pos2d = positions.reshape(n, 1)
    for i in range(DEPTH):
        wts = _prep_layer(i, w_in, conv_w, conv_b, conv_ln_g, conv_ln_b, nsa_cmp_pe, nsa_cmp_w1, nsa_cmp_w2,
                          mla_q_norm, mla_kv_norm, mla_w_uq, mla_w_ukv, w_branch, w_out, ln1_g, ln1_b,
                          ple_w_gate, ple_w_proj, ln2_g, ln2_b)
        wts['overlap_t'], wts['eye_q'] = tabs['overlap_t'], tabs['eye_q']
        (conv_in, nq, nqr, ks, kw, vs, vw, kvc, ng, mq, mk, mv, sq, sk, sv) = _mixer_in(x2d, pos2d, wts)
        y_a = _conv(conv_in, wts['conv_w'], wts['conv_b'], wts['conv_g'], wts['conv_beta'], batch, seq)
        ocmp, sel = _nsa_cmp(kvc, nq, wts, batch, seq)
        y_b = _nsa_attn(nqr, ks, vs, kw, vw, sel, tabs['expand'], ocmp, ng, tabs['gate_expand'], batch, seq)
        y_c = _mla_attn(mq, mk, mv, batch, seq)
        y_d = _sb_attn(sq, sk, sv, tabs['tri_keys'], batch, seq)
        ys = [y.reshape(n, BRANCH_W) for y in (y_a, y_b, y_c, y_d)]
        x1 = _merge(x2d, ys, wts)
        p2d = p[i].reshape(n, P_DIM)
        if i % 2 == 0:
            wts['ffn_w_in'] = ffn_w_in[i // 2].astype(BF16)
            wts['ffn_w_out'] = ffn_w_out[i // 2].astype(BF16)
            x2d = _ffn_dense(x1, p2d, wts)
        else:
            wts['w_router'] = jnp.concatenate(
                [moe_router[i // 2], jnp.zeros((D_MODEL, LANES - N_EXPERTS), F32)], axis=1)
            wts['tri_tokens'] = tabs['tri_tokens']
            wts['moe_w_in'] = moe_w_in[i // 2].astype(BF16)
            wts['moe_w_out'] = moe_w_out[i // 2].astype(BF16)
            x2d = _moe(x1, p2d, wts)
    return x2d.reshape(batch, seq, D_MODEL)
```

```python
import functools

import jax
import jax.numpy as jnp
from jax import lax
from jax.experimental import pallas as pl
from jax.experimental.pallas import tpu as pltpu

F32 = jnp.float32
BF16 = jnp.bfloat16

D_MODEL = 1024
DEPTH = 2
CONV_CH = 256
CONV_WIDTH = 31
NSA_HEADS = 4
HEAD_DIM = 64
CMP_BLOCK = 32
CMP_STRIDE = 16
SEL_BLOCK = 64
SEL_TOPN = 16
WINDOW = 512
MLA_HEADS = 4
MLA_Q_RANK = 256
MLA_KV_RANK = 128
MLA_NOPE = 64
MLA_ROPE = 32
MLA_V = 64
BRANCH_W = 256
ROPE_THETA = 10000.0
LN_EPS = 1e-5
RMS_EPS = 1e-6
D_FF = 2816
N_EXPERTS = 8
D_FF_EXPERT = 3584
MOE_BLOCK = 512
P_DIM = 256
DEEPNORM_ALPHA = (2 * DEPTH) ** 0.25

LANES = 128
SUBLANES = 8
MXU_W = 256
HEADS_W = 256
VMEM_LIMIT = 56 * 1024 * 1024

NEG = -1e30

C_CONV = 0
C_NQ = 512
C_NQR = 768
C_KS = 1024
C_KW = 1280
C_VS = 1536
C_KVC = 1792
C_MQ = 2048
C_MKV = 2304
C_MKRR = 2560
C_SB = 2816
C_TOT = 3584

TM = 256
TF = 512
TQ = 256
TK = 256


def _cparams(sem, vmem=VMEM_LIMIT):
    return pltpu.CompilerParams(dimension_semantics=sem, vmem_limit_bytes=vmem)


def _const_spec(shape):
    nd = len(shape)
    return pl.BlockSpec(shape, lambda *_: (0,) * nd, pipeline_mode=pl.Buffered(1))


def _dot(a, b):
    return jnp.dot(a, b, preferred_element_type=F32)


def _dot_nt(a, b):
    return lax.dot_general(a, b, (((1,), (1,)), ((), ())), preferred_element_type=F32)


def _layer_norm(h, g, b):
    mu = jnp.mean(h, axis=-1, keepdims=True)
    d = h - mu
    var = jnp.mean(d * d, axis=-1, keepdims=True)
    return d * lax.rsqrt(var + LN_EPS) * g + b


def _rms_norm(h, g):
    return h * lax.rsqrt(jnp.mean(h * h, axis=-1, keepdims=True) + RMS_EPS) * g


def _sigmoid(x):
    return 1.0 / (1.0 + jnp.exp(-x))


def _silu(x):
    return x * _sigmoid(x)


def _split_bf16(x):
    hi = x.astype(BF16)
    lo = (x - hi.astype(F32)).astype(BF16)
    return hi, lo


def _half_select(sub, x):
    lane = lax.broadcasted_iota(jnp.int32, x.shape, 1)
    keep = (lane < HEAD_DIM) if sub == 0 else (lane >= HEAD_DIM)
    return jnp.where(keep, x, 0.0)


def _mixer_in_kernel(x_ref, pos_ref, w_ref, invn_ref, invm_ref, qn_ref, kvn_ref, wq_ref, wqr_ref,
                     wk_ref, wv_ref,
                     conv_ref, nq_ref, nqr_ref, ks_ref, kw_ref, vs_ref, vw_ref, kvc_ref, ng_ref,
                     mq_ref, mk_ref, mv_ref, sq_ref, sk_ref, sv_ref):
    xb = x_ref[...].astype(BF16)

    def proj(c0, width):
        return _dot(xb, w_ref[:, c0:c0 + width])

    posf = pos_ref[...].astype(F32)
    ang_n = posf * invn_ref[...]
    cos_n, sin_n = jnp.cos(ang_n), jnp.sin(ang_n)
    ang_m = posf * invm_ref[...]
    cos_m, sin_m = jnp.cos(ang_m), jnp.sin(ang_m)

    lo, hi = slice(0, LANES), slice(LANES, 2 * LANES)
    qn = _rms_norm(proj(C_MQ, MLA_Q_RANK), qn_ref[...]).astype(BF16)
    mkv = proj(C_MKV, MXU_W)
    kvn = _rms_norm(mkv[:, lo], kvn_ref[...]).astype(BF16)

    conv_ref[...] = proj(C_CONV, 2 * CONV_CH)

    scale = HEAD_DIM ** -0.5
    cos_n2 = jnp.concatenate([cos_n, cos_n], axis=1)
    sin_n2 = jnp.concatenate([sin_n, sin_n], axis=1)
    q, qrot = proj(C_NQ, MXU_W), proj(C_NQR, MXU_W)
    nq_ref[...] = (q * scale).astype(BF16)
    nqr_ref[...] = ((q * cos_n2 + qrot * sin_n2) * scale).astype(BF16)
    k = proj(C_KS, MXU_W)
    ks_ref[...] = (k[:, lo] * cos_n + k[:, hi] * sin_n).astype(BF16)
    k = proj(C_KW, MXU_W)
    kw_ref[...] = (k[:, lo] * cos_n + k[:, hi] * sin_n).astype(BF16)
    v = proj(C_VS, MXU_W)
    vs_ref[...] = v[:, lo].astype(BF16)
    vw_ref[...] = v[:, hi].astype(BF16)
    u = proj(C_KVC, MXU_W)
    kvc_ref[...] = u[:, lo]
    ng_ref[...] = u[:, hi]

    kr = mkv[:, hi] * cos_m + proj(C_MKRR, MXU_W)[:, lo] * sin_m
    qa, qr, kk = _dot(qn, wq_ref[...]), _dot(qn, wqr_ref[...]), _dot(kvn, wk_ref[...])
    for h in range(MLA_HEADS):
        sl = slice(h * LANES, (h + 1) * LANES)
        mq_ref[:, sl] = (qa[:, sl] * cos_m + qr[:, sl] * sin_m).astype(BF16)
        mk_ref[:, sl] = (kk[:, sl] + kr).astype(BF16)
    mv_ref[...] = _dot(kvn, wv_ref[...]).astype(BF16)

    sq_ref[...] = (proj(C_SB, HEADS_W) * scale).astype(BF16)
    sk_ref[...] = proj(C_SB + HEADS_W, HEADS_W).astype(BF16)
    sv_ref[...] = proj(C_SB + 2 * HEADS_W, HEADS_W).astype(BF16)


def _mixer_in(x2d, pos2d, wts):
    n = x2d.shape[0]
    row = lambda w: pl.BlockSpec((TF, w), lambda i: (i, 0))
    mla_w = MLA_HEADS * LANES
    out_widths = [2 * CONV_CH, HEADS_W, HEADS_W, LANES, LANES, LANES, LANES, LANES, LANES, mla_w, mla_w,
                  HEADS_W, HEADS_W, HEADS_W, HEADS_W]
    out_dtypes = [F32, BF16, BF16, BF16, BF16, BF16, BF16, F32, F32, BF16, BF16, BF16, BF16, BF16, BF16]
    consts = [wts['w1'], wts['inv_nsa'], wts['inv_mla'], wts['mla_qn'], wts['mla_kvn'], wts['wq'],
              wts['wqr'], wts['wk'], wts['wv']]
    return pl.pallas_call(
        _mixer_in_kernel,
        grid=(n // TF,),
        in_specs=[row(D_MODEL), row(1)] + [_const_spec(c.shape) for c in consts],
        out_specs=[row(w) for w in out_widths],
        out_shape=[jax.ShapeDtypeStruct((n, w), d) for w, d in zip(out_widths, out_dtypes)],
        compiler_params=_cparams(("parallel",)),
        name="mixer_in",
    )(x2d, pos2d, *consts)


CONV_PAD = 32
CONV_CHUNK = 128


def _conv_kernel(u_ref, w_ref, b_ref, g_ref, beta_ref, o_ref, hp_ref):
    seq = u_ref.shape[0]
    hp_ref[0, 0:CONV_PAD, :] = jnp.zeros((CONV_PAD, CONV_CH), F32)
    hp_ref[0, CONV_PAD:CONV_PAD + seq, :] = u_ref[:, 0:CONV_CH] * _sigmoid(u_ref[:, CONV_CH:2 * CONV_CH])
    rows = seq + CONV_PAD - SUBLANES
    for s in range(1, SUBLANES):
        hp_ref[s, 0:rows, :] = hp_ref[0, s:s + rows, :]
    first = CONV_PAD - (CONV_WIDTH - 1)
    for c in range(seq // CONV_CHUNK):
        base = c * CONV_CHUNK
        acc = jnp.broadcast_to(b_ref[...], (CONV_CHUNK, CONV_CH))
        for j in range(CONV_WIDTH):
            a, s = divmod(first + j, SUBLANES)
            r0 = base + a * SUBLANES
            acc = acc + hp_ref[s, r0:r0 + CONV_CHUNK, :] * w_ref[j:j + 1, :]
        y = _layer_norm(acc, g_ref[...], beta_ref[...])
        o_ref[base:base + CONV_CHUNK, :] = _silu(y).astype(BF16)


def _conv(conv_in, w, b, g, beta, batch, seq):
    return pl.pallas_call(
        _conv_kernel,
        grid=(batch,),
        in_specs=[pl.BlockSpec((None, seq, 2 * CONV_CH), lambda i: (i, 0, 0)),
                  _const_spec(w.shape), _const_spec(b.shape), _const_spec(g.shape), _const_spec(beta.shape)],
        out_specs=pl.BlockSpec((None, seq, CONV_CH), lambda i: (i, 0, 0)),
        out_shape=jax.ShapeDtypeStruct((batch, seq, CONV_CH), BF16),
        scratch_shapes=[pltpu.VMEM((SUBLANES, CONV_PAD + seq, CONV_CH), F32)],
        compiler_params=_cparams(("parallel",)),
        name="conformer_conv",
    )(conv_in.reshape(batch, seq, 2 * CONV_CH), w, b, g, beta)


def _gelu_tanh(x):
    return 0.5 * x * (1.0 + jnp.tanh(0.7978845608028654 * (x + 0.044715 * x * x * x)))


SEL_SHIFT = 6


def _nsa_cmp_kernel(kvc_ref, q_ref, pea_ref, peb_ref, w1a_ref, w1b_ref, w2k_ref, w2v_ref, ovt_ref, eye_ref,
                    ocmp_ref, sel_ref, *, seq):
    n_cmp = (seq - CMP_BLOCK) // CMP_STRIDE + 1
    n_sel = seq // SEL_BLOCK
    nb = seq // CMP_STRIDE
    x2 = kvc_ref[...]
    xa = (x2 + pea_ref[...]).astype(BF16)
    xb = (x2 + peb_ref[...]).astype(BF16)
    ha = _dot(xa, w1a_ref[...])
    hb = _dot(xb, w1b_ref[...])
    hid = ha + pltpu.roll(hb, nb - 1, 0)
    hid = _gelu_tanh(hid).astype(BF16)
    kk = _dot(hid, w2k_ref[...]).astype(BF16)
    vv = _dot(hid, w2v_ref[...]).astype(BF16)
    ovt = ovt_ref[...]

    for c in range(seq // TQ):
        r0 = c * TQ
        t = r0 + lax.broadcasted_iota(jnp.int32, (TQ, LANES), 0)
        j = lax.broadcasted_iota(jnp.int32, (TQ, LANES), 1)
        valid = (j * CMP_STRIDE + CMP_BLOCK - 1 <= t) & (j < n_cmp)
        psum = jnp.zeros((TQ, LANES), F32)
        for pair in range(2):
            qp = q_ref[r0:r0 + TQ, pair * LANES:(pair + 1) * LANES].astype(F32)
            outs = []
            for sub in range(2):
                qm = _half_select(sub, qp).astype(BF16)
                s = jnp.where(valid, _dot_nt(qm, kk), NEG)
                m = jnp.max(s, axis=-1, keepdims=True)
                e = jnp.where(valid, jnp.exp(s - m), 0.0)
                den = jnp.sum(e, axis=-1, keepdims=True)
                p = e / jnp.where(den > 0, den, 1.0)
                psum = psum + p
                outs.append(_dot(p.astype(BF16), vv))
            ocmp_ref[r0:r0 + TQ, pair * LANES:(pair + 1) * LANES] = jnp.where(
                lax.broadcasted_iota(jnp.int32, (TQ, LANES), 1) < HEAD_DIM, outs[0], outs[1])
        p_hi, p_lo = _split_bf16(psum)
        imp = (_dot_nt(ovt, p_hi) + _dot_nt(ovt, p_lo))[0:n_sel, :]
        n = lax.broadcasted_iota(jnp.int32, (n_sel, TQ), 0)
        cur = jnp.right_shift(r0 + lax.broadcasted_iota(jnp.int32, (n_sel, TQ), 1), SEL_SHIFT)
        forced = (n == 0) | (n == cur) | (n == cur - 1)
        imp = jnp.where(forced, jnp.inf, imp)
        imp = jnp.where(n > cur, -jnp.inf, imp)
        rank = jnp.zeros((n_sel, TQ), F32)
        for n2 in range(n_sel):
            other = imp[n2:n2 + 1, :]
            ahead = (other > imp) | ((other == imp) & (n2 < n))
            rank = rank + jnp.where(ahead, 1.0, 0.0)
        sel_t = jnp.where((rank < SEL_TOPN) & (imp > -jnp.inf), 1.0, 0.0)
        sel_t = jnp.concatenate([sel_t, jnp.zeros((LANES - n_sel, TQ), F32)], axis=0).astype(BF16)
        sel_ref[r0:r0 + TQ, :] = _dot_nt(eye_ref[...], sel_t).astype(BF16)


def _nsa_cmp(kvc, nq, wts, batch, seq):
    nb = seq // CMP_STRIDE
    consts = [wts['pe_a'], wts['pe_b'], wts['w1a'], wts['w1b'], wts['w2k'], wts['w2v'], wts['overlap_t'],
              wts['eye_q']]
    return pl.pallas_call(
        functools.partial(_nsa_cmp_kernel, seq=seq),
        grid=(batch,),
        in_specs=[pl.BlockSpec((None, nb, CMP_STRIDE * LANES), lambda i: (i, 0, 0)),
                  pl.BlockSpec((None, seq, HEADS_W), lambda i: (i, 0, 0))] + [_const_spec(c.shape) for c in consts],
        out_specs=[pl.BlockSpec((None, seq, HEADS_W), lambda i: (i, 0, 0)),
                   pl.BlockSpec((None, seq, LANES), lambda i: (i, 0, 0))],
        out_shape=[jax.ShapeDtypeStruct((batch, seq, HEADS_W), F32),
                   jax.ShapeDtypeStruct((batch, seq, LANES), BF16)],
        compiler_params=_cparams(("parallel",)),
        name="nsa_compress_select",
    )(kvc.reshape(batch, nb, CMP_STRIDE * LANES), nq.reshape(batch, seq, HEADS_W), *consts)


LOG2E = 1.4426950408889634
N_HEADS = 4


def _softmax_scratch(n_tiles):
    slab = pltpu.VMEM((N_HEADS, TQ, LANES), F32)
    return [pltpu.VMEM((N_HEADS, n_tiles, TQ, TK), F32), slab, slab, slab, slab]


def _scores_put(h, t, s, s_ref, mx_ref, first):
    s_ref[h, t] = s
    m = jnp.maximum(s[:, :LANES], s[:, LANES:])
    mx_ref[h] = m if first else jnp.maximum(mx_ref[h], m)


def _row_max(mx_ref, mb_ref):
    for h in range(N_HEADS):
        mb_ref[h] = jnp.broadcast_to(jnp.max(mx_ref[h], axis=-1, keepdims=True), (TQ, LANES))


def _probs_accumulate(h, t, c, v_blk, s_ref, mb_ref, ls_ref, acc_ref, first):
    s, mb = s_ref[h, t], mb_ref[h]
    pa = jnp.exp2((s[:, :LANES] - mb) * c)
    pb = jnp.exp2((s[:, LANES:] - mb) * c)
    pv = _dot(jnp.concatenate([pa, pb], axis=1).astype(BF16), v_blk)
    if first:
        ls_ref[h] = pa + pb
        acc_ref[h] = pv
    else:
        ls_ref[h] += pa + pb
        acc_ref[h] += pv


def _lane_sum_dense(x):
    ones = jnp.ones((2 * LANES, LANES), BF16)
    return _dot(jnp.concatenate(_split_bf16(x), axis=1), ones)


def _softmax_out(h, ls_ref, acc_ref):
    return acc_ref[h] / _lane_sum_dense(ls_ref[h])


def _tile_iotas():
    return (lax.broadcasted_iota(jnp.int32, (TQ, TK), 0), lax.broadcasted_iota(jnp.int32, (TQ, TK), 1))


def _nsa_attn_kernel(q_ref, ks_ref, vs_ref, kw_ref, vw_ref, sel_ref, exp_ref, ocmp_ref, ng_ref, gate_ref, o_ref,
                     qm_ref, ss_ref, mxs_ref, mbs_ref, lss_ref, accs_ref,
                     sw_ref, mxw_ref, mbw_ref, lsw_ref, accw_ref):
    i = pl.program_id(1)
    row, col = _tile_iotas()
    sel = sel_ref[...]
    for pair in range(2):
        qp = q_ref[:, pair * LANES:(pair + 1) * LANES].astype(F32)
        for sub in range(2):
            qm_ref[2 * pair + sub] = _half_select(sub, qp).astype(BF16)

    def selected_scores(kb, diagonal):
        k0 = pl.multiple_of(kb * TK, TK)
        hit = _dot(sel, exp_ref[kb]) > 0.5
        if diagonal:
            hit = hit & (col <= row)
        bias = jnp.where(hit, 0.0, NEG)
        k_blk = ks_ref[pl.ds(k0, TK), :]
        for h in range(N_HEADS):
            _scores_put(h, kb, _dot_nt(qm_ref[h], k_blk) + bias, ss_ref, mxs_ref, diagonal)

    def window_scores(slot, mask):
        k0 = pl.multiple_of((i - 2 + slot) * TK, TK)
        k_blk = kw_ref[pl.ds(k0, TK), :]
        for h in range(N_HEADS):
            s = _dot_nt(qm_ref[h], k_blk)
            if mask is not None:
                s = jnp.where(mask, s, NEG)
            _scores_put(h, slot, s, sw_ref, mxw_ref, slot == 2)

    def off_diagonal(kb, _):
        selected_scores(kb, False)
        return 0

    selected_scores(i, True)
    window_scores(2, col <= row)
    lax.fori_loop(0, i, off_diagonal, 0)
    pl.when(i >= 2)(lambda: window_scores(0, col > row))
    pl.when(i >= 1)(lambda: window_scores(1, None))
    _row_max(mxs_ref, mbs_ref)
    _row_max(mxw_ref, mbw_ref)

    def selected_probs(kb, first):
        v_blk = vs_ref[pl.ds(pl.multiple_of(kb * TK, TK), TK), :]
        for h in range(N_HEADS):
            _probs_accumulate(h, kb, LOG2E, v_blk, ss_ref, mbs_ref, lss_ref, accs_ref, first)

    def window_probs(slot):
        v_blk = vw_ref[pl.ds(pl.multiple_of((i - 2 + slot) * TK, TK), TK), :]
        for h in range(N_HEADS):
            _probs_accumulate(h, slot, LOG2E, v_blk, sw_ref, mbw_ref, lsw_ref, accw_ref, slot == 2)

    def off_diagonal_probs(kb, _):
        selected_probs(kb, False)
        return 0

    selected_probs(i, True)
    window_probs(2)
    lax.fori_loop(0, i, off_diagonal_probs, 0)
    pl.when(i >= 2)(lambda: window_probs(0))
    pl.when(i >= 1)(lambda: window_probs(1))

    g = _sigmoid(ng_ref[...])
    gx = _dot(jnp.concatenate(_split_bf16(g), axis=1), gate_ref[...])
    lane = lax.broadcasted_iota(jnp.int32, (TQ, LANES), 1)

    def heads_out(ls_ref, acc_ref):
        return jnp.concatenate(
            [jnp.where(lane < HEAD_DIM, _softmax_out(2 * pair, ls_ref, acc_ref),
                       _softmax_out(2 * pair + 1, ls_ref, acc_ref)) for pair in range(2)], axis=1)

    width = N_HEADS * HEAD_DIM
    o_ref[...] = (gx[:, 0:width] * ocmp_ref[...] + gx[:, width:2 * width] * heads_out(lss_ref, accs_ref)
                  + gx[:, 2 * width:3 * width] * heads_out(lsw_ref, accw_ref)).astype(BF16)


def _nsa_attn(nqr, ks, vs, kw, vw, sel, expand, ocmp, ng, gate_expand, batch, seq):
    qspec = lambda w: pl.BlockSpec((None, TQ, w), lambda b, i: (b, i, 0))
    kspec = pl.BlockSpec((None, seq, LANES), lambda b, i: (b, 0, 0))
    r3 = lambda a: a.reshape(batch, seq, a.shape[-1])
    return pl.pallas_call(
        _nsa_attn_kernel,
        grid=(batch, seq // TQ),
        in_specs=[qspec(HEADS_W), kspec, kspec, kspec, kspec, qspec(LANES), _const_spec(expand.shape),
                  qspec(HEADS_W), qspec(LANES), _const_spec(gate_expand.shape)],
        out_specs=qspec(HEADS_W),
        out_shape=jax.ShapeDtypeStruct((batch, seq, HEADS_W), BF16),
        scratch_shapes=[pltpu.VMEM((N_HEADS, TQ, LANES), BF16)] + _softmax_scratch(seq // TK)
        + _softmax_scratch(WINDOW // TK + 1),
        compiler_params=_cparams(("parallel", "parallel")),
        name="nsa_select_window",
    )(r3(nqr), r3(ks), r3(vs), r3(kw), r3(vw), sel, expand, ocmp, r3(ng), gate_expand)


def _mla_attn_kernel(q_ref, k_ref, v_ref, o_ref, s_ref, mx_ref, mb_ref, ls_ref, acc_ref):
    i = pl.program_id(1)
    c = (MLA_NOPE + MLA_ROPE) ** -0.5 * LOG2E
    row, col = _tile_iotas()

    def scores(kb, diagonal):
        k0 = pl.multiple_of(kb * TK, TK)
        for h in range(N_HEADS):
            hs = slice(h * LANES, (h + 1) * LANES)
            s = _dot_nt(q_ref[:, hs], k_ref[pl.ds(k0, TK), hs])
            if diagonal:
                s = jnp.where(col <= row, s, NEG)
            _scores_put(h, kb, s, s_ref, mx_ref, diagonal)

    def off_diagonal(kb, _):
        scores(kb, False)
        return 0

    scores(i, True)
    lax.fori_loop(0, i, off_diagonal, 0)
    _row_max(mx_ref, mb_ref)

    def probs(kb, first):
        k0 = pl.multiple_of(kb * TK, TK)
        for h in range(N_HEADS):
            v_blk = v_ref[pl.ds(k0, TK), (h // 2) * LANES:(h // 2 + 1) * LANES]
            _probs_accumulate(h, kb, c, v_blk, s_ref, mb_ref, ls_ref, acc_ref, first)

    def off_diagonal_probs(kb, _):
        probs(kb, False)
        return 0

    probs(i, True)
    lax.fori_loop(0, i, off_diagonal_probs, 0)
    lane = lax.broadcasted_iota(jnp.int32, (TQ, LANES), 1)
    for pair in range(2):
        o_ref[:, pair * LANES:(pair + 1) * LANES] = jnp.where(
            lane < HEAD_DIM, _softmax_out(2 * pair, ls_ref, acc_ref),
            _softmax_out(2 * pair + 1, ls_ref, acc_ref)).astype(BF16)


def _mla_attn(mq, mk, mv, batch, seq):
    r3 = lambda a: a.reshape(batch, seq, a.shape[-1])
    return pl.pallas_call(
        _mla_attn_kernel,
        grid=(batch, seq // TQ),
        in_specs=[pl.BlockSpec((None, TQ, MLA_HEADS * LANES), lambda b, i: (b, i, 0)),
                  pl.BlockSpec((None, seq, MLA_HEADS * LANES), lambda b, i: (b, 0, 0)),
                  pl.BlockSpec((None, seq, HEADS_W), lambda b, i: (b, 0, 0))],
        out_specs=pl.BlockSpec((None, TQ, HEADS_W), lambda b, i: (b, i, 0)),
        out_shape=jax.ShapeDtypeStruct((batch, seq, HEADS_W), BF16),
        scratch_shapes=_softmax_scratch(seq // TK),
        compiler_params=_cparams(("parallel", "parallel")),
        name="mla_attention",
    )(r3(mq), r3(mk), r3(mv))


def _sb_attn_kernel(q_ref, k_ref, v_ref, tri_ref, o_ref, qm_ref, e_ref, tail_ref, acc_ref):
    i = pl.program_id(1)
    row, col = _tile_iotas()
    for pair in range(2):
        qp = q_ref[:, pair * LANES:(pair + 1) * LANES].astype(F32)
        for sub in range(2):
            qm_ref[2 * pair + sub] = _half_select(sub, qp).astype(BF16)

    def log_weights(kb, diagonal):
        k0 = pl.multiple_of(kb * TK, TK)
        tri2 = tri_ref[...]
        heads = range(N_HEADS)
        zs = [_dot_nt(qm_ref[h], k_ref[pl.ds(k0, TK), (h // 2) * LANES:(h // 2 + 1) * LANES]) for h in heads]
        drops = [jnp.maximum(z, 0.0) + jnp.log(1.0 + jnp.exp(-jnp.abs(z))) for z in zs]
        if diagonal:
            drops = [jnp.where(col < row, d, 0.0) for d in drops]
        incls = [_dot(jnp.concatenate(_split_bf16(d), axis=1), tri2) for d in drops]
        for h in heads:
            total = jnp.broadcast_to(incls[h][:, 0:1], (TQ, LANES))
            if diagonal:
                e_ref[h, kb] = jnp.where(col < row, zs[h] - incls[h], NEG)
                tail_ref[h] = total
            else:
                tail = tail_ref[h]
                e_ref[h, kb] = zs[h] - incls[h] - jnp.concatenate([tail, tail], axis=1)
                tail_ref[h] = tail + total

    log_weights(i, True)

    def off_diagonal(step, _):
        log_weights(i - 1 - step, False)
        return 0

    lax.fori_loop(0, i, off_diagonal, 0)

    def weighted_values(kb, first):
        k0 = pl.multiple_of(kb * TK, TK)
        for h in range(N_HEADS):
            ps = slice((h // 2) * LANES, (h // 2 + 1) * LANES)
            av = _dot(jnp.exp(e_ref[h, kb]).astype(BF16), v_ref[pl.ds(k0, TK), ps])
            if first:
                acc_ref[h] = av
            else:
                acc_ref[h] += av

    def off_diagonal_values(kb, _):
        weighted_values(kb, False)
        return 0

    weighted_values(i, True)
    lax.fori_loop(0, i, off_diagonal_values, 0)
    lane = lax.broadcasted_iota(jnp.int32, (TQ, LANES), 1)
    for pair in range(2):
        o_ref[:, pair * LANES:(pair + 1) * LANES] = jnp.where(
            lane < HEAD_DIM, acc_ref[2 * pair], acc_ref[2 * pair + 1]).astype(BF16)


def _sb_attn(sq, sk, sv, tri, batch, seq):
    r3 = lambda a: a.reshape(batch, seq, a.shape[-1])
    kspec = pl.BlockSpec((None, seq, HEADS_W), lambda b, i: (b, 0, 0))
    qspec = pl.BlockSpec((None, TQ, HEADS_W), lambda b, i: (b, i, 0))
    return pl.pallas_call(
        _sb_attn_kernel,
        grid=(batch, seq // TQ),
        in_specs=[qspec, kspec, kspec, _const_spec(tri.shape)],
        out_specs=qspec,
        out_shape=jax.ShapeDtypeStruct((batch, seq, HEADS_W), BF16),
        scratch_shapes=[pltpu.VMEM((N_HEADS, TQ, LANES), BF16), pltpu.VMEM((N_HEADS, seq // TK, TQ, TK), F32),
                        pltpu.VMEM((N_HEADS, TQ, LANES), F32), pltpu.VMEM((N_HEADS, TQ, LANES), F32)],
        compiler_params=_cparams(("parallel", "parallel")),
        name="stick_breaking_attention",
    )(r3(sq), r3(sk), r3(sv), tri)


def _merge_kernel(x_ref, ya_ref, yb_ref, yc_ref, yd_ref, wg_ref, wb_ref, wo_ref, g_ref, b_ref,
                  o_ref):
    x = x_ref[...]
    xb = x.astype(BF16)
    mixed = jnp.zeros((TM, D_MODEL), F32)
    for n, y_ref in enumerate((ya_ref, yb_ref, yc_ref, yd_ref)):
        gate = _sigmoid(_dot(xb, wg_ref[:, n * D_MODEL:(n + 1) * D_MODEL]))
        mixed = mixed + gate * _dot(y_ref[...], wb_ref[n])
    h = DEEPNORM_ALPHA * x + _dot(mixed.astype(BF16), wo_ref[...])
    o_ref[...] = _layer_norm(h, g_ref[...], b_ref[...])


def _merge(x2d, ys, wts):
    n = x2d.shape[0]
    row = lambda w: pl.BlockSpec((TM, w), lambda i: (i, 0))
    consts = [wts['w_gate'], wts['w_branch'], wts['w_out'], wts['ln1_g'], wts['ln1_b']]
    return pl.pallas_call(
        _merge_kernel,
        grid=(n // TM,),
        in_specs=[row(D_MODEL)] + [row(BRANCH_W)] * 4 + [_const_spec(c.shape) for c in consts],
        out_specs=row(D_MODEL),
        out_shape=jax.ShapeDtypeStruct((n, D_MODEL), F32),
        compiler_params=_cparams(("parallel",)),
        name="branch_merge_ln1",
    )(x2d, *ys, *consts)


def _ple_ln2(x1, x1b, f, p_ref, wpg_ref, wpp_ref, g_ref, b_ref):
    ple = _sigmoid(_dot(x1b, wpg_ref[...])) * _dot(p_ref[...].astype(BF16), wpp_ref[...])
    return _layer_norm(DEEPNORM_ALPHA * x1 + f + ple, g_ref[...], b_ref[...])


FF_CHUNK = 256


def _ffn_dense_kernel(x_ref, p_ref, wi_ref, wo_ref, wpg_ref, wpp_ref, g_ref, b_ref, o_ref, acc_ref):
    xb = x_ref[...].astype(BF16)
    for c in range(D_FF // FF_CHUNK):
        a = _dot(xb, wi_ref[:, c * FF_CHUNK:(c + 1) * FF_CHUNK])
        u = _dot(xb, wi_ref[:, D_FF + c * FF_CHUNK:D_FF + (c + 1) * FF_CHUNK])
        part = _dot((_silu(a) * u).astype(BF16), wo_ref[c * FF_CHUNK:(c + 1) * FF_CHUNK, :])
        if c == 0:
            acc_ref[...] = part
        else:
            acc_ref[...] += part
    o_ref[...] = _ple_ln2(x_ref[...], xb, acc_ref[...], p_ref, wpg_ref, wpp_ref, g_ref, b_ref)


def _ffn_dense(x1, p2d, wts):
    n = x1.shape[0]
    row = lambda w: pl.BlockSpec((TF, w), lambda i: (i, 0))
    consts = [wts['ffn_w_in'], wts['ffn_w_out'], wts['ple_w_gate'], wts['ple_w_proj'], wts['ln2_g'], wts['ln2_b']]
    return pl.pallas_call(
        _ffn_dense_kernel,
        grid=(n // TF,),
        in_specs=[row(D_MODEL), row(P_DIM)] + [_const_spec(c.shape) for c in consts],
        out_specs=row(D_MODEL),
        out_shape=jax.ShapeDtypeStruct((n, D_MODEL), F32),
        scratch_shapes=[pltpu.VMEM((TF, D_MODEL), F32)],
        compiler_params=_cparams(("parallel",)),
        name="ffn_dense_ple_ln2",
    )(x1, p2d, *consts)


TR = 512


def _moe_route_kernel(x_ref, wr_ref, tri_ref, info_ref, cnt_ref, run_ref):
    @pl.when(pl.program_id(0) == 0)
    def _():
        run_ref[...] = jnp.zeros_like(run_ref)

    xh, xl = _split_bf16(x_ref[...])
    wh, wl = _split_bf16(wr_ref[...])
    logits = _dot(xh, wh) + (_dot(xl, wh) + _dot(xh, wl))
    lane = lax.broadcasted_iota(jnp.int32, (TR, LANES), 1)
    lane_f = lane.astype(F32)
    logits = jnp.where(lane < N_EXPERTS, logits, NEG)
    m1 = jnp.max(logits, axis=-1, keepdims=True)
    i1 = jnp.min(jnp.where(logits == m1, lane_f, float(LANES)), axis=-1, keepdims=True)
    rest = jnp.where(lane_f == i1, NEG, logits)
    m2 = jnp.max(rest, axis=-1, keepdims=True)
    i2 = jnp.min(jnp.where(rest == m2, lane_f, float(LANES)), axis=-1, keepdims=True)
    e = jnp.exp(m2 - m1)
    g1 = 1.0 / (1.0 + e)
    g2 = e / (1.0 + e)
    hot1 = lane_f == i1
    hot2 = lane_f == i2
    onehot = jnp.where(hot1 | hot2, 1.0, 0.0)
    before = _dot(tri_ref[...], onehot.astype(BF16)) + run_ref[0:1, :]
    r1 = jnp.sum(jnp.where(hot1, before, 0.0), axis=-1, keepdims=True)
    r2 = jnp.sum(jnp.where(hot2, before, 0.0), axis=-1, keepdims=True)
    run_ref[0:1, :] = run_ref[0:1, :] + jnp.sum(onehot, axis=0, keepdims=True)
    info = jnp.zeros((TR, LANES), F32)
    for k, val in enumerate((i1, i2, r1, r2, g1, g2)):
        info = jnp.where(lane == k, val, info)
    info_ref[...] = info
    cnt_ref[...] = jnp.broadcast_to(run_ref[0:1, :], cnt_ref.shape)


def _moe_route(x1, w_router_pad, tri):
    n = x1.shape[0]
    return pl.pallas_call(
        _moe_route_kernel,
        grid=(n // TR,),
        in_specs=[pl.BlockSpec((TR, D_MODEL), lambda i: (i, 0)), _const_spec(w_router_pad.shape),
                  _const_spec(tri.shape)],
        out_specs=[pl.BlockSpec((TR, LANES), lambda i: (i, 0)), pl.BlockSpec((8, LANES), lambda i: (0, 0))],
        out_shape=[jax.ShapeDtypeStruct((n, LANES), F32), jax.ShapeDtypeStruct((8, LANES), F32)],
        scratch_shapes=[pltpu.VMEM((8, LANES), F32)],
        compiler_params=_cparams(("arbitrary",)),
        name="moe_router_rank",
    )(x1, w_router_pad, tri)


def _moe_rowmap_kernel(d1_ref, d2_ref, rt_ref):
    def clear(r, _):
        rt_ref[r] = 0
        return 0

    lax.fori_loop(0, rt_ref.shape[0], clear, 0, unroll=16)

    def place(t, _):
        rt_ref[d1_ref[t]] = t
        rt_ref[d2_ref[t]] = t
        return 0

    lax.fori_loop(0, d1_ref.shape[0], place, 0, unroll=16)


def _moe_rowmap(dest1, dest2, n_rows):
    smem = pl.BlockSpec(memory_space=pltpu.SMEM)
    return pl.pallas_call(
        _moe_rowmap_kernel,
        in_specs=[smem, smem],
        out_specs=smem,
        out_shape=jax.ShapeDtypeStruct((n_rows,), jnp.int32),
        name="moe_row_map",
    )(dest1, dest2)


EF_CHUNK = 1792
EF_STEPS = D_FF_EXPERT // EF_CHUNK
assert EF_STEPS >= 2
ROWS_PER_STEP = MOE_BLOCK // EF_STEPS


def _moe_ffn_kernel(be_ref, na_ref, rt_ref, x_hbm, wa_ref, wu_ref, wo_ref, ys_ref, xs_ref, acc_ref, sem):
    del be_ref
    blk, c = pl.program_id(0), pl.program_id(1)
    n_live = na_ref[0]
    last_step = pl.num_programs(1) - 1
    slot = blk % 2

    def row_copy(block, r, s):
        return pltpu.make_async_copy(x_hbm.at[pl.ds(rt_ref[block * MOE_BLOCK + r], 1), :],
                                     xs_ref.at[s, pl.ds(r, 1), :], sem.at[s])

    def wait_block(s):
        pltpu.make_async_copy(x_hbm.at[pl.ds(0, MOE_BLOCK), :], xs_ref.at[s], sem.at[s]).wait()

    @pl.when(blk < n_live)
    def _():
        @pl.when((blk == 0) & (c == 0))
        def _():
            def body(r, _):
                row_copy(0, r, 0).start()
                return 0
            lax.fori_loop(0, MOE_BLOCK, body, 0)

        @pl.when(c == 0)
        def _():
            wait_block(slot)

        xb = xs_ref[slot].astype(BF16)
        h = (_silu(_dot(xb, wa_ref[...])) * _dot(xb, wu_ref[...])).astype(BF16)
        part = _dot(h, wo_ref[...])

        nxt = jnp.minimum(blk + 1, pl.num_programs(0) - 1)
        for r in range(ROWS_PER_STEP):
            row_copy(nxt, c * ROWS_PER_STEP + r, 1 - slot).start()

        @pl.when(c == 0)
        def _():
            acc_ref[...] = part

        @pl.when((c > 0) & (c < last_step))
        def _():
            acc_ref[...] += part

        @pl.when(c == last_step)
        def _():
            ys_ref[...] = acc_ref[...] + part

        @pl.when((c == last_step) & (blk == n_live - 1))
        def _():
            wait_block(1 - slot)

    @pl.when((blk >= n_live) & (c == last_step))
    def _():
        ys_ref[...] = jnp.zeros_like(ys_ref)


def _moe_ffn(x1, row_tok, blk_expert, n_active, w_in, w_out):
    n_rows = row_tok.shape[0]
    n_blk = n_rows // MOE_BLOCK
    n_ch = EF_STEPS
    live = lambda b, na: jnp.minimum(b, na[0] - 1)
    chunk = lambda b, c, na: jnp.where(b < na[0], c, n_ch - 1)
    return pl.pallas_call(
        _moe_ffn_kernel,
        grid_spec=pltpu.PrefetchScalarGridSpec(
            num_scalar_prefetch=3,
            grid=(n_blk, n_ch),
            in_specs=[
                pl.BlockSpec(memory_space=pl.ANY),
                pl.BlockSpec((None, D_MODEL, EF_CHUNK),
                             lambda b, c, be, na, rt: (be[live(b, na)], 0, chunk(b, c, na))),
                pl.BlockSpec((None, D_MODEL, EF_CHUNK),
                             lambda b, c, be, na, rt: (be[live(b, na)], 0, n_ch + chunk(b, c, na))),
                pl.BlockSpec((None, EF_CHUNK, D_MODEL),
                             lambda b, c, be, na, rt: (be[live(b, na)], chunk(b, c, na), 0)),
            ],
            out_specs=pl.BlockSpec((MOE_BLOCK, D_MODEL), lambda b, c, be, na, rt: (b, 0)),
            scratch_shapes=[pltpu.VMEM((2, MOE_BLOCK, D_MODEL), F32), pltpu.VMEM((MOE_BLOCK, D_MODEL), F32),
                            pltpu.SemaphoreType.DMA((2,))],
        ),
        out_shape=jax.ShapeDtypeStruct((n_rows, D_MODEL), F32),
        compiler_params=_cparams(("arbitrary", "arbitrary")),
        name="moe_expert_swiglu",
    )(blk_expert, n_active, row_tok, x1, w_in, w_in, w_out)


TC = 512


def _moe_combine_kernel(d1_ref, d2_ref, ys_hbm, x_ref, p_ref, info_ref, wpg_ref, wpp_ref, g_ref,
                        b_ref, o_ref, ya_ref, yb_ref, sem):
    i = pl.program_id(0)
    last = pl.num_programs(0) - 1
    slot = i % 2

    def copies(tile, r, s):
        t = tile * TC + r
        return (pltpu.make_async_copy(ys_hbm.at[pl.ds(d1_ref[t], 1), :], ya_ref.at[s, pl.ds(r, 1), :], sem.at[0, s]),
                pltpu.make_async_copy(ys_hbm.at[pl.ds(d2_ref[t], 1), :], yb_ref.at[s, pl.ds(r, 1), :], sem.at[1, s]))

    def wait_tile(s):
        pltpu.make_async_copy(ys_hbm.at[pl.ds(0, TC), :], ya_ref.at[s], sem.at[0, s]).wait()
        pltpu.make_async_copy(ys_hbm.at[pl.ds(0, TC), :], yb_ref.at[s], sem.at[1, s]).wait()

    @pl.when(i == 0)
    def _():
        def body(r, _):
            for cp in copies(0, r, 0):
                cp.start()
            return 0
        lax.fori_loop(0, TC, body, 0)

    wait_tile(slot)
    nxt = jnp.minimum(i + 1, last)
    for r in range(TC):
        for cp in copies(nxt, r, 1 - slot):
            cp.start()
    info = info_ref[...]
    f = info[:, 4:5] * ya_ref[slot] + info[:, 5:6] * yb_ref[slot]
    x = x_ref[...]
    o_ref[...] = _ple_ln2(x, x.astype(BF16), f, p_ref, wpg_ref, wpp_ref, g_ref, b_ref)

    @pl.when(i == last)
    def _():
        wait_tile(1 - slot)


def _moe_combine(ys, dest1, dest2, x1, p2d, info, wts):
    n = x1.shape[0]
    row = lambda w: pl.BlockSpec((TC, w), lambda i, d1, d2: (i, 0))
    consts = [wts['ple_w_gate'], wts['ple_w_proj'], wts['ln2_g'], wts['ln2_b']]
    cspec = lambda c: pl.BlockSpec(c.shape, lambda i, d1, d2: (0,) * c.ndim, pipeline_mode=pl.Buffered(1))
    return pl.pallas_call(
        _moe_combine_kernel,
        grid_spec=pltpu.PrefetchScalarGridSpec(
            num_scalar_prefetch=2,
            grid=(n // TC,),
            in_specs=[pl.BlockSpec(memory_space=pl.ANY), row(D_MODEL), row(P_DIM), row(LANES)]
            + [cspec(c) for c in consts],
            out_specs=row(D_MODEL),
            scratch_shapes=[pltpu.VMEM((2, TC, D_MODEL), F32), pltpu.VMEM((2, TC, D_MODEL), F32),
                            pltpu.SemaphoreType.DMA((2, 2))],
        ),
        out_shape=jax.ShapeDtypeStruct((n, D_MODEL), F32),
        compiler_params=_cparams(("arbitrary",)),
        name="moe_combine_ple_ln2",
    )(dest1, dest2, ys, x1, p2d, info, *consts)


def _moe(x1, p2d, wts):
    n = x1.shape[0]
    n_rows = ((n * 2 + MOE_BLOCK - 1) // MOE_BLOCK) * MOE_BLOCK + N_EXPERTS * MOE_BLOCK
    info, cnt = _moe_route(x1, wts['w_router'], wts['tri_tokens'])
    counts = cnt[0, :N_EXPERTS].astype(jnp.int32)
    padded = ((counts + MOE_BLOCK - 1) // MOE_BLOCK) * MOE_BLOCK
    ends = jnp.cumsum(padded)
    start_pad = ends - padded
    e1, e2 = info[:, 0].astype(jnp.int32), info[:, 1].astype(jnp.int32)
    dest1 = start_pad[e1] + info[:, 2].astype(jnp.int32)
    dest2 = start_pad[e2] + info[:, 3].astype(jnp.int32)
    n_blk = n_rows // MOE_BLOCK
    blk_row0 = jnp.arange(n_blk, dtype=jnp.int32) * MOE_BLOCK
    blk_expert = jnp.minimum(jnp.sum((ends[None, :] <= blk_row0[:, None]).astype(jnp.int32), axis=1),
                             N_EXPERTS - 1)
    n_active = (ends[-1:] // MOE_BLOCK).astype(jnp.int32)
    row_tok = _moe_rowmap(dest1, dest2, n_rows)
    ys = _moe_ffn(x1, row_tok, blk_expert, n_active, wts['moe_w_in'], wts['moe_w_out'])
    return _moe_combine(ys, dest1, dest2, x1, p2d, info, wts)


def _rot_half_cols(w, heads, dim):
    w3 = w.reshape(w.shape[0], heads, dim)
    half = dim // 2
    return jnp.concatenate([-w3[..., half:], w3[..., :half]], axis=-1).reshape(w.shape[0], heads * dim)


def _prep_layer(i, w_in, conv_w, conv_b, conv_ln_g, conv_ln_b, nsa_cmp_pe, nsa_cmp_w1, nsa_cmp_w2,
                mla_q_norm, mla_kv_norm, mla_w_uq, mla_w_ukv, w_branch, w_out, ln1_g, ln1_b,
                ple_w_gate, ple_w_proj, ln2_g, ln2_b):
    w = w_in[i]
    d = w.shape[0]
    z = lambda n: jnp.zeros((d, n), F32)
    dup = lambda a: jnp.concatenate([a, a], axis=1)
    c_glu, nq = w[:, 0:512], w[:, 512:768]
    nkv = w[:, 768:1152]
    k_cmp, v_cmp, k_slc, v_slc, k_win, v_win = [nkv[:, j * 64:(j + 1) * 64] for j in range(6)]
    ng = w[:, 1152:1164]
    mq, mkv, mkr = w[:, 1164:1420], w[:, 1420:1548], w[:, 1548:1580]
    sb = w[:, 1580:2348]
    bg = w[:, 2348:6444]
    cols = [c_glu, nq, _rot_half_cols(nq, NSA_HEADS, HEAD_DIM),
            dup(k_slc), dup(_rot_half_cols(k_slc, 1, HEAD_DIM)),
            dup(k_win), dup(_rot_half_cols(k_win, 1, HEAD_DIM)),
            dup(v_slc), dup(v_win), k_cmp, v_cmp, ng, z(LANES - 12), mq, mkv,
            z(64), mkr, z(32), z(64), _rot_half_cols(mkr, 1, MLA_ROPE), z(32), z(LANES), sb]
    w1 = jnp.concatenate(cols, axis=1).astype(BF16)
    assert w1.shape[1] == C_TOT

    inv32 = ROPE_THETA ** (-jnp.arange(HEAD_DIM // 2, dtype=F32) / (HEAD_DIM // 2))
    inv16 = ROPE_THETA ** (-jnp.arange(MLA_ROPE // 2, dtype=F32) / (MLA_ROPE // 2))
    inv_nsa = jnp.tile(inv32, 4)[None, :]
    inv_mla = jnp.concatenate([jnp.zeros((64,), F32), inv16, inv16, jnp.zeros((32,), F32)])[None, :]

    wuq = mla_w_uq[i].reshape(MLA_Q_RANK, MLA_HEADS, MLA_NOPE + MLA_ROPE)
    zq = jnp.zeros((MLA_Q_RANK, MLA_HEADS, 32), F32)
    wq = jnp.concatenate([wuq, zq], axis=-1).reshape(MLA_Q_RANK, MLA_HEADS * LANES)
    rope_rot = jnp.concatenate([-wuq[..., MLA_NOPE + 16:], wuq[..., MLA_NOPE:MLA_NOPE + 16]], axis=-1)
    wqr = jnp.concatenate([jnp.zeros((MLA_Q_RANK, MLA_HEADS, MLA_NOPE), F32), rope_rot, zq],
                          axis=-1).reshape(MLA_Q_RANK, MLA_HEADS * LANES)
    wukv = mla_w_ukv[i].reshape(MLA_KV_RANK, MLA_HEADS, MLA_NOPE + MLA_V)
    wk = jnp.concatenate([wukv[..., :MLA_NOPE], jnp.zeros((MLA_KV_RANK, MLA_HEADS, 64), F32)],
                         axis=-1).reshape(MLA_KV_RANK, MLA_HEADS * LANES)
    wv = wukv[..., MLA_NOPE:].reshape(MLA_KV_RANK, MLA_HEADS * MLA_V)

    pe = nsa_cmp_pe[i]
    pe_rows = pe.reshape(CMP_BLOCK, 2 * HEAD_DIM)
    pe_a = pe_rows[:CMP_STRIDE].reshape(1, CMP_STRIDE * LANES)
    pe_b = pe_rows[CMP_STRIDE:].reshape(1, CMP_STRIDE * LANES)
    w1c = nsa_cmp_w1[i].reshape(2, CMP_BLOCK, HEAD_DIM, HEAD_DIM)
    zblk = jnp.zeros((CMP_BLOCK, HEAD_DIM, HEAD_DIM), F32)
    w1full = jnp.concatenate([jnp.concatenate([w1c[0], zblk], axis=2),
                              jnp.concatenate([zblk, w1c[1]], axis=2)], axis=1)
    w1a = w1full[:CMP_STRIDE].reshape(CMP_STRIDE * LANES, LANES).astype(BF16)
    w1b = w1full[CMP_STRIDE:].reshape(CMP_STRIDE * LANES, LANES).astype(BF16)
    w2 = nsa_cmp_w2[i]
    z64 = jnp.zeros((HEAD_DIM, LANES), F32)
    w2k = jnp.concatenate([dup(w2[0]), z64], axis=0).astype(BF16)
    w2v = jnp.concatenate([z64, dup(w2[1])], axis=0).astype(BF16)

    return dict(
        w1=w1, inv_nsa=inv_nsa, inv_mla=inv_mla,
        mla_qn=mla_q_norm[i][None, :], mla_kvn=mla_kv_norm[i][None, :],
        wq=wq.astype(BF16), wqr=wqr.astype(BF16), wk=wk.astype(BF16), wv=wv.astype(BF16),
        conv_w=conv_w[i], conv_b=conv_b[i][None, :], conv_g=conv_ln_g[i][None, :], conv_beta=conv_ln_b[i][None, :],
        pe_a=pe_a, pe_b=pe_b, w1a=w1a, w1b=w1b, w2k=w2k, w2v=w2v,
        w_gate=bg.astype(BF16), w_branch=w_branch[i].astype(BF16), w_out=w_out[i].astype(BF16),
        ln1_g=ln1_g[i][None, :], ln1_b=ln1_b[i][None, :],
        ple_w_gate=ple_w_gate[i].astype(BF16), ple_w_proj=ple_w_proj[i].astype(BF16),
        ln2_g=ln2_g[i][None, :], ln2_b=ln2_b[i][None, :],
    )


def _tables(seq):
    n_cmp_rows = seq // CMP_STRIDE
    n_sel = seq // SEL_BLOCK
    cmp_start = jnp.arange(n_cmp_rows) * CMP_STRIDE
    sel_start = jnp.arange(LANES) * SEL_BLOCK
    n_cmp = (seq - CMP_BLOCK) // CMP_STRIDE + 1
    overlap = ((cmp_start[:, None] < sel_start[None, :] + SEL_BLOCK)
               & (cmp_start[:, None] + CMP_BLOCK > sel_start[None, :])
               & (jnp.arange(n_cmp_rows)[:, None] < n_cmp) & (jnp.arange(LANES)[None, :] < n_sel))
    kb = jnp.arange(seq // TK)[:, None, None]
    nn = jnp.arange(LANES)[None, :, None]
    ll = jnp.arange(TK)[None, None, :]
    expand = (kb * TK + ll) // SEL_BLOCK == nn
    jj = jnp.arange(TK)
    tri_keys = jnp.tile(jj[:, None] >= jj[None, :], (2, 1))
    tt = jnp.arange(TR)
    tri_tokens = tt[None, :] < tt[:, None]
    src = jnp.arange(LANES)[:, None]
    dst = jnp.arange(3 * N_HEADS * HEAD_DIM)[None, :]
    width = N_HEADS * HEAD_DIM
    gate_expand = jnp.tile((src < 3 * N_HEADS) & (src % 3 == dst // width) & (src // 3 == (dst % width) // HEAD_DIM),
                           (2, 1))
    return dict(overlap_t=overlap.T.astype(BF16), eye_q=jnp.eye(TQ, dtype=BF16), gate_expand=gate_expand.astype(BF16),
                expand=expand.astype(BF16), tri_keys=tri_keys.astype(BF16),
                tri_tokens=tri_tokens.astype(BF16))


def kernel(x, p, positions, w_in, conv_w, conv_b, conv_ln_g, conv_ln_b, nsa_cmp_pe, nsa_cmp_w1, nsa_cmp_w2,
           mla_q_norm, mla_kv_norm, mla_w_uq, mla_w_ukv, w_branch, w_out, ln1_g, ln1_b, ffn_w_in, ffn_w_out,
           moe_router, moe_w_in, moe_w_out, ple_w_gate, ple_w_proj, ln2_g, ln2_b):
    batch, seq, _ = x.shape
    n = batch * seq
    tabs = _tables(seq)
    x2d = x.reshape(n, D_MODEL)
    pos2d = positions.reshape(n, 1)
    for i in range(DEPTH):
        wts = _prep_layer(i, w_in, conv_w, conv_b, conv_ln_g, conv_ln_b, nsa_cmp_pe, nsa_cmp_w1, nsa_cmp_w2,
                          mla_q_norm, mla_kv_norm, mla_w_uq, mla_w_ukv, w_branch, w_out, ln1_g, ln1_b,
                          ple_w_gate, ple_w_proj, ln2_g, ln2_b)
        wts['overlap_t'], wts['eye_q'] = tabs['overlap_t'], tabs['eye_q']
        (conv_in, nq, nqr, ks, kw, vs, vw, kvc, ng, mq, mk, mv, sq, sk, sv) = _mixer_in(x2d, pos2d, wts)
        y_a = _conv(conv_in, wts['conv_w'], wts['conv_b'], wts['conv_g'], wts['conv_beta'], batch, seq)
        ocmp, sel = _nsa_cmp(kvc, nq, wts, batch, seq)
        y_b = _nsa_attn(nqr, ks, vs, kw, vw, sel, tabs['expand'], ocmp, ng, tabs['gate_expand'], batch, seq)
        y_c = _mla_attn(mq, mk, mv, batch, seq)
        y_d = _sb_attn(sq, sk, sv, tabs['tri_keys'], batch, seq)
        ys = [y.reshape(n, BRANCH_W) for y in (y_a, y_b, y_c, y_d)]
        x1 = _merge(x2d, ys, wts)
        p2d = p[i].reshape(n, P_DIM)
        if i % 2 == 0:
            wts['ffn_w_in'] = ffn_w_in[i // 2].astype(BF16)
            wts['ffn_w_out'] = ffn_w_out[i // 2].astype(BF16)
            x2d = _ffn_dense(x1, p2d, wts)
        else:
            wts['w_router'] = jnp.concatenate(
                [moe_router[i // 2], jnp.zeros((D_MODEL, LANES - N_EXPERTS), F32)], axis=1)
            wts['tri_tokens'] = tabs['tri_tokens']
            wts['moe_w_in'] = moe_w_in[i // 2].astype(BF16)
            wts['moe_w_out'] = moe_w_out[i // 2].astype(BF16)
            x2d = _moe(x1, p2d, wts)
    return x2d.reshape(batch, seq, D_MODEL)
```

```python
import functools

import jax
import jax.numpy as jnp
from jax import lax
from jax.experimental import pallas as pl
from jax.experimental.pallas import tpu as pltpu

F32 = jnp.float32
BF16 = jnp.bfloat16

D_MODEL = 1024
DEPTH = 2
CONV_CH = 256
CONV_WIDTH = 31
NSA_HEADS = 4
HEAD_DIM = 64
CMP_BLOCK = 32
CMP_STRIDE = 16
SEL_BLOCK = 64
SEL_TOPN = 16
WINDOW = 512
MLA_HEADS = 4
MLA_Q_RANK = 256
MLA_KV_RANK = 128
MLA_NOPE = 64
MLA_ROPE = 32
MLA_V = 64
BRANCH_W = 256
ROPE_THETA = 10000.0
LN_EPS = 1e-5
RMS_EPS = 1e-6
D_FF = 2816
N_EXPERTS = 8
D_FF_EXPERT = 3584
MOE_BLOCK = 512
P_DIM = 256
DEEPNORM_ALPHA = (2 * DEPTH) ** 0.25

LANES = 128
SUBLANES = 8
MXU_W = 256
HEADS_W = 256
VMEM_LIMIT = 56 * 1024 * 1024

NEG = -1e30

C_CONV = 0
C_NQ = 512
C_NQR = 768
C_KS = 1024
C_KW = 1280
C_VS = 1536
C_KVC = 1792
C_MQ = 2048
C_MKV = 2304
C_MKRR = 2560
C_SB = 2816
C_TOT = 3584

TM = 256
TF = 512
TQ = 256
TK = 256


def _cparams(sem, vmem=VMEM_LIMIT):
    return pltpu.CompilerParams(dimension_semantics=sem, vmem_limit_bytes=vmem)


def _const_spec(shape):
    nd = len(shape)
    return pl.BlockSpec(shape, lambda *_: (0,) * nd, pipeline_mode=pl.Buffered(1))


def _dot(a, b):
    return jnp.dot(a, b, preferred_element_type=F32)


def _dot_nt(a, b):
    return lax.dot_general(a, b, (((1,), (1,)), ((), ())), preferred_element_type=F32)


def _layer_norm(h, g, b):
    mu = jnp.mean(h, axis=-1, keepdims=True)
    d = h - mu
    var = jnp.mean(d * d, axis=-1, keepdims=True)
    return d * lax.rsqrt(var + LN_EPS) * g + b


def _rms_norm(h, g):
    return h * lax.rsqrt(jnp.mean(h * h, axis=-1, keepdims=True) + RMS_EPS) * g


def _sigmoid(x):
    return 1.0 / (1.0 + jnp.exp(-x))


def _silu(x):
    return x * _sigmoid(x)


def _split_bf16(x):
    hi = x.astype(BF16)
    lo = (x - hi.astype(F32)).astype(BF16)
    return hi, lo


def _half_select(sub, x):
    lane = lax.broadcasted_iota(jnp.int32, x.shape, 1)
    keep = (lane < HEAD_DIM) if sub == 0 else (lane >= HEAD_DIM)
    return jnp.where(keep, x, 0.0)


def _mixer_in_kernel(x_ref, pos_ref, w_ref, invn_ref, invm_ref, qn_ref, kvn_ref, wq_ref, wqr_ref,
                     wk_ref, wv_ref,
                     conv_ref, nq_ref, nqr_ref, ks_ref, kw_ref, vs_ref, vw_ref, kvc_ref, ng_ref,
                     mq_ref, mk_ref, mv_ref, sq_ref, sk_ref, sv_ref):
    xb = x_ref[...].astype(BF16)

    def proj(c0, width):
        return _dot(xb, w_ref[:, c0:c0 + width])

    posf = pos_ref[...].astype(F32)
    ang_n = posf * invn_ref[...]
    cos_n, sin_n = jnp.cos(ang_n), jnp.sin(ang_n)
    ang_m = posf * invm_ref[...]
    cos_m, sin_m = jnp.cos(ang_m), jnp.sin(ang_m)

    lo, hi = slice(0, LANES), slice(LANES, 2 * LANES)
    qn = _rms_norm(proj(C_MQ, MLA_Q_RANK), qn_ref[...]).astype(BF16)
    mkv = proj(C_MKV, MXU_W)
    kvn = _rms_norm(mkv[:, lo], kvn_ref[...]).astype(BF16)

    conv_ref[...] = proj(C_CONV, 2 * CONV_CH)

    scale = HEAD_DIM ** -0.5
    cos_n2 = jnp.concatenate([cos_n, cos_n], axis=1)
    sin_n2 = jnp.concatenate([sin_n, sin_n], axis=1)
    q, qrot = proj(C_NQ, MXU_W), proj(C_NQR, MXU_W)
    nq_ref[...] = (q * scale).astype(BF16)
    nqr_ref[...] = ((q * cos_n2 + qrot * sin_n2) * scale).astype(BF16)
    k = proj(C_KS, MXU_W)
    ks_ref[...] = (k[:, lo] * cos_n + k[:, hi] * sin_n).astype(BF16)
    k = proj(C_KW, MXU_W)
    kw_ref[...] = (k[:, lo] * cos_n + k[:, hi] * sin_n).astype(BF16)
    v = proj(C_VS, MXU_W)
    vs_ref[...] = v[:, lo].astype(BF16)
    vw_ref[...] = v[:, hi].astype(BF16)
    u = proj(C_KVC, MXU_W)
    kvc_ref[...] = u[:, lo]
    ng_ref[...] = u[:, hi]

    kr = mkv[:, hi] * cos_m + proj(C_MKRR, MXU_W)[:, lo] * sin_m
    qa, qr, kk = _dot(qn, wq_ref[...]), _dot(qn, wqr_ref[...]), _dot(kvn, wk_ref[...])
    for h in range(MLA_HEADS):
        sl = slice(h * LANES, (h + 1) * LANES)
        mq_ref[:, sl] = (qa[:, sl] * cos_m + qr[:, sl] * sin_m).astype(BF16)
        mk_ref[:, sl] = (kk[:, sl] + kr).astype(BF16)
    mv_ref[...] = _dot(kvn, wv_ref[...]).astype(BF16)

    sq_ref[...] = (proj(C_SB, HEADS_W) * scale).astype(BF16)
    sk_ref[...] = proj(C_SB + HEADS_W, HEADS_W).astype(BF16)
    sv_ref[...] = proj(C_SB + 2 * HEADS_W, HEADS_W).astype(BF16)


def _mixer_in(x2d, pos2d, wts):
    n = x2d.shape[0]
    row = lambda w: pl.BlockSpec((TF, w), lambda i: (i, 0))
    mla_w = MLA_HEADS * LANES
    out_widths = [2 * CONV_CH, HEADS_W, HEADS_W, LANES, LANES, LANES, LANES, LANES, LANES, mla_w, mla_w,
                  HEADS_W, HEADS_W, HEADS_W, HEADS_W]
    out_dtypes = [F32, BF16, BF16, BF16, BF16, BF16, BF16, F32, F32, BF16, BF16, BF16, BF16, BF16, BF16]
    consts = [wts['w1'], wts['inv_nsa'], wts['inv_mla'], wts['mla_qn'], wts['mla_kvn'], wts['wq'],
              wts['wqr'], wts['wk'], wts['wv']]
    return pl.pallas_call(
        _mixer_in_kernel,
        grid=(n // TF,),
        in_specs=[row(D_MODEL), row(1)] + [_const_spec(c.shape) for c in consts],
        out_specs=[row(w) for w in out_widths],
        out_shape=[jax.ShapeDtypeStruct((n, w), d) for w, d in zip(out_widths, out_dtypes)],
        compiler_params=_cparams(("parallel",)),
        name="mixer_in",
    )(x2d, pos2d, *consts)


CONV_PAD = 32
CONV_CHUNK = 128


def _conv_kernel(u_ref, w_ref, b_ref, g_ref, beta_ref, o_ref, hp_ref):
    seq = u_ref.shape[0]
    hp_ref[0, 0:CONV_PAD, :] = jnp.zeros((CONV_PAD, CONV_CH), F32)
    hp_ref[0, CONV_PAD:CONV_PAD + seq, :] = u_ref[:, 0:CONV_CH] * _sigmoid(u_ref[:, CONV_CH:2 * CONV_CH])
    rows = seq + CONV_PAD - SUBLANES
    for s in range(1, SUBLANES):
        hp_ref[s, 0:rows, :] = hp_ref[0, s:s + rows, :]
    first = CONV_PAD - (CONV_WIDTH - 1)
    for c in range(seq // CONV_CHUNK):
        base = c * CONV_CHUNK
        acc = jnp.broadcast_to(b_ref[...], (CONV_CHUNK, CONV_CH))
        for j in range(CONV_WIDTH):
            a, s = divmod(first + j, SUBLANES)
            r0 = base + a * SUBLANES
            acc = acc + hp_ref[s, r0:r0 + CONV_CHUNK, :] * w_ref[j:j + 1, :]
        y = _layer_norm(acc, g_ref[...], beta_ref[...])
        o_ref[base:base + CONV_CHUNK, :] = _silu(y).astype(BF16)


def _conv(conv_in, w, b, g, beta, batch, seq):
    return pl.pallas_call(
        _conv_kernel,
        grid=(batch,),
        in_specs=[pl.BlockSpec((None, seq, 2 * CONV_CH), lambda i: (i, 0, 0)),
                  _const_spec(w.shape), _const_spec(b.shape), _const_spec(g.shape), _const_spec(beta.shape)],
        out_specs=pl.BlockSpec((None, seq, CONV_CH), lambda i: (i, 0, 0)),
        out_shape=jax.ShapeDtypeStruct((batch, seq, CONV_CH), BF16),
        scratch_shapes=[pltpu.VMEM((SUBLANES, CONV_PAD + seq, CONV_CH), F32)],
        compiler_params=_cparams(("parallel",)),
        name="conformer_conv",
    )(conv_in.reshape(batch, seq, 2 * CONV_CH), w, b, g, beta)


def _gelu_tanh(x):
    return 0.5 * x * (1.0 + jnp.tanh(0.7978845608028654 * (x + 0.044715 * x * x * x)))


SEL_SHIFT = 6


def _nsa_cmp_kernel(kvc_ref, q_ref, pea_ref, peb_ref, w1a_ref, w1b_ref, w2k_ref, w2v_ref, ovt_ref, eye_ref,
                    ocmp_ref, sel_ref, *, seq):
    n_cmp = (seq - CMP_BLOCK) // CMP_STRIDE + 1
    n_sel = seq // SEL_BLOCK
    nb = seq // CMP_STRIDE
    x2 = kvc_ref[...]
    xa = (x2 + pea_ref[...]).astype(BF16)
    xb = (x2 + peb_ref[...]).astype(BF16)
    ha = _dot(xa, w1a_ref[...])
    hb = _dot(xb, w1b_ref[...])
    hid = ha + pltpu.roll(hb, nb - 1, 0)
    hid = _gelu_tanh(hid).astype(BF16)
    kk = _dot(hid, w2k_ref[...]).astype(BF16)
    vv = _dot(hid, w2v_ref[...]).astype(BF16)
    ovt = ovt_ref[...]

    for c in range(seq // TQ):
        r0 = c * TQ
        t = r0 + lax.broadcasted_iota(jnp.int32, (TQ, LANES), 0)
        j = lax.broadcasted_iota(jnp.int32, (TQ, LANES), 1)
        valid = (j * CMP_STRIDE + CMP_BLOCK - 1 <= t) & (j < n_cmp)
        psum = jnp.zeros((TQ, LANES), F32)
        for pair in range(2):
            qp = q_ref[r0:r0 + TQ, pair * LANES:(pair + 1) * LANES].astype(F32)
            outs = []
            for sub in range(2):
                qm = _half_select(sub, qp).astype(BF16)
                s = jnp.where(valid, _dot_nt(qm, kk), NEG)
                m = jnp.max(s, axis=-1, keepdims=True)
                e = jnp.where(valid, jnp.exp(s - m), 0.0)
                den = jnp.sum(e, axis=-1, keepdims=True)
                p = e / jnp.where(den > 0, den, 1.0)
                psum = psum + p
                outs.append(_dot(p.astype(BF16), vv))
            ocmp_ref[r0:r0 + TQ, pair * LANES:(pair + 1) * LANES] = jnp.where(
                lax.broadcasted_iota(jnp.int32, (TQ, LANES), 1) < HEAD_DIM, outs[0], outs[1])
        p_hi, p_lo = _split_bf16(psum)
        imp = (_dot_nt(ovt, p_hi) + _dot_nt(ovt, p_lo))[0:n_sel, :]
        n = lax.broadcasted_iota(jnp.int32, (n_sel, TQ), 0)
        cur = jnp.right_shift(r0 + lax.broadcasted_iota(jnp.int32, (n_sel, TQ), 1), SEL_SHIFT)
        forced = (n == 0) | (n == cur) | (n == cur - 1)
        imp = jnp.where(forced, jnp.inf, imp)
        imp = jnp.where(n > cur, -jnp.inf, imp)
        rank = jnp.zeros((n_sel, TQ), F32)
        for n2 in range(n_sel):
            other = imp[n2:n2 + 1, :]
            ahead = (other > imp) | ((other == imp) & (n2 < n))
            rank = rank + jnp.where(ahead, 1.0, 0.0)
        sel_t = jnp.where((rank < SEL_TOPN) & (imp > -jnp.inf), 1.0, 0.0)
        sel_t = jnp.concatenate([sel_t, jnp.zeros((LANES - n_sel, TQ), F32)], axis=0).astype(BF16)
        sel_ref[r0:r0 + TQ, :] = _dot_nt(eye_ref[...], sel_t).astype(BF16)


def _nsa_cmp(kvc, nq, wts, batch, seq):
    nb = seq // CMP_STRIDE
    consts = [wts['pe_a'], wts['pe_b'], wts['w1a'], wts['w1b'], wts['w2k'], wts['w2v'], wts['overlap_t'],
              wts['eye_q']]
    return pl.pallas_call(
        functools.partial(_nsa_cmp_kernel, seq=seq),
        grid=(batch,),
        in_specs=[pl.BlockSpec((None, nb, CMP_STRIDE * LANES), lambda i: (i, 0, 0)),
                  pl.BlockSpec((None, seq, HEADS_W), lambda i: (i, 0, 0))] + [_const_spec(c.shape) for c in consts],
        out_specs=[pl.BlockSpec((None, seq, HEADS_W), lambda i: (i, 0, 0)),
                   pl.BlockSpec((None, seq, LANES), lambda i: (i, 0, 0))],
        out_shape=[jax.ShapeDtypeStruct((batch, seq, HEADS_W), F32),
                   jax.ShapeDtypeStruct((batch, seq, LANES), BF16)],
        compiler_params=_cparams(("parallel",)),
        name="nsa_compress_select",
    )(kvc.reshape(batch, nb, CMP_STRIDE * LANES), nq.reshape(batch, seq, HEADS_W), *consts)


LOG2E = 1.4426950408889634
N_HEADS = 4


def _softmax_scratch(n_tiles):
    slab = pltpu.VMEM((N_HEADS, TQ, LANES), F32)
    return [pltpu.VMEM((N_HEADS, n_tiles, TQ, TK), F32), slab, slab, slab, slab]


def _scores_put(h, t, s, s_ref, mx_ref, first):
    s_ref[h, t] = s
    m = jnp.maximum(s[:, :LANES], s[:, LANES:])
    mx_ref[h] = m if first else jnp.maximum(mx_ref[h], m)


def _row_max(mx_ref, mb_ref):
    for h in range(N_HEADS):
        mb_ref[h] = jnp.broadcast_to(jnp.max(mx_ref[h], axis=-1, keepdims=True), (TQ, LANES))


def _probs_accumulate(h, t, c, v_blk, s_ref, mb_ref, ls_ref, acc_ref, first):
    s, mb = s_ref[h, t], mb_ref[h]
    pa = jnp.exp2((s[:, :LANES] - mb) * c)
    pb = jnp.exp2((s[:, LANES:] - mb) * c)
    pv = _dot(jnp.concatenate([pa, pb], axis=1).astype(BF16), v_blk)
    if first:
        ls_ref[h] = pa + pb
        acc_ref[h] = pv
    else:
        ls_ref[h] += pa + pb
        acc_ref[h] += pv


def _lane_sum_dense(x):
    ones = jnp.ones((2 * LANES, LANES), BF16)
    return _dot(jnp.concatenate(_split_bf16(x), axis=1), ones)


def _softmax_out(h, ls_ref, acc_ref):
    return acc_ref[h] / _lane_sum_dense(ls_ref[h])


def _tile_iotas():
    return (lax.broadcasted_iota(jnp.int32, (TQ, TK), 0), lax.broadcasted_iota(jnp.int32, (TQ, TK), 1))


def _nsa_attn_kernel(q_ref, ks_ref, vs_ref, kw_ref, vw_ref, sel_ref, exp_ref, ocmp_ref, ng_ref, gate_ref, o_ref,
                     qm_ref, ss_ref, mxs_ref, mbs_ref, lss_ref, accs_ref,
                     sw_ref, mxw_ref, mbw_ref, lsw_ref, accw_ref):
    i = pl.program_id(1)
    row, col = _tile_iotas()
    sel = sel_ref[...]
    for pair in range(2):
        qp = q_ref[:, pair * LANES:(pair + 1) * LANES].astype(F32)
        for sub in range(2):
            qm_ref[2 * pair + sub] = _half_select(sub, qp).astype(BF16)

    def selected_scores(kb, diagonal):
        k0 = pl.multiple_of(kb * TK, TK)
        hit = _dot(sel, exp_ref[kb]) > 0.5
        if diagonal:
            hit = hit & (col <= row)
        bias = jnp.where(hit, 0.0, NEG)
        k_blk = ks_ref[pl.ds(k0, TK), :]
        for h in range(N_HEADS):
            _scores_put(h, kb, _dot_nt(qm_ref[h], k_blk) + bias, ss_ref, mxs_ref, diagonal)

    def window_scores(slot, mask):
        k0 = pl.multiple_of((i - 2 + slot) * TK, TK)
        k_blk = kw_ref[pl.ds(k0, TK), :]
        for h in range(N_HEADS):
            s = _dot_nt(qm_ref[h], k_blk)
            if mask is not None:
                s = jnp.where(mask, s, NEG)
            _scores_put(h, slot, s, sw_ref, mxw_ref, slot == 2)

    def off_diagonal(kb, _):
        selected_scores(kb, False)
        return 0

    selected_scores(i, True)
    window_scores(2, col <= row)
    lax.fori_loop(0, i, off_diagonal, 0)
    pl.when(i >= 2)(lambda: window_scores(0, col > row))
    pl.when(i >= 1)(lambda: window_scores(1, None))
    _row_max(mxs_ref, mbs_ref)
    _row_max(mxw_ref, mbw_ref)

    def selected_probs(kb, first):
        v_blk = vs_ref[pl.ds(pl.multiple_of(kb * TK, TK), TK), :]
        for h in range(N_HEADS):
            _probs_accumulate(h, kb, LOG2E, v_blk, ss_ref, mbs_ref, lss_ref, accs_ref, first)

    def window_probs(slot):
        v_blk = vw_ref[pl.ds(pl.multiple_of((i - 2 + slot) * TK, TK), TK), :]
        for h in range(N_HEADS):
            _probs_accumulate(h, slot, LOG2E, v_blk, sw_ref, mbw_ref, lsw_ref, accw_ref, slot == 2)

    def off_diagonal_probs(kb, _):
        selected_probs(kb, False)
        return 0

    selected_probs(i, True)
    window_probs(2)
    lax.fori_loop(0, i, off_diagonal_probs, 0)
    pl.when(i >= 2)(lambda: window_probs(0))
    pl.when(i >= 1)(lambda: window_probs(1))

    g = _sigmoid(ng_ref[...])
    gx = _dot(jnp.concatenate(_split_bf16(g), axis=1), gate_ref[...])
    lane = lax.broadcasted_iota(jnp.int32, (TQ, LANES), 1)

    def heads_out(ls_ref, acc_ref):
        return jnp.concatenate(
            [jnp.where(lane < HEAD_DIM, _softmax_out(2 * pair, ls_ref, acc_ref),
                       _softmax_out(2 * pair + 1, ls_ref, acc_ref)) for pair in range(2)], axis=1)

    width = N_HEADS * HEAD_DIM
    o_ref[...] = (gx[:, 0:width] * ocmp_ref[...] + gx[:, width:2 * width] * heads_out(lss_ref, accs_ref)
                  + gx[:, 2 * width:3 * width] * heads_out(lsw_ref, accw_ref)).astype(BF16)


def _nsa_attn(nqr, ks, vs, kw, vw, sel, expand, ocmp, ng, gate_expand, batch, seq):
    qspec = lambda w: pl.BlockSpec((None, TQ, w), lambda b, i: (b, i, 0))
    kspec = pl.BlockSpec((None, seq, LANES), lambda b, i: (b, 0, 0))
    r3 = lambda a: a.reshape(batch, seq, a.shape[-1])
    return pl.pallas_call(
        _nsa_attn_kernel,
        grid=(batch, seq // TQ),
        in_specs=[qspec(HEADS_W), kspec, kspec, kspec, kspec, qspec(LANES), _const_spec(expand.shape),
                  qspec(HEADS_W), qspec(LANES), _const_spec(gate_expand.shape)],
        out_specs=qspec(HEADS_W),
        out_shape=jax.ShapeDtypeStruct((batch, seq, HEADS_W), BF16),
        scratch_shapes=[pltpu.VMEM((N_HEADS, TQ, LANES), BF16)] + _softmax_scratch(seq // TK)
        + _softmax_scratch(WINDOW // TK + 1),
        compiler_params=_cparams(("parallel", "parallel")),
        name="nsa_select_window",
    )(r3(nqr), r3(ks), r3(vs), r3(kw), r3(vw), sel, expand, ocmp, r3(ng), gate_expand)


def _mla_attn_kernel(q_ref, k_ref, v_ref, o_ref, s_ref, mx_ref, mb_ref, ls_ref, acc_ref):
    i = pl.program_id(1)
    c = (MLA_NOPE + MLA_ROPE) ** -0.5 * LOG2E
    row, col = _tile_iotas()

    def scores(kb, diagonal):
        k0 = pl.multiple_of(kb * TK, TK)
        for h in range(N_HEADS):
            hs = slice(h * LANES, (h + 1) * LANES)
            s = _dot_nt(q_ref[:, hs], k_ref[pl.ds(k0, TK), hs])
            if diagonal:
                s = jnp.where(col <= row, s, NEG)
            _scores_put(h, kb, s, s_ref, mx_ref, diagonal)

    def off_diagonal(kb, _):
        scores(kb, False)
        return 0

    scores(i, True)
    lax.fori_loop(0, i, off_diagonal, 0)
    _row_max(mx_ref, mb_ref)

    def probs(kb, first):
        k0 = pl.multiple_of(kb * TK, TK)
        for h in range(N_HEADS):
            v_blk = v_ref[pl.ds(k0, TK), (h // 2) * LANES:(h // 2 + 1) * LANES]
            _probs_accumulate(h, kb, c, v_blk, s_ref, mb_ref, ls_ref, acc_ref, first)

    def off_diagonal_probs(kb, _):
        probs(kb, False)
        return 0

    probs(i, True)
    lax.fori_loop(0, i, off_diagonal_probs, 0)
    lane = lax.broadcasted_iota(jnp.int32, (TQ, LANES), 1)
    for pair in range(2):
        o_ref[:, pair * LANES:(pair + 1) * LANES] = jnp.where(
            lane < HEAD_DIM, _softmax_out(2 * pair, ls_ref, acc_ref),
            _softmax_out(2 * pair + 1, ls_ref, acc_ref)).astype(BF16)


def _mla_attn(mq, mk, mv, batch, seq):
    r3 = lambda a: a.reshape(batch, seq, a.shape[-1])
    return pl.pallas_call(
        _mla_attn_kernel,
        grid=(batch, seq // TQ),
        in_specs=[pl.BlockSpec((None, TQ, MLA_HEADS * LANES), lambda b, i: (b, i, 0)),
                  pl.BlockSpec((None, seq, MLA_HEADS * LANES), lambda b, i: (b, 0, 0)),
                  pl.BlockSpec((None, seq, HEADS_W), lambda b, i: (b, 0, 0))],
        out_specs=pl.BlockSpec((None, TQ, HEADS_W), lambda b, i: (b, i, 0)),
        out_shape=jax.ShapeDtypeStruct((batch, seq, HEADS_W), BF16),
        scratch_shapes=_softmax_scratch(seq // TK),
        compiler_params=_cparams(("parallel", "parallel")),
        name="mla_attention",
    )(r3(mq), r3(mk), r3(mv))


def _sb_attn_kernel(q_ref, k_ref, v_ref, tri_ref, o_ref, qm_ref, e_ref, tail_ref, acc_ref):
    i = pl.program_id(1)
    row, col = _tile_iotas()
    for pair in range(2):
        qp = q_ref[:, pair * LANES:(pair + 1) * LANES].astype(F32)
        for sub in range(2):
            qm_ref[2 * pair + sub] = _half_select(sub, qp).astype(BF16)

    def log_weights(kb, diagonal):
        k0 = pl.multiple_of(kb * TK, TK)
        tri2 = tri_ref[...]
        heads = range(N_HEADS)
        zs = [_dot_nt(qm_ref[h], k_ref[pl.ds(k0, TK), (h // 2) * LANES:(h // 2 + 1) * LANES]) for h in heads]
        drops = [jnp.maximum(z, 0.0) + jnp.log(1.0 + jnp.exp(-jnp.abs(z))) for z in zs]
        if diagonal:
            drops = [jnp.where(col < row, d, 0.0) for d in drops]
        incls = [_dot(jnp.concatenate(_split_bf16(d), axis=1), tri2) for d in drops]
        for h in heads:
            total = jnp.broadcast_to(incls[h][:, 0:1], (TQ, LANES))
            if diagonal:
                e_ref[h, kb] = jnp.where(col < row, zs[h] - incls[h], NEG)
                tail_ref[h] = total
            else:
                tail = tail_ref[h]
                e_ref[h, kb] = zs[h] - incls[h] - jnp.concatenate([tail, tail], axis=1)
                tail_ref[h] = tail + total

    log_weights(i, True)

    def off_diagonal(step, _):
        log_weights(i - 1 - step, False)
        return 0

    lax.fori_loop(0, i, off_diagonal, 0)

    def weighted_values(kb, first):
        k0 = pl.multiple_of(kb * TK, TK)
        for h in range(N_HEADS):
            ps = slice((h // 2) * LANES, (h // 2 + 1) * LANES)
            av = _dot(jnp.exp(e_ref[h, kb]).astype(BF16), v_ref[pl.ds(k0, TK), ps])
            if first:
                acc_ref[h] = av
            else:
                acc_ref[h] += av

    def off_diagonal_values(kb, _):
        weighted_values(kb, False)
        return 0

    weighted_values(i, True)
    lax.fori_loop(0, i, off_diagonal_values, 0)
    lane = lax.broadcasted_iota(jnp.int32, (TQ, LANES), 1)
    for pair in range(2):
        o_ref[:, pair * LANES:(pair + 1) * LANES] = jnp.where(
            lane < HEAD_DIM, acc_ref[2 * pair], acc_ref[2 * pair + 1]).astype(BF16)


def _sb_attn(sq, sk, sv, tri, batch, seq):
    r3 = lambda a: a.reshape(batch, seq, a.shape[-1])
    kspec = pl.BlockSpec((None, seq, HEADS_W), lambda b, i: (b, 0, 0))
    qspec = pl.BlockSpec((None, TQ, HEADS_W), lambda b, i: (b, i, 0))
    return pl.pallas_call(
        _sb_attn_kernel,
        grid=(batch, seq // TQ),
        in_specs=[qspec, kspec, kspec, _const_spec(tri.shape)],
        out_specs=qspec,
        out_shape=jax.ShapeDtypeStruct((batch, seq, HEADS_W), BF16),
        scratch_shapes=[pltpu.VMEM((N_HEADS, TQ, LANES), BF16), pltpu.VMEM((N_HEADS, seq // TK, TQ, TK), F32),
                        pltpu.VMEM((N_HEADS, TQ, LANES), F32), pltpu.VMEM((N_HEADS, TQ, LANES), F32)],
        compiler_params=_cparams(("parallel", "parallel")),
        name="stick_breaking_attention",
    )(r3(sq), r3(sk), r3(sv), tri)


def _merge_kernel(x_ref, ya_ref, yb_ref, yc_ref, yd_ref, wg_ref, wb_ref, wo_ref, g_ref, b_ref,
                  o_ref):
    x = x_ref[...]
    xb = x.astype(BF16)
    mixed = jnp.zeros((TM, D_MODEL), F32)
    for n, y_ref in enumerate((ya_ref, yb_ref, yc_ref, yd_ref)):
        gate = _sigmoid(_dot(xb, wg_ref[:, n * D_MODEL:(n + 1) * D_MODEL]))
        mixed = mixed + gate * _dot(y_ref[...], wb_ref[n])
    h = DEEPNORM_ALPHA * x + _dot(mixed.astype(BF16), wo_ref[...])
    o_ref[...] = _layer_norm(h, g_ref[...], b_ref[...])


def _merge(x2d, ys, wts):
    n = x2d.shape[0]
    row = lambda w: pl.BlockSpec((TM, w), lambda i: (i, 0))
    consts = [wts['w_gate'], wts['w_branch'], wts['w_out'], wts['ln1_g'], wts['ln1_b']]
    return pl.pallas_call(
        _merge_kernel,
        grid=(n // TM,),
        in_specs=[row(D_MODEL)] + [row(BRANCH_W)] * 4 + [_const_spec(c.shape) for c in consts],
        out_specs=row(D_MODEL),
        out_shape=jax.ShapeDtypeStruct((n, D_MODEL), F32),
        compiler_params=_cparams(("parallel",)),
        name="branch_merge_ln1",
    )(x2d, *ys, *consts)


def _ple_ln2(x1, x1b, f, p_ref, wpg_ref, wpp_ref, g_ref, b_ref):
    ple = _sigmoid(_dot(x1b, wpg_ref[...])) * _dot(p_ref[...].astype(BF16), wpp_ref[...])
    return _layer_norm(DEEPNORM_ALPHA * x1 + f + ple, g_ref[...], b_ref[...])


FF_CHUNK = 256


def _ffn_dense_kernel(x_ref, p_ref, wi_ref, wo_ref, wpg_ref, wpp_ref, g_ref, b_ref, o_ref, acc_ref):
    xb = x_ref[...].astype(BF16)
    for c in range(D_FF // FF_CHUNK):
        a = _dot(xb, wi_ref[:, c * FF_CHUNK:(c + 1) * FF_CHUNK])
        u = _dot(xb, wi_ref[:, D_FF + c * FF_CHUNK:D_FF + (c + 1) * FF_CHUNK])
        part = _dot((_silu(a) * u).astype(BF16), wo_ref[c * FF_CHUNK:(c + 1) * FF_CHUNK, :])
        if c == 0:
            acc_ref[...] = part
        else:
            acc_ref[...] += part
    o_ref[...] = _ple_ln2(x_ref[...], xb, acc_ref[...], p_ref, wpg_ref, wpp_ref, g_ref, b_ref)


def _ffn_dense(x1, p2d, wts):
    n = x1.shape[0]
    row = lambda w: pl.BlockSpec((TF, w), lambda i: (i, 0))
    consts = [wts['ffn_w_in'], wts['ffn_w_out'], wts['ple_w_gate'], wts['ple_w_proj'], wts['ln2_g'], wts['ln2_b']]
    return pl.pallas_call(
        _ffn_dense_kernel,
        grid=(n // TF,),
        in_specs=[row(D_MODEL), row(P_DIM)] + [_const_spec(c.shape) for c in consts],
        out_specs=row(D_MODEL),
        out_shape=jax.ShapeDtypeStruct((n, D_MODEL), F32),
        scratch_shapes=[pltpu.VMEM((TF, D_MODEL), F32)],
        compiler_params=_cparams(("parallel",)),
        name="ffn_dense_ple_ln2",
    )(x1, p2d, *consts)


TR = 512


def _moe_route_kernel(x_ref, wr_ref, tri_ref, info_ref, cnt_ref, run_ref):
    @pl.when(pl.program_id(0) == 0)
    def _():
        run_ref[...] = jnp.zeros_like(run_ref)

    xh, xl = _split_bf16(x_ref[...])
    wh, wl = _split_bf16(wr_ref[...])
    logits = _dot(xh, wh) + (_dot(xl, wh) + _dot(xh, wl))
    lane = lax.broadcasted_iota(jnp.int32, (TR, LANES), 1)
    lane_f = lane.astype(F32)
    logits = jnp.where(lane < N_EXPERTS, logits, NEG)
    m1 = jnp.max(logits, axis=-1, keepdims=True)
    i1 = jnp.min(jnp.where(logits == m1, lane_f, float(LANES)), axis=-1, keepdims=True)
    rest = jnp.where(lane_f == i1, NEG, logits)
    m2 = jnp.max(rest, axis=-1, keepdims=True)
    i2 = jnp.min(jnp.where(rest == m2, lane_f, float(LANES)), axis=-1, keepdims=True)
    e = jnp.exp(m2 - m1)
    g1 = 1.0 / (1.0 + e)
    g2 = e / (1.0 + e)
    hot1 = lane_f == i1
    hot2 = lane_f == i2
    onehot = jnp.where(hot1 | hot2, 1.0, 0.0)
    before = _dot(tri_ref[...], onehot.astype(BF16)) + run_ref[0:1, :]
    r1 = jnp.sum(jnp.where(hot1, before, 0.0), axis=-1, keepdims=True)
    r2 = jnp.sum(jnp.where(hot2, before, 0.0), axis=-1, keepdims=True)
    run_ref[0:1, :] = run_ref[0:1, :] + jnp.sum(onehot, axis=0, keepdims=True)
    info = jnp.zeros((TR, LANES), F32)
    for k, val in enumerate((i1, i2, r1, r2, g1, g2)):
        info = jnp.where(lane == k, val, info)
    info_ref[...] = info
    cnt_ref[...] = jnp.broadcast_to(run_ref[0:1, :], cnt_ref.shape)


def _moe_route(x1, w_router_pad, tri):
    n = x1.shape[0]
    return pl.pallas_call(
        _moe_route_kernel,
        grid=(n // TR,),
        in_specs=[pl.BlockSpec((TR, D_MODEL), lambda i: (i, 0)), _const_spec(w_router_pad.shape),
                  _const_spec(tri.shape)],
        out_specs=[pl.BlockSpec((TR, LANES), lambda i: (i, 0)), pl.BlockSpec((8, LANES), lambda i: (0, 0))],
        out_shape=[jax.ShapeDtypeStruct((n, LANES), F32), jax.ShapeDtypeStruct((8, LANES), F32)],
        scratch_shapes=[pltpu.VMEM((8, LANES), F32)],
        compiler_params=_cparams(("arbitrary",)),
        name="moe_router_rank",
    )(x1, w_router_pad, tri)


def _moe_rowmap_kernel(d1_ref, d2_ref, rt_ref):
    def clear(r, _):
        rt_ref[r] = 0
        return 0

    lax.fori_loop(0, rt_ref.shape[0], clear, 0, unroll=16)

    def place(t, _):
        rt_ref[d1_ref[t]] = t
        rt_ref[d2_ref[t]] = t
        return 0

    lax.fori_loop(0, d1_ref.shape[0], place, 0, unroll=16)


def _moe_rowmap(dest1, dest2, n_rows):
    smem = pl.BlockSpec(memory_space=pltpu.SMEM)
    return pl.pallas_call(
        _moe_rowmap_kernel,
        in_specs=[smem, smem],
        out_specs=smem,
        out_shape=jax.ShapeDtypeStruct((n_rows,), jnp.int32),
        name="moe_row_map",
    )(dest1, dest2)


EF_CHUNK = 1792
EF_STEPS = D_FF_EXPERT // EF_CHUNK
assert EF_STEPS >= 2
ROWS_PER_STEP = MOE_BLOCK // EF_STEPS


def _moe_ffn_kernel(be_ref, na_ref, rt_ref, x_hbm, wa_ref, wu_ref, wo_ref, ys_ref, xs_ref, acc_ref, sem):
    del be_ref
    blk, c = pl.program_id(0), pl.program_id(1)
    n_live = na_ref[0]
    last_step = pl.num_programs(1) - 1
    slot = blk % 2

    def row_copy(block, r, s):
        return pltpu.make_async_copy(x_hbm.at[pl.ds(rt_ref[block * MOE_BLOCK + r], 1), :],
                                     xs_ref.at[s, pl.ds(r, 1), :], sem.at[s])

    def wait_block(s):
        pltpu.make_async_copy(x_hbm.at[pl.ds(0, MOE_BLOCK), :], xs_ref.at[s], sem.at[s]).wait()

    @pl.when(blk < n_live)
    def _():
        @pl.when((blk == 0) & (c == 0))
        def _():
            def body(r, _):
                row_copy(0, r, 0).start()
                return 0
            lax.fori_loop(0, MOE_BLOCK, body, 0)

        @pl.when(c == 0)
        def _():
            wait_block(slot)

        xb = xs_ref[slot].astype(BF16)
        h = (_silu(_dot(xb, wa_ref[...])) * _dot(xb, wu_ref[...])).astype(BF16)
        part = _dot(h, wo_ref[...])

        nxt = jnp.minimum(blk + 1, pl.num_programs(0) - 1)
        for r in range(ROWS_PER_STEP):
            row_copy(nxt, c * ROWS_PER_STEP + r, 1 - slot).start()

        @pl.when(c == 0)
        def _():
            acc_ref[...] = part

        @pl.when((c > 0) & (c < last_step))
        def _():
            acc_ref[...] += part

        @pl.when(c == last_step)
        def _():
            ys_ref[...] = acc_ref[...] + part

        @pl.when((c == last_step) & (blk == n_live - 1))
        def _():
            wait_block(1 - slot)

    @pl.when((blk >= n_live) & (c == last_step))
    def _():
        ys_ref[...] = jnp.zeros_like(ys_ref)


def _moe_ffn(x1, row_tok, blk_expert, n_active, w_in, w_out):
    n_rows = row_tok.shape[0]
    n_blk = n_rows // MOE_BLOCK
    n_ch = EF_STEPS
    live = lambda b, na: jnp.minimum(b, na[0] - 1)
    chunk = lambda b, c, na: jnp.where(b < na[0], c, n_ch - 1)
    return pl.pallas_call(
        _moe_ffn_kernel,
        grid_spec=pltpu.PrefetchScalarGridSpec(
            num_scalar_prefetch=3,
            grid=(n_blk, n_ch),
            in_specs=[
                pl.BlockSpec(memory_space=pl.ANY),
                pl.BlockSpec((None, D_MODEL, EF_CHUNK),
                             lambda b, c, be, na, rt: (be[live(b, na)], 0, chunk(b, c, na))),
                pl.BlockSpec((None, D_MODEL, EF_CHUNK),
                             lambda b, c, be, na, rt: (be[live(b, na)], 0, n_ch + chunk(b, c, na))),
                pl.BlockSpec((None, EF_CHUNK, D_MODEL),
                             lambda b, c, be, na, rt: (be[live(b, na)], chunk(b, c, na), 0)),
            ],
            out_specs=pl.BlockSpec((MOE_BLOCK, D_MODEL), lambda b, c, be, na, rt: (b, 0)),
            scratch_shapes=[pltpu.VMEM((2, MOE_BLOCK, D_MODEL), F32), pltpu.VMEM((MOE_BLOCK, D_MODEL), F32),
                            pltpu.SemaphoreType.DMA((2,))],
        ),
        out_shape=jax.ShapeDtypeStruct((n_rows, D_MODEL), F32),
        compiler_params=_cparams(("arbitrary", "arbitrary")),
        name="moe_expert_swiglu",
    )(blk_expert, n_active, row_tok, x1, w_in, w_in, w_out)


TC = 512


def _moe_combine_kernel(d1_ref, d2_ref, ys_hbm, x_ref, p_ref, info_ref, wpg_ref, wpp_ref, g_ref,
                        b_ref, o_ref, ya_ref, yb_ref, sem):
    i = pl.program_id(0)
    last = pl.num_programs(0) - 1
    slot = i % 2

    def copies(tile, r, s):
        t = tile * TC + r
        return (pltpu.make_async_copy(ys_hbm.at[pl.ds(d1_ref[t], 1), :], ya_ref.at[s, pl.ds(r, 1), :], sem.at[0, s]),
                pltpu.make_async_copy(ys_hbm.at[pl.ds(d2_ref[t], 1), :], yb_ref.at[s, pl.ds(r, 1), :], sem.at[1, s]))

    def wait_tile(s):
        pltpu.make_async_copy(ys_hbm.at[pl.ds(0, TC), :], ya_ref.at[s], sem.at[0, s]).wait()
        pltpu.make_async_copy(ys_hbm.at[pl.ds(0, TC), :], yb_ref.at[s], sem.at[1, s]).wait()

    @pl.when(i == 0)
    def _():
        def body(r, _):
            for cp in copies(0, r, 0):
                cp.start()
            return 0
        lax.fori_loop(0, TC, body, 0)

    wait_tile(slot)
    nxt = jnp.minimum(i + 1, last)
    for r in range(TC):
        for queue, cp in enumerate(copies(nxt, r, 1 - slot)):
            cp.start(priority=queue)
    info = info_ref[...]
    f = info[:, 4:5] * ya_ref[slot] + info[:, 5:6] * yb_ref[slot]
    x = x_ref[...]
    o_ref[...] = _ple_ln2(x, x.astype(BF16), f, p_ref, wpg_ref, wpp_ref, g_ref, b_ref)

    @pl.when(i == last)
    def _():
        wait_tile(1 - slot)


def _moe_combine(ys, dest1, dest2, x1, p2d, info, wts):
    n = x1.shape[0]
    row = lambda w: pl.BlockSpec((TC, w), lambda i, d1, d2: (i, 0))
    consts = [wts['ple_w_gate'], wts['ple_w_proj'], wts['ln2_g'], wts['ln2_b']]
    cspec = lambda c: pl.BlockSpec(c.shape, lambda i, d1, d2: (0,) * c.ndim, pipeline_mode=pl.Buffered(1))
    return pl.pallas_call(
        _moe_combine_kernel,
        grid_spec=pltpu.PrefetchScalarGridSpec(
            num_scalar_prefetch=2,
            grid=(n // TC,),
            in_specs=[pl.BlockSpec(memory_space=pl.ANY), row(D_MODEL), row(P_DIM), row(LANES)]
            + [cspec(c) for c in consts],
            out_specs=row(D_MODEL),
            scratch_shapes=[pltpu.VMEM((2, TC, D_MODEL), F32), pltpu.VMEM((2, TC, D_MODEL), F32),
                            pltpu.SemaphoreType.DMA((2, 2))],
        ),
        out_shape=jax.ShapeDtypeStruct((n, D_MODEL), F32),
        compiler_params=_cparams(("arbitrary",)),
        name="moe_combine_ple_ln2",
    )(dest1, dest2, ys, x1, p2d, info, *consts)


def _moe(x1, p2d, wts):
    n = x1.shape[0]
    n_rows = ((n * 2 + MOE_BLOCK - 1) // MOE_BLOCK) * MOE_BLOCK + N_EXPERTS * MOE_BLOCK
    info, cnt = _moe_route(x1, wts['w_router'], wts['tri_tokens'])
    counts = cnt[0, :N_EXPERTS].astype(jnp.int32)
    padded = ((counts + MOE_BLOCK - 1) // MOE_BLOCK) * MOE_BLOCK
    ends = jnp.cumsum(padded)
    start_pad = ends - padded
    e1, e2 = info[:, 0].astype(jnp.int32), info[:, 1].astype(jnp.int32)
    dest1 = start_pad[e1] + info[:, 2].astype(jnp.int32)
    dest2 = start_pad[e2] + info[:, 3].astype(jnp.int32)
    n_blk = n_rows // MOE_BLOCK
    blk_row0 = jnp.arange(n_blk, dtype=jnp.int32) * MOE_BLOCK
    blk_expert = jnp.minimum(jnp.sum((ends[None, :] <= blk_row0[:, None]).astype(jnp.int32), axis=1),
                             N_EXPERTS - 1)
    n_active = (ends[-1:] // MOE_BLOCK).astype(jnp.int32)
    row_tok = _moe_rowmap(dest1, dest2, n_rows)
    ys = _moe_ffn(x1, row_tok, blk_expert, n_active, wts['moe_w_in'], wts['moe_w_out'])
    return _moe_combine(ys, dest1, dest2, x1, p2d, info, wts)


def _rot_half_cols(w, heads, dim):
    w3 = w.reshape(w.shape[0], heads, dim)
    half = dim // 2
    return jnp.concatenate([-w3[..., half:], w3[..., :half]], axis=-1).reshape(w.shape[0], heads * dim)


def _prep_layer(i, w_in, conv_w, conv_b, conv_ln_g, conv_ln_b, nsa_cmp_pe, nsa_cmp_w1, nsa_cmp_w2,
                mla_q_norm, mla_kv_norm, mla_w_uq, mla_w_ukv, w_branch, w_out, ln1_g, ln1_b,
                ple_w_gate, ple_w_proj, ln2_g, ln2_b):
    w = w_in[i]
    d = w.shape[0]
    z = lambda n: jnp.zeros((d, n), F32)
    dup = lambda a: jnp.concatenate([a, a], axis=1)
    c_glu, nq = w[:, 0:512], w[:, 512:768]
    nkv = w[:, 768:1152]
    k_cmp, v_cmp, k_slc, v_slc, k_win, v_win = [nkv[:, j * 64:(j + 1) * 64] for j in range(6)]
    ng = w[:, 1152:1164]
    mq, mkv, mkr = w[:, 1164:1420], w[:, 1420:1548], w[:, 1548:1580]
    sb = w[:, 1580:2348]
    bg = w[:, 2348:6444]
    cols = [c_glu, nq, _rot_half_cols(nq, NSA_HEADS, HEAD_DIM),
            dup(k_slc), dup(_rot_half_cols(k_slc, 1, HEAD_DIM)),
            dup(k_win), dup(_rot_half_cols(k_win, 1, HEAD_DIM)),
            dup(v_slc), dup(v_win), k_cmp, v_cmp, ng, z(LANES - 12), mq, mkv,
            z(64), mkr, z(32), z(64), _rot_half_cols(mkr, 1, MLA_ROPE), z(32), z(LANES), sb]
    w1 = jnp.concatenate(cols, axis=1).astype(BF16)
    assert w1.shape[1] == C_TOT

    inv32 = ROPE_THETA ** (-jnp.arange(HEAD_DIM // 2, dtype=F32) / (HEAD_DIM // 2))
    inv16 = ROPE_THETA ** (-jnp.arange(MLA_ROPE // 2, dtype=F32) / (MLA_ROPE // 2))
    inv_nsa = jnp.tile(inv32, 4)[None, :]
    inv_mla = jnp.concatenate([jnp.zeros((64,), F32), inv16, inv16, jnp.zeros((32,), F32)])[None, :]

    wuq = mla_w_uq[i].reshape(MLA_Q_RANK, MLA_HEADS, MLA_NOPE + MLA_ROPE)
    zq = jnp.zeros((MLA_Q_RANK, MLA_HEADS, 32), F32)
    wq = jnp.concatenate([wuq, zq], axis=-1).reshape(MLA_Q_RANK, MLA_HEADS * LANES)
    rope_rot = jnp.concatenate([-wuq[..., MLA_NOPE + 16:], wuq[..., MLA_NOPE:MLA_NOPE + 16]], axis=-1)
    wqr = jnp.concatenate([jnp.zeros((MLA_Q_RANK, MLA_HEADS, MLA_NOPE), F32), rope_rot, zq],
                          axis=-1).reshape(MLA_Q_RANK, MLA_HEADS * LANES)
    wukv = mla_w_ukv[i].reshape(MLA_KV_RANK, MLA_HEADS, MLA_NOPE + MLA_V)
    wk = jnp.concatenate([wukv[..., :MLA_NOPE], jnp.zeros((MLA_KV_RANK, MLA_HEADS, 64), F32)],
                         axis=-1).reshape(MLA_KV_RANK, MLA_HEADS * LANES)
    wv = wukv[..., MLA_NOPE:].reshape(MLA_KV_RANK, MLA_HEADS * MLA_V)

    pe = nsa_cmp_pe[i]
    pe_rows = pe.reshape(CMP_BLOCK, 2 * HEAD_DIM)
    pe_a = pe_rows[:CMP_STRIDE].reshape(1, CMP_STRIDE * LANES)
    pe_b = pe_rows[CMP_STRIDE:].reshape(1, CMP_STRIDE * LANES)
    w1c = nsa_cmp_w1[i].reshape(2, CMP_BLOCK, HEAD_DIM, HEAD_DIM)
    zblk = jnp.zeros((CMP_BLOCK, HEAD_DIM, HEAD_DIM), F32)
    w1full = jnp.concatenate([jnp.concatenate([w1c[0], zblk], axis=2),
                              jnp.concatenate([zblk, w1c[1]], axis=2)], axis=1)
    w1a = w1full[:CMP_STRIDE].reshape(CMP_STRIDE * LANES, LANES).astype(BF16)
    w1b = w1full[CMP_STRIDE:].reshape(CMP_STRIDE * LANES, LANES).astype(BF16)
    w2 = nsa_cmp_w2[i]
    z64 = jnp.zeros((HEAD_DIM, LANES), F32)
    w2k = jnp.concatenate([dup(w2[0]), z64], axis=0).astype(BF16)
    w2v = jnp.concatenate([z64, dup(w2[1])], axis=0).astype(BF16)

    return dict(
        w1=w1, inv_nsa=inv_nsa, inv_mla=inv_mla,
        mla_qn=mla_q_norm[i][None, :], mla_kvn=mla_kv_norm[i][None, :],
        wq=wq.astype(BF16), wqr=wqr.astype(BF16), wk=wk.astype(BF16), wv=wv.astype(BF16),
        conv_w=conv_w[i], conv_b=conv_b[i][None, :], conv_g=conv_ln_g[i][None, :], conv_beta=conv_ln_b[i][None, :],
        pe_a=pe_a, pe_b=pe_b, w1a=w1a, w1b=w1b, w2k=w2k, w2v=w2v,
        w_gate=bg.astype(BF16), w_branch=w_branch[i].astype(BF16), w_out=w_out[i].astype(BF16),
        ln1_g=ln1_g[i][None, :], ln1_b=ln1_b[i][None, :],
        ple_w_gate=ple_w_gate[i].astype(BF16), ple_w_proj=ple_w_proj[i].astype(BF16),
        ln2_g=ln2_g[i][None, :], ln2_b=ln2_b[i][None, :],
    )


def _tables(seq):
    n_cmp_rows = seq // CMP_STRIDE
    n_sel = seq // SEL_BLOCK
    cmp_start = jnp.arange(n_cmp_rows) * CMP_STRIDE
    sel_start = jnp.arange(LANES) * SEL_BLOCK
    n_cmp = (seq - CMP_BLOCK) // CMP_STRIDE + 1
    overlap = ((cmp_start[:, None] < sel_start[None, :] + SEL_BLOCK)
               & (cmp_start[:, None] + CMP_BLOCK > sel_start[None, :])
               & (jnp.arange(n_cmp_rows)[:, None] < n_cmp) & (jnp.arange(LANES)[None, :] < n_sel))
    kb = jnp.arange(seq // TK)[:, None, None]
    nn = jnp.arange(LANES)[None, :, None]
    ll = jnp.arange(TK)[None, None, :]
    expand = (kb * TK + ll) // SEL_BLOCK == nn
    jj = jnp.arange(TK)
    tri_keys = jnp.tile(jj[:, None] >= jj[None, :], (2, 1))
    tt = jnp.arange(TR)
    tri_tokens = tt[None, :] < tt[:, None]
    src = jnp.arange(LANES)[:, None]
    dst = jnp.arange(3 * N_HEADS * HEAD_DIM)[None, :]
    width = N_HEADS * HEAD_DIM
    gate_expand = jnp.tile((src < 3 * N_HEADS) & (src % 3 == dst // width) & (src // 3 == (dst % width) // HEAD_DIM),
                           (2, 1))
    return dict(overlap_t=overlap.T.astype(BF16), eye_q=jnp.eye(TQ, dtype=BF16), gate_expand=gate_expand.astype(BF16),
                expand=expand.astype(BF16), tri_keys=tri_keys.astype(BF16),
                tri_tokens=tri_tokens.astype(BF16))


def kernel(x, p, positions, w_in, conv_w, conv_b, conv_ln_g, conv_ln_b, nsa_cmp_pe, nsa_cmp_w1, nsa_cmp_w2,
           mla_q_norm, mla_kv_norm, mla_w_uq, mla_w_ukv, w_branch, w_out, ln1_g, ln1_b, ffn_w_in, ffn_w_out,
           moe_router, moe_w_in, moe_w_out, ple_w_gate, ple_w_proj, ln2_g, ln2_b):
    batch, seq, _ = x.shape
    n = batch * seq
    tabs = _tables(seq)
    x2d = x.reshape(n, D_MODEL)
    pos2d = positions.reshape(n, 1)
    for i in range(DEPTH):
        wts = _prep_layer(i, w_in, conv_w, conv_b, conv_ln_g, conv_ln_b, nsa_cmp_pe, nsa_cmp_w1, nsa_cmp_w2,
                          mla_q_norm, mla_kv_norm, mla_w_uq, mla_w_ukv, w_branch, w_out, ln1_g, ln1_b,
                          ple_w_gate, ple_w_proj, ln2_g, ln2_b)
        wts['overlap_t'], wts['eye_q'] = tabs['overlap_t'], tabs['eye_q']
        (conv_in, nq, nqr, ks, kw, vs, vw, kvc, ng, mq, mk, mv, sq, sk, sv) = _mixer_in(x2d, pos2d, wts)
        y_a = _conv(conv_in, wts['conv_w'], wts['conv_b'], wts['conv_g'], wts['conv_beta'], batch, seq)
        ocmp, sel = _nsa_cmp(kvc, nq, wts, batch, seq)
        y_b = _nsa_attn(nqr, ks, vs, kw, vw, sel, tabs['expand'], ocmp, ng, tabs['gate_expand'], batch, seq)
        y_c = _mla_attn(mq, mk, mv, batch, seq)
        y_d = _sb_attn(sq, sk, sv, tabs['tri_keys'], batch, seq)
        ys = [y.reshape(n, BRANCH_W) for y in (y_a, y_b, y_c, y_d)]
        x1 = _merge(x2d, ys, wts)
        p2d = p[i].reshape(n, P_DIM)
        if i % 2 == 0:
            wts['ffn_w_in'] = ffn_w_in[i // 2].astype(BF16)
            wts['ffn_w_out'] = ffn_w_out[i // 2].astype(BF16)
            x2d = _ffn_dense(x1, p2d, wts)
        else:
            wts['w_router'] = jnp.concatenate(
                [moe_router[i // 2], jnp.zeros((D_MODEL, LANES - N_EXPERTS), F32)], axis=1)
            wts['tri_tokens'] = tabs['tri_tokens']
            wts['moe_w_in'] = moe_w_in[i // 2].astype(BF16)
            wts['moe_w_out'] = moe_w_out[i // 2].astype(BF16)
            x2d = _moe(x1, p2d, wts)
    return x2d.reshape(batch, seq, D_MODEL)
```
